```python
import jax, jax.numpy as jnp
from jax import lax
import numpy as np

D_MODEL = 1024
BATCH = 2
SEQ = 8192
DEPTH = 2

GRID_W = 64
NA_HEADS = 8
NA_HEAD_DIM = 64
NA_WIDTH = NA_HEADS * NA_HEAD_DIM
NA_ROWS = 8
NA_COLS = 16
Q_BLOCK_COLS = 16
KEY_BAND_COLS = Q_BLOCK_COLS + NA_COLS
N_COL_BLOCKS = GRID_W // Q_BLOCK_COLS
F_GROUPS = 4
F_GROUP_DIM = 128
F_WIDTH = F_GROUPS * F_GROUP_DIM
IN_COLS = 3 * NA_WIDTH + F_WIDTH + 2 * D_MODEL
D_FF = 2816
RMS_EPS = 1e-6

kernel_name = "hybrid_na_fnet_macaron_encoder"


def rms_norm(x, g):
    xf = x.astype(jnp.float32)
    y = xf * lax.rsqrt(jnp.mean(xf * xf, axis=-1, keepdims=True) + RMS_EPS)
    return (y * g.astype(jnp.float32)).astype(x.dtype)


def swiglu(x, w_in, w_out):
    gu = x @ w_in
    g, u = jnp.split(gu, 2, axis=-1)
    return (jax.nn.silu(g) * u) @ w_out


def neighbourhood_attention(q, k, v, rpb):
    B, T, H, Dh = q.shape
    rows = T // GRID_W
    kh = min(NA_ROWS, rows)
    r = jnp.arange(rows)
    row_start = jnp.clip(r - kh // 2, 0, rows - kh)
    row_idx = row_start[:, None] + jnp.arange(kh)
    j = jnp.arange(N_COL_BLOCKS)
    band_start = jnp.clip(j * Q_BLOCK_COLS - NA_COLS // 2, 0, GRID_W - KEY_BAND_COLS)
    key_col = band_start[:, None] + jnp.arange(KEY_BAND_COLS)
    q_col = j[:, None] * Q_BLOCK_COLS + jnp.arange(Q_BLOCK_COLS)
    win_start = jnp.clip(q_col - NA_COLS // 2, 0, GRID_W - NA_COLS)
    col_ok = (key_col[:, None, :] >= win_start[..., None]) & (
        key_col[:, None, :] < win_start[..., None] + NA_COLS)

    qg = q.reshape(B, rows, N_COL_BLOCKS, Q_BLOCK_COLS, H, Dh)
    kg = k.reshape(B, rows, GRID_W, H, Dh)
    vg = v.reshape(B, rows, GRID_W, H, Dh)
    gi_r = row_idx[:, None, :, None]
    gi_c = key_col[None, :, None, :]
    k_win = kg[:, gi_r, gi_c]
    v_win = vg[:, gi_r, gi_c]

    dr = row_idx - r[:, None] + (NA_ROWS - 1)
    dc = jnp.clip(key_col[:, None, :] - q_col[..., None], -(NA_COLS - 1), NA_COLS - 1) + (NA_COLS - 1)
    bias = rpb[:, dr[:, None, None, :, None], dc[None, :, :, None, :]]
    bias = jnp.transpose(bias, (1, 2, 0, 3, 4, 5)).astype(jnp.float32)

    scale = Dh ** -0.5
    s = jnp.einsum('brjqhd,brjuvhd->brjhquv', qg, k_win).astype(jnp.float32) * scale + bias[None]
    s = jnp.where(col_ok[None, None, :, None, :, None, :], s, -jnp.inf)
    sh = s.shape
    p = jax.nn.softmax(s.reshape(sh[:-2] + (kh * KEY_BAND_COLS,)), axis=-1).reshape(sh)
    o = jnp.einsum('brjhquv,brjuvhd->brjqhd', p.astype(v.dtype), v_win)
    return o.reshape(B, T, H * Dh)


def fourier_mix(u):
    B, T, _ = u.shape
    ug = u.astype(jnp.float32).reshape(B, T, F_GROUPS, F_GROUP_DIM)
    f = jnp.fft.fft2(ug, axes=(1, 3), norm="ortho").real
    return f.reshape(B, T, F_WIDTH).astype(u.dtype)


def mixer(h, w_in, gate_bias, rpb, w_na_out, w_f_out, w_o):
    B, T, _ = h.shape
    z = h @ w_in
    s1 = NA_WIDTH
    q = z[..., :s1]
    k = z[..., s1:2 * s1]
    v = z[..., 2 * s1:3 * s1]
    uf = z[..., 3 * s1:3 * s1 + F_WIDTH]
    ga = z[..., 3 * s1 + F_WIDTH:3 * s1 + F_WIDTH + D_MODEL]
    gf = z[..., 3 * s1 + F_WIDTH + D_MODEL:]
    shp = (B, T, NA_HEADS, NA_HEAD_DIM)
    y_na = neighbourhood_attention(q.reshape(shp), k.reshape(shp), v.reshape(shp), rpb) @ w_na_out
    y_f = fourier_mix(uf) @ w_f_out
    g_na = jax.nn.sigmoid(ga + gate_bias[0])
    g_f = jax.nn.sigmoid(gf + gate_bias[1])
    return (g_na * y_na + g_f * y_f) @ w_o


def setup_inputs(seed: int = 0) -> dict:
    key = jax.random.key(seed)
    ks = jax.random.split(key, 20)
    L, D, F = DEPTH, D_MODEL, D_FF

    def w(k, shape, fan_in, mult=1.0):
        return jax.random.normal(k, shape, jnp.float32) * (mult * fan_in ** -0.5)

    def gain(k, shape):
        return 1.0 + 0.05 * jax.random.normal(k, shape, jnp.float32)

    return {
        "x": jax.random.normal(ks[0], (BATCH, SEQ, D), jnp.float32),
        "ffn1_norm": gain(ks[1], (L, D)),
        "ffn1_w_in": w(ks[2], (L, D, 2 * F), D),
        "ffn1_w_out": w(ks[3], (L, F, D), F),
        "mix_norm": gain(ks[4], (L, D)),
        "mix_w_in": w(ks[5], (L, D, IN_COLS), D),
        "mix_gate_bias": 0.1 * jax.random.normal(ks[6], (L, 2, D), jnp.float32),
        "na_rpb": 0.5 * jax.random.normal(ks[7], (L, NA_HEADS, 2 * NA_ROWS - 1, 2 * NA_COLS - 1), jnp.float32),
        "na_w_out": w(ks[8], (L, NA_WIDTH, D), NA_WIDTH),
        "f_w_out": w(ks[9], (L, F_WIDTH, D), F_WIDTH),
        "mix_w_o": w(ks[10], (L, D, D), D),
        "ffn2_norm": gain(ks[11], (L, D)),
        "ffn2_w_in": w(ks[12], (L, D, 2 * F), D),
        "ffn2_w_out": w(ks[13], (L, F, D), F),
        "final_norm": gain(ks[14], (D,)),
    }


def reference(x, ffn1_norm, ffn1_w_in, ffn1_w_out, mix_norm, mix_w_in, mix_gate_bias,
              na_rpb, na_w_out, f_w_out, mix_w_o, ffn2_norm, ffn2_w_in, ffn2_w_out,
              final_norm):
    for l in range(DEPTH):
        x = x + 0.5 * swiglu(rms_norm(x, ffn1_norm[l]), ffn1_w_in[l], ffn1_w_out[l])
        x = x + mixer(rms_norm(x, mix_norm[l]), mix_w_in[l], mix_gate_bias[l], na_rpb[l],
                      na_w_out[l], f_w_out[l], mix_w_o[l])
        x = x + 0.5 * swiglu(rms_norm(x, ffn2_norm[l]), ffn2_w_in[l], ffn2_w_out[l])
    return rms_norm(x, final_norm)
```

```python
import functools
import math

import jax
import jax.numpy as jnp
from jax import lax
from jax.experimental import pallas as pl
from jax.experimental.pallas import tpu as pltpu

D_MODEL = 1024
GRID_W = 64
NA_HEADS = 8
NA_HEAD_DIM = 64
NA_WIDTH = NA_HEADS * NA_HEAD_DIM
NA_ROWS = 8
NA_COLS = 16
F_GROUPS = 4
F_GROUP_DIM = 128
F_WIDTH = F_GROUPS * F_GROUP_DIM
D_FF = 2816
RMS_EPS = 1e-6

BF16 = jnp.bfloat16
F32 = jnp.float32

MXU_COLS_V7X = 256
VMEM_BYTES_V7X = 64 * 1024 * 1024

TOKEN_TILE = 512
FF_CHUNK = MXU_COLS_V7X
NA_ROW_BLOCK = 8
DFT_ROWS = 128
DFT_COLS = 64
F1_LANE_BLOCK = 4096
F3_K1_BLOCK = 8


def _params(vmem_mib):
    return pltpu.CompilerParams(
        dimension_semantics=None,
        vmem_limit_bytes=vmem_mib * 1024 * 1024,
    )


def _const_spec(shape):
    nd = len(shape)
    return pl.BlockSpec(shape, lambda *_: (0,) * nd, pipeline_mode=pl.Buffered(1))


def _rms(x, g):
    ms = jnp.mean(x * x, axis=-1, keepdims=True)
    return x * lax.rsqrt(ms + RMS_EPS) * g


def _dot(a, b):
    return jnp.dot(a, b, preferred_element_type=F32)


def _ffn_kernel(x_ref, g_ref, win_ref, wout_ref, *rest, final):
    if final:
        gf_ref, o_ref, act_ref = rest
    else:
        o_ref, act_ref = rest
    x = x_ref[...]
    h = _rms(x, g_ref[...]).astype(BF16)
    for c in range(D_FF // FF_CHUNK):
        lo = c * FF_CHUNK
        g = _dot(h, win_ref[:, lo:lo + FF_CHUNK])
        u = _dot(h, win_ref[:, D_FF + lo:D_FF + lo + FF_CHUNK])
        act_ref[:, lo:lo + FF_CHUNK] = (g * jax.nn.sigmoid(g) * u).astype(BF16)
    y = x + 0.5 * _dot(act_ref[...], wout_ref[...])
    if final:
        y = _rms(y, gf_ref[...])
    o_ref[...] = y


def _ffn(x, g, w_in, w_out, final_g=None):
    n = x.shape[0]
    final = final_g is not None
    tile = pl.BlockSpec((TOKEN_TILE, D_MODEL), lambda i: (i, 0))
    in_specs = [tile, _const_spec((1, D_MODEL)), _const_spec((D_MODEL, 2 * D_FF)),
                _const_spec((D_FF, D_MODEL))]
    args = [x, g.reshape(1, D_MODEL), w_in, w_out]
    if final:
        in_specs.append(_const_spec((1, D_MODEL)))
        args.append(final_g.reshape(1, D_MODEL))
    return pl.pallas_call(
        functools.partial(_ffn_kernel, final=final),
        grid=(n // TOKEN_TILE,),
        in_specs=in_specs,
        out_specs=tile,
        out_shape=jax.ShapeDtypeStruct((n, D_MODEL), F32),
        scratch_shapes=[pltpu.VMEM((TOKEN_TILE, D_FF), BF16)],
        compiler_params=_params(48),
        name="ffn",
    )(*args)


def _proj_kernel(x_ref, g_ref, w_ref, q_ref, k_ref, v_ref, u_ref):
    h = _rms(x_ref[...], g_ref[...]).astype(BF16)
    z = _dot(h, w_ref[...])
    s = NA_WIDTH
    q_ref[...] = (z[:, :s] * (NA_HEAD_DIM ** -0.5)).astype(BF16)
    k_ref[...] = z[:, s:2 * s].astype(BF16)
    v_ref[...] = z[:, 2 * s:3 * s].astype(BF16)
    u_ref[...] = z[:, 3 * s:3 * s + F_WIDTH].astype(BF16)


def _proj(x, g, w):
    n = x.shape[0]
    cols = w.shape[1]
    out_tile = pl.BlockSpec((TOKEN_TILE, NA_WIDTH), lambda i: (i, 0))
    out_sds = jax.ShapeDtypeStruct((n, NA_WIDTH), BF16)
    return pl.pallas_call(
        _proj_kernel,
        grid=(n // TOKEN_TILE,),
        in_specs=[pl.BlockSpec((TOKEN_TILE, D_MODEL), lambda i: (i, 0)),
                  _const_spec((1, D_MODEL)), _const_spec((D_MODEL, cols))],
        out_specs=[out_tile] * 4,
        out_shape=[out_sds] * 4,
        compiler_params=_params(40),
        name="mix_proj",
    )(x, g.reshape(1, D_MODEL), w)


def _na_bias_table(rpb, rows):
    kh = min(NA_ROWS, rows)
    half = kh // 2
    rv = jnp.concatenate([jnp.arange(half + 1), jnp.arange(rows - half + 1, rows)])
    row_start = jnp.clip(rv - half, 0, rows - kh)
    dr = row_start[:, None] + jnp.arange(kh)[None, :] - rv[:, None] + (NA_ROWS - 1)
    qc = jnp.arange(GRID_W)
    kc = jnp.arange(GRID_W)
    dc = jnp.clip(kc[None, :] - qc[:, None], -(NA_COLS - 1), NA_COLS - 1) + (NA_COLS - 1)
    ws = jnp.clip(qc - NA_COLS // 2, 0, GRID_W - NA_COLS)
    ok = (kc[None, :] >= ws[:, None]) & (kc[None, :] < ws[:, None] + NA_COLS)
    b = rpb[:, dr[:, :, None, None], dc[None, None, :, :]]
    b = jnp.where(ok[None, None, None], b, -jnp.inf).astype(F32)
    b = jnp.transpose(b, (0, 1, 3, 2, 4))
    return b.reshape(NA_HEADS, rv.shape[0], GRID_W, kh * GRID_W)


def _na_kernel(q_ref, k_ref, v_ref, bias_ref, hm_ref, o_ref, *, rows):
    blk = pl.program_id(1)
    kh = NA_ROWS
    half = kh // 2
    lane_lo = lax.broadcasted_iota(jnp.int32, (GRID_W, 128), 1) < NA_HEAD_DIM

    def row_body(i, carry):
        r = blk * NA_ROW_BLOCK + i
        rs = jnp.clip(r - half, 0, rows - kh)
        var = jnp.where(r <= half, r, jnp.where(r > rows - half, r - (rows - 2 * half), half))
        qoff = pl.multiple_of(i * GRID_W, GRID_W)
        koff = pl.multiple_of(rs * GRID_W, GRID_W)
        for p in range(NA_HEADS // 2):
            lanes = slice(p * 128, (p + 1) * 128)
            qp = q_ref[pl.ds(qoff, GRID_W), lanes]
            ql = jnp.concatenate([qp * hm_ref[0], qp * hm_ref[1]], axis=0)
            kw = k_ref[0, pl.ds(koff, kh * GRID_W), lanes]
            vw = v_ref[0, pl.ds(koff, kh * GRID_W), lanes]
            s = lax.dot_general(ql, kw, (((1,), (1,)), ((), ())),
                                preferred_element_type=F32)
            s = s + jnp.concatenate([bias_ref[2 * p, var], bias_ref[2 * p + 1, var]], axis=0)
            m = jnp.max(s, axis=-1, keepdims=True)
            e = jnp.exp(s - m)
            l = jnp.sum(e, axis=-1, keepdims=True)
            pv = _dot(e.astype(BF16), vw) / l
            o = jnp.where(lane_lo, pv[:GRID_W], pv[GRID_W:])
            o_ref[pl.ds(qoff, GRID_W), lanes] = o.astype(BF16)
        return carry

    lax.fori_loop(0, NA_ROW_BLOCK, row_body, 0)


def _na(q, k, v, bias, batch, seq):
    rows = seq // GRID_W
    nvar = bias.shape[1]
    blk_tokens = NA_ROW_BLOCK * GRID_W
    nblk = rows // NA_ROW_BLOCK
    head_mask = (jnp.arange(128)[None, None, :] // NA_HEAD_DIM
                 == jnp.arange(2)[:, None, None]).astype(BF16)
    head_mask = jnp.broadcast_to(head_mask, (2, GRID_W, 128))
    q_tile = pl.BlockSpec((blk_tokens, NA_WIDTH), lambda b, i: (b * nblk + i, 0))
    kv_spec = pl.BlockSpec((1, seq, NA_WIDTH), lambda b, i: (b, 0, 0))
    return pl.pallas_call(
        functools.partial(_na_kernel, rows=rows),
        grid=(batch, nblk),
        in_specs=[q_tile, kv_spec, kv_spec,
                  _const_spec((NA_HEADS, nvar, GRID_W, NA_ROWS * GRID_W)),
                  _const_spec((2, GRID_W, 128))],
        out_specs=q_tile,
        out_shape=jax.ShapeDtypeStruct((batch * seq, NA_WIDTH), BF16),
        compiler_params=_params(56),
        name="na",
    )(q, k.reshape(batch, seq, NA_WIDTH), v.reshape(batch, seq, NA_WIDTH), bias, head_mask)


def _dft_tables():
    two_pi = 2.0 * math.pi
    n = DFT_ROWS * DFT_COLS
    k1 = jnp.arange(DFT_ROWS)
    ang = ((k1[:, None] * k1[None, :]) % DFT_ROWS).astype(F32) * (two_pi / DFT_ROWS)
    f1 = jnp.stack([jnp.cos(ang), -jnp.sin(ang)], axis=1).reshape(2 * DFT_ROWS, DFT_ROWS)
    k2 = jnp.arange(DFT_COLS)
    t2 = jnp.arange(DFT_COLS)
    idx = (t2[None, None, :] * (DFT_ROWS * k2[None, :, None] + k1[:, None, None])) % n
    ang = idx.astype(F32) * (two_pi / n)
    gr, gi = jnp.cos(ang), -jnp.sin(ang)
    g = jnp.concatenate([jnp.concatenate([gr, -gi], axis=2),
                         jnp.concatenate([gi, gr], axis=2)], axis=1)
    c = jnp.arange(F_GROUP_DIM)
    ang = ((c[:, None] * c[None, :]) % F_GROUP_DIM).astype(F32) * (two_pi / F_GROUP_DIM)
    eye = jnp.eye(F_GROUPS, dtype=F32)
    cc = jnp.kron(eye, jnp.cos(ang))
    ss = jnp.kron(eye, jnp.sin(ang))
    return f1.astype(BF16), g.astype(BF16), cc.astype(BF16), ss.astype(BF16)


def _f1_kernel(f_ref, x_ref, o_ref):
    o_ref[0] = _dot(f_ref[...], x_ref[0]).astype(BF16)


def _f3_kernel(a_ref, g_ref, cc_ref, ss_ref, o_ref, zr_ref, zi_ref):
    for j in range(F3_K1_BLOCK):
        z = _dot(g_ref[j], a_ref[0, j])
        zr_ref[j * DFT_COLS:(j + 1) * DFT_COLS] = z[:DFT_COLS].astype(BF16)
        zi_ref[j * DFT_COLS:(j + 1) * DFT_COLS] = z[DFT_COLS:].astype(BF16)
    y = _dot(zr_ref[...], cc_ref[...]) + _dot(zi_ref[...], ss_ref[...])
    scale = 1.0 / math.sqrt(DFT_ROWS * DFT_COLS * F_GROUP_DIM)
    y = (y * scale).astype(BF16)
    for j in range(F3_K1_BLOCK):
        o_ref[0, j] = y[j * DFT_COLS:(j + 1) * DFT_COLS]


def _fourier(u, tables, batch, seq):
    f1, g, cc, ss = tables
    lanes = DFT_COLS * F_WIDTH
    a = pl.pallas_call(
        _f1_kernel,
        grid=(batch, lanes // F1_LANE_BLOCK),
        in_specs=[_const_spec((2 * DFT_ROWS, DFT_ROWS)),
                  pl.BlockSpec((1, DFT_ROWS, F1_LANE_BLOCK), lambda b, i: (b, 0, i))],
        out_specs=pl.BlockSpec((1, 2 * DFT_ROWS, F1_LANE_BLOCK), lambda b, i: (b, 0, i)),
        out_shape=jax.ShapeDtypeStruct((batch, 2 * DFT_ROWS, lanes), BF16),
        compiler_params=_params(32),
        name="dft_rows",
    )(f1, u.reshape(batch, DFT_ROWS, lanes))
    a = a.reshape(batch, DFT_ROWS, 2 * DFT_COLS, F_WIDTH)
    y = pl.pallas_call(
        _f3_kernel,
        grid=(batch, DFT_ROWS // F3_K1_BLOCK),
        in_specs=[pl.BlockSpec((1, F3_K1_BLOCK, 2 * DFT_COLS, F_WIDTH), lambda b, i: (b, i, 0, 0)),
                  pl.BlockSpec((F3_K1_BLOCK, 2 * DFT_COLS, 2 * DFT_COLS), lambda b, i: (i, 0, 0)),
                  _const_spec((F_WIDTH, F_WIDTH)), _const_spec((F_WIDTH, F_WIDTH))],
        out_specs=pl.BlockSpec((1, F3_K1_BLOCK, DFT_COLS, F_WIDTH), lambda b, i: (b, i, 0, 0)),
        out_shape=jax.ShapeDtypeStruct((batch, DFT_ROWS, DFT_COLS, F_WIDTH), BF16),
        scratch_shapes=[pltpu.VMEM((F3_K1_BLOCK * DFT_COLS, F_WIDTH), BF16),
                        pltpu.VMEM((F3_K1_BLOCK * DFT_COLS, F_WIDTH), BF16)],
        compiler_params=_params(32),
        name="dft_cols_channels",
    )(a, g, cc, ss)
    return jnp.transpose(y, (0, 2, 1, 3)).reshape(batch * seq, F_WIDTH)


def _mixout_kernel(x_ref, g_ref, att_ref, fou_ref, wg_ref, gb_ref, wna_ref, wf_ref, wo_ref, o_ref):
    x = x_ref[...]
    h = _rms(x, g_ref[...]).astype(BF16)
    gates = jax.nn.sigmoid(_dot(h, wg_ref[...]) + gb_ref[...])
    y_na = _dot(att_ref[...], wna_ref[...])
    y_f = _dot(fou_ref[...], wf_ref[...])
    m = gates[:, :D_MODEL] * y_na + gates[:, D_MODEL:] * y_f
    o_ref[...] = x + _dot(m.astype(BF16), wo_ref[...])


def _mixout(x, g, att, fou, w_gate, gate_bias, w_na, w_f, w_o):
    n = x.shape[0]
    tile = pl.BlockSpec((TOKEN_TILE, D_MODEL), lambda i: (i, 0))
    half = pl.BlockSpec((TOKEN_TILE, NA_WIDTH), lambda i: (i, 0))
    return pl.pallas_call(
        _mixout_kernel,
        grid=(n // TOKEN_TILE,),
        in_specs=[tile, _const_spec((1, D_MODEL)), half, half,
                  _const_spec((D_MODEL, 2 * D_MODEL)), _const_spec((1, 2 * D_MODEL)),
                  _const_spec((NA_WIDTH, D_MODEL)), _const_spec((F_WIDTH, D_MODEL)),
                  _const_spec((D_MODEL, D_MODEL))],
        out_specs=tile,
        out_shape=jax.ShapeDtypeStruct((n, D_MODEL), F32),
        compiler_params=_params(48),
        name="mix_out",
    )(x, g.reshape(1, D_MODEL), att, fou, w_gate, gate_bias.reshape(1, 2 * D_MODEL), w_na, w_f, w_o)


def kernel(x, ffn1_norm, ffn1_w_in, ffn1_w_out, mix_norm, mix_w_in, mix_gate_bias, na_rpb,
           na_w_out, f_w_out, mix_w_o, ffn2_norm, ffn2_w_in, ffn2_w_out, final_norm):
    batch, seq, d = x.shape
    depth = ffn1_norm.shape[0]
    assert d == D_MODEL and seq == DFT_ROWS * DFT_COLS and seq % (GRID_W * NA_ROW_BLOCK) == 0
    assert (batch * seq) % TOKEN_TILE == 0
    rows = seq // GRID_W
    n_qkvu = 3 * NA_WIDTH + F_WIDTH
    tables = _dft_tables()
    xs = x.reshape(batch * seq, d)
    for l in range(depth):
        xs = _ffn(xs, ffn1_norm[l], ffn1_w_in[l].astype(BF16), ffn1_w_out[l].astype(BF16))
        w_in = mix_w_in[l].astype(BF16)
        q, k, v, u = _proj(xs, mix_norm[l], w_in[:, :n_qkvu])
        att = _na(q, k, v, _na_bias_table(na_rpb[l], rows), batch, seq)
        fou = _fourier(u, tables, batch, seq)
        xs = _mixout(xs, mix_norm[l], att, fou, w_in[:, n_qkvu:], mix_gate_bias[l],
                     na_w_out[l].astype(BF16), f_w_out[l].astype(BF16), mix_w_o[l].astype(BF16))
        last = l == depth - 1
        xs = _ffn(xs, ffn2_norm[l], ffn2_w_in[l].astype(BF16), ffn2_w_out[l].astype(BF16),
                  final_g=final_norm if last else None)
    return xs.reshape(batch, seq, d)
```

```python
import functools
import math

import jax
import jax.numpy as jnp
from jax import lax
from jax.experimental import pallas as pl
from jax.experimental.pallas import tpu as pltpu

D_MODEL = 1024
GRID_W = 64
NA_HEADS = 8
NA_HEAD_DIM = 64
NA_WIDTH = NA_HEADS * NA_HEAD_DIM
NA_ROWS = 8
NA_COLS = 16
F_GROUPS = 4
F_GROUP_DIM = 128
F_WIDTH = F_GROUPS * F_GROUP_DIM
D_FF = 2816
RMS_EPS = 1e-6

BF16 = jnp.bfloat16
F32 = jnp.float32

MXU_COLS_V7X = 256
VMEM_BYTES_V7X = 64 * 1024 * 1024

TOKEN_TILE = 512
FF_CHUNK = MXU_COLS_V7X
NA_ROW_BLOCK = 8
DFT_ROWS = 128
DFT_COLS = 64
F1_LANE_BLOCK = 4096
F3_K1_BLOCK = 8


def _params(vmem_mib):
    return pltpu.CompilerParams(
        dimension_semantics=None,
        vmem_limit_bytes=vmem_mib * 1024 * 1024,
    )


def _const_spec(shape):
    nd = len(shape)
    return pl.BlockSpec(shape, lambda *_: (0,) * nd, pipeline_mode=pl.Buffered(1))


def _rms(x, g):
    ms = jnp.mean(x * x, axis=-1, keepdims=True)
    return x * lax.rsqrt(ms + RMS_EPS) * g


def _dot(a, b):
    return jnp.dot(a, b, preferred_element_type=F32)


def _ffn_kernel(x_ref, g_ref, win_ref, wout_ref, *rest, final):
    if final:
        gf_ref, o_ref, act_ref = rest
    else:
        o_ref, act_ref = rest
    x = x_ref[...]
    h = _rms(x, g_ref[...]).astype(BF16)
    for c in range(D_FF // FF_CHUNK):
        lo = c * FF_CHUNK
        g = _dot(h, win_ref[:, lo:lo + FF_CHUNK])
        u = _dot(h, win_ref[:, D_FF + lo:D_FF + lo + FF_CHUNK])
        act_ref[:, lo:lo + FF_CHUNK] = (g * jax.nn.sigmoid(g) * u).astype(BF16)
    y = x + 0.5 * _dot(act_ref[...], wout_ref[...])
    if final:
        y = _rms(y, gf_ref[...])
    o_ref[...] = y


def _ffn(x, g, w_in, w_out, final_g=None):
    n = x.shape[0]
    final = final_g is not None
    tile = pl.BlockSpec((TOKEN_TILE, D_MODEL), lambda i: (i, 0))
    in_specs = [tile, _const_spec((1, D_MODEL)), _const_spec((D_MODEL, 2 * D_FF)),
                _const_spec((D_FF, D_MODEL))]
    args = [x, g.reshape(1, D_MODEL), w_in, w_out]
    if final:
        in_specs.append(_const_spec((1, D_MODEL)))
        args.append(final_g.reshape(1, D_MODEL))
    return pl.pallas_call(
        functools.partial(_ffn_kernel, final=final),
        grid=(n // TOKEN_TILE,),
        in_specs=in_specs,
        out_specs=tile,
        out_shape=jax.ShapeDtypeStruct((n, D_MODEL), F32),
        scratch_shapes=[pltpu.VMEM((TOKEN_TILE, D_FF), BF16)],
        compiler_params=_params(48),
        name="ffn",
    )(*args)


def _proj_kernel(x_ref, g_ref, w_ref, q_ref, k_ref, v_ref, u_ref):
    h = _rms(x_ref[...], g_ref[...]).astype(BF16)
    z = _dot(h, w_ref[...])
    s = NA_WIDTH
    q_ref[...] = (z[:, :s] * (NA_HEAD_DIM ** -0.5)).astype(BF16)
    k_ref[...] = z[:, s:2 * s].astype(BF16)
    v_ref[...] = z[:, 2 * s:3 * s].astype(BF16)
    u_ref[...] = z[:, 3 * s:3 * s + F_WIDTH].astype(BF16)


def _proj(x, g, w):
    n = x.shape[0]
    cols = w.shape[1]
    out_tile = pl.BlockSpec((TOKEN_TILE, NA_WIDTH), lambda i: (i, 0))
    out_sds = jax.ShapeDtypeStruct((n, NA_WIDTH), BF16)
    return pl.pallas_call(
        _proj_kernel,
        grid=(n // TOKEN_TILE,),
        in_specs=[pl.BlockSpec((TOKEN_TILE, D_MODEL), lambda i: (i, 0)),
                  _const_spec((1, D_MODEL)), _const_spec((D_MODEL, cols))],
        out_specs=[out_tile] * 4,
        out_shape=[out_sds] * 4,
        compiler_params=_params(40),
        name="mix_proj",
    )(x, g.reshape(1, D_MODEL), w)


def _na_bias_table(rpb, rows):
    kh = min(NA_ROWS, rows)
    half = kh // 2
    rep_rows = list(range(half + 1)) + list(range(rows - half + 1, rows))
    qc = jnp.arange(GRID_W)
    kc = jnp.arange(GRID_W)
    ws = jnp.clip(qc - NA_COLS // 2, 0, GRID_W - NA_COLS)
    ok = (kc[None, :] >= ws[:, None]) & (kc[None, :] < ws[:, None] + NA_COLS)
    pad = GRID_W - NA_COLS
    padded = jnp.pad(rpb.astype(F32), ((0, 0), (0, 0), (pad, pad)))
    toep = jnp.stack([padded[:, :, GRID_W - 1 - q:2 * GRID_W - 1 - q] for q in range(GRID_W)],
                     axis=2)
    toep = jnp.where(ok[None, None], toep, -jnp.inf)
    variants = []
    for r in rep_rows:
        dr0 = min(max(r - half, 0), rows - kh) - r + (NA_ROWS - 1)
        variants.append(toep[:, dr0:dr0 + kh])
    b = jnp.stack(variants, axis=1)
    b = jnp.transpose(b, (0, 1, 3, 2, 4))
    return b.reshape(NA_HEADS, len(rep_rows), GRID_W, kh * GRID_W)


def _na_kernel(q_ref, k_ref, v_ref, bias_ref, hm_ref, o_ref, *, rows):
    blk = pl.program_id(1)
    kh = NA_ROWS
    half = kh // 2
    lane_lo = lax.broadcasted_iota(jnp.int32, (GRID_W, 128), 1) < NA_HEAD_DIM

    def row_body(i, carry):
        r = blk * NA_ROW_BLOCK + i
        rs = jnp.clip(r - half, 0, rows - kh)
        var = jnp.where(r <= half, r, jnp.where(r > rows - half, r - (rows - 2 * half), half))
        qoff = pl.multiple_of(i * GRID_W, GRID_W)
        koff = pl.multiple_of(rs * GRID_W, GRID_W)
        for p in range(NA_HEADS // 2):
            lanes = slice(p * 128, (p + 1) * 128)
            qp = q_ref[pl.ds(qoff, GRID_W), lanes]
            ql = jnp.concatenate([qp * hm_ref[0], qp * hm_ref[1]], axis=0)
            kw = k_ref[0, pl.ds(koff, kh * GRID_W), lanes]
            vw = v_ref[0, pl.ds(koff, kh * GRID_W), lanes]
            s = lax.dot_general(ql, kw, (((1,), (1,)), ((), ())),
                                preferred_element_type=F32)
            s = s + jnp.concatenate([bias_ref[2 * p, var], bias_ref[2 * p + 1, var]], axis=0)
            m = jnp.max(s, axis=-1, keepdims=True)
            e = jnp.exp(s - m)
            l = jnp.sum(e, axis=-1, keepdims=True)
            pv = _dot(e.astype(BF16), vw) / l
            o = jnp.where(lane_lo, pv[:GRID_W], pv[GRID_W:])
            o_ref[pl.ds(qoff, GRID_W), lanes] = o.astype(BF16)
        return carry

    lax.fori_loop(0, NA_ROW_BLOCK, row_body, 0)


def _na(q, k, v, bias, batch, seq):
    rows = seq // GRID_W
    nvar = bias.shape[1]
    blk_tokens = NA_ROW_BLOCK * GRID_W
    nblk = rows // NA_ROW_BLOCK
    head_mask = (jnp.arange(128)[None, None, :] // NA_HEAD_DIM
                 == jnp.arange(2)[:, None, None]).astype(BF16)
    head_mask = jnp.broadcast_to(head_mask, (2, GRID_W, 128))
    q_tile = pl.BlockSpec((blk_tokens, NA_WIDTH), lambda b, i: (b * nblk + i, 0))
    kv_spec = pl.BlockSpec((1, seq, NA_WIDTH), lambda b, i: (b, 0, 0))
    return pl.pallas_call(
        functools.partial(_na_kernel, rows=rows),
        grid=(batch, nblk),
        in_specs=[q_tile, kv_spec, kv_spec,
                  _const_spec((NA_HEADS, nvar, GRID_W, NA_ROWS * GRID_W)),
                  _const_spec((2, GRID_W, 128))],
        out_specs=q_tile,
        out_shape=jax.ShapeDtypeStruct((batch * seq, NA_WIDTH), BF16),
        compiler_params=_params(56),
        name="na",
    )(q, k.reshape(batch, seq, NA_WIDTH), v.reshape(batch, seq, NA_WIDTH), bias, head_mask)


def _dft_tables():
    two_pi = 2.0 * math.pi
    n = DFT_ROWS * DFT_COLS
    k1 = jnp.arange(DFT_ROWS)
    ang = ((k1[:, None] * k1[None, :]) % DFT_ROWS).astype(F32) * (two_pi / DFT_ROWS)
    f1 = jnp.stack([jnp.cos(ang), -jnp.sin(ang)], axis=1).reshape(2 * DFT_ROWS, DFT_ROWS)
    k2 = jnp.arange(DFT_COLS)
    t2 = jnp.arange(DFT_COLS)
    idx = (t2[None, None, :] * (DFT_ROWS * k2[None, :, None] + k1[:, None, None])) % n
    ang = idx.astype(F32) * (two_pi / n)
    gr, gi = jnp.cos(ang), -jnp.sin(ang)
    g = jnp.concatenate([jnp.concatenate([gr, -gi], axis=2),
                         jnp.concatenate([gi, gr], axis=2)], axis=1)
    c = jnp.arange(F_GROUP_DIM)
    ang = ((c[:, None] * c[None, :]) % F_GROUP_DIM).astype(F32) * (two_pi / F_GROUP_DIM)
    eye = jnp.eye(F_GROUPS, dtype=F32)
    cc = jnp.kron(eye, jnp.cos(ang))
    ss = jnp.kron(eye, jnp.sin(ang))
    return f1.astype(BF16), g.astype(BF16), cc.astype(BF16), ss.astype(BF16)


def _f1_kernel(f_ref, x_ref, o_ref):
    o_ref[0] = _dot(f_ref[...], x_ref[0]).astype(BF16)


def _f3_kernel(a_ref, g_ref, cc_ref, ss_ref, o_ref, zr_ref, zi_ref):
    for j in range(F3_K1_BLOCK):
        z = _dot(g_ref[j], a_ref[0, j])
        zr_ref[j * DFT_COLS:(j + 1) * DFT_COLS] = z[:DFT_COLS].astype(BF16)
        zi_ref[j * DFT_COLS:(j + 1) * DFT_COLS] = z[DFT_COLS:].astype(BF16)
    y = _dot(zr_ref[...], cc_ref[...]) + _dot(zi_ref[...], ss_ref[...])
    scale = 1.0 / math.sqrt(DFT_ROWS * DFT_COLS * F_GROUP_DIM)
    y = (y * scale).astype(BF16)
    for j in range(F3_K1_BLOCK):
        o_ref[0, j] = y[j * DFT_COLS:(j + 1) * DFT_COLS]


def _fourier(u, tables, batch, seq):
    f1, g, cc, ss = tables
    lanes = DFT_COLS * F_WIDTH
    a = pl.pallas_call(
        _f1_kernel,
        grid=(batch, lanes // F1_LANE_BLOCK),
        in_specs=[_const_spec((2 * DFT_ROWS, DFT_ROWS)),
                  pl.BlockSpec((1, DFT_ROWS, F1_LANE_BLOCK), lambda b, i: (b, 0, i))],
        out_specs=pl.BlockSpec((1, 2 * DFT_ROWS, F1_LANE_BLOCK), lambda b, i: (b, 0, i)),
        out_shape=jax.ShapeDtypeStruct((batch, 2 * DFT_ROWS, lanes), BF16),
        compiler_params=_params(32),
        name="dft_rows",
    )(f1, u.reshape(batch, DFT_ROWS, lanes))
    a = a.reshape(batch, DFT_ROWS, 2 * DFT_COLS, F_WIDTH)
    y = pl.pallas_call(
        _f3_kernel,
        grid=(batch, DFT_ROWS // F3_K1_BLOCK),
        in_specs=[pl.BlockSpec((1, F3_K1_BLOCK, 2 * DFT_COLS, F_WIDTH), lambda b, i: (b, i, 0, 0)),
                  pl.BlockSpec((F3_K1_BLOCK, 2 * DFT_COLS, 2 * DFT_COLS), lambda b, i: (i, 0, 0)),
                  _const_spec((F_WIDTH, F_WIDTH)), _const_spec((F_WIDTH, F_WIDTH))],
        out_specs=pl.BlockSpec((1, F3_K1_BLOCK, DFT_COLS, F_WIDTH), lambda b, i: (b, i, 0, 0)),
        out_shape=jax.ShapeDtypeStruct((batch, DFT_ROWS, DFT_COLS, F_WIDTH), BF16),
        scratch_shapes=[pltpu.VMEM((F3_K1_BLOCK * DFT_COLS, F_WIDTH), BF16),
                        pltpu.VMEM((F3_K1_BLOCK * DFT_COLS, F_WIDTH), BF16)],
        compiler_params=_params(32),
        name="dft_cols_channels",
    )(a, g, cc, ss)
    return jnp.transpose(y, (0, 2, 1, 3)).reshape(batch * seq, F_WIDTH)


def _mixout_kernel(x_ref, g_ref, att_ref, fou_ref, wg_ref, gb_ref, wna_ref, wf_ref, wo_ref, o_ref):
    x = x_ref[...]
    h = _rms(x, g_ref[...]).astype(BF16)
    gates = jax.nn.sigmoid(_dot(h, wg_ref[...]) + gb_ref[...])
    y_na = _dot(att_ref[...], wna_ref[...])
    y_f = _dot(fou_ref[...], wf_ref[...])
    m = gates[:, :D_MODEL] * y_na + gates[:, D_MODEL:] * y_f
    o_ref[...] = x + _dot(m.astype(BF16), wo_ref[...])


def _mixout(x, g, att, fou, w_gate, gate_bias, w_na, w_f, w_o):
    n = x.shape[0]
    tile = pl.BlockSpec((TOKEN_TILE, D_MODEL), lambda i: (i, 0))
    half = pl.BlockSpec((TOKEN_TILE, NA_WIDTH), lambda i: (i, 0))
    return pl.pallas_call(
        _mixout_kernel,
        grid=(n // TOKEN_TILE,),
        in_specs=[tile, _const_spec((1, D_MODEL)), half, half,
                  _const_spec((D_MODEL, 2 * D_MODEL)), _const_spec((1, 2 * D_MODEL)),
                  _const_spec((NA_WIDTH, D_MODEL)), _const_spec((F_WIDTH, D_MODEL)),
                  _const_spec((D_MODEL, D_MODEL))],
        out_specs=tile,
        out_shape=jax.ShapeDtypeStruct((n, D_MODEL), F32),
        compiler_params=_params(48),
        name="mix_out",
    )(x, g.reshape(1, D_MODEL), att, fou, w_gate, gate_bias.reshape(1, 2 * D_MODEL), w_na, w_f, w_o)


def kernel(x, ffn1_norm, ffn1_w_in, ffn1_w_out, mix_norm, mix_w_in, mix_gate_bias, na_rpb,
           na_w_out, f_w_out, mix_w_o, ffn2_norm, ffn2_w_in, ffn2_w_out, final_norm):
    batch, seq, d = x.shape
    depth = ffn1_norm.shape[0]
    assert d == D_MODEL and seq == DFT_ROWS * DFT_COLS and seq % (GRID_W * NA_ROW_BLOCK) == 0
    assert (batch * seq) % TOKEN_TILE == 0
    rows = seq // GRID_W
    n_qkvu = 3 * NA_WIDTH + F_WIDTH
    tables = _dft_tables()
    xs = x.reshape(batch * seq, d)
    for l in range(depth):
        xs = _ffn(xs, ffn1_norm[l], ffn1_w_in[l].astype(BF16), ffn1_w_out[l].astype(BF16))
        w_in = mix_w_in[l].astype(BF16)
        q, k, v, u = _proj(xs, mix_norm[l], w_in[:, :n_qkvu])
        att = _na(q, k, v, _na_bias_table(na_rpb[l], rows), batch, seq)
        fou = _fourier(u, tables, batch, seq)
        xs = _mixout(xs, mix_norm[l], att, fou, w_in[:, n_qkvu:], mix_gate_bias[l],
                     na_w_out[l].astype(BF16), f_w_out[l].astype(BF16), mix_w_o[l].astype(BF16))
        last = l == depth - 1
        xs = _ffn(xs, ffn2_norm[l], ffn2_w_in[l].astype(BF16), ffn2_w_out[l].astype(BF16),
                  final_g=final_norm if last else None)
    return xs.reshape(batch, seq, d)
```

```python
import functools
import math

import jax
import jax.numpy as jnp
from jax import lax
from jax.experimental import pallas as pl
from jax.experimental.pallas import tpu as pltpu

D_MODEL = 1024
GRID_W = 64
NA_HEADS = 8
NA_HEAD_DIM = 64
NA_WIDTH = NA_HEADS * NA_HEAD_DIM
NA_ROWS = 8
NA_COLS = 16
F_GROUPS = 4
F_GROUP_DIM = 128
F_WIDTH = F_GROUPS * F_GROUP_DIM
D_FF = 2816
RMS_EPS = 1e-6

BF16 = jnp.bfloat16
F32 = jnp.float32

MXU_COLS_V7X = 256
VMEM_BYTES_V7X = 64 * 1024 * 1024

TOKEN_TILE = 512
FF_CHUNK = MXU_COLS_V7X
NA_ROW_BLOCK = 8
DFT_ROWS = 128
DFT_COLS = 64
F1_LANE_BLOCK = 4096
F3_K1_BLOCK = 8


def _params(vmem_mib):
    return pltpu.CompilerParams(
        dimension_semantics=None,
        vmem_limit_bytes=vmem_mib * 1024 * 1024,
    )


def _const_spec(shape):
    nd = len(shape)
    return pl.BlockSpec(shape, lambda *_: (0,) * nd, pipeline_mode=pl.Buffered(1))


def _layer_spec(shape, layer, col_block=0):
    nd = len(shape)
    index = (layer,) + (0,) * (nd - 1) + (col_block,)
    return pl.BlockSpec((None,) + tuple(shape), lambda *_: index, pipeline_mode=pl.Buffered(1))


def _rms(x, g):
    ms = jnp.mean(x * x, axis=-1, keepdims=True)
    return x * lax.rsqrt(ms + RMS_EPS) * g


def _dot(a, b):
    return jnp.dot(a, b, preferred_element_type=F32)


def _ffn_kernel(x_ref, g_ref, win_ref, wout_ref, *rest, final):
    if final:
        gf_ref, o_ref, act_ref = rest
    else:
        o_ref, act_ref = rest
    x = x_ref[...]
    h = _rms(x, g_ref[...]).astype(BF16)
    for c in range(D_FF // FF_CHUNK):
        lo = c * FF_CHUNK
        g = _dot(h, win_ref[:, lo:lo + FF_CHUNK])
        u = _dot(h, win_ref[:, D_FF + lo:D_FF + lo + FF_CHUNK])
        act_ref[:, lo:lo + FF_CHUNK] = (g * jax.nn.sigmoid(g) * u).astype(BF16)
    y = x + 0.5 * _dot(act_ref[...], wout_ref[...])
    if final:
        y = _rms(y, gf_ref[...])
    o_ref[...] = y


def _ffn(x, layer, g, w_in, w_out, final_g=None):
    n = x.shape[0]
    final = final_g is not None
    tile = pl.BlockSpec((TOKEN_TILE, D_MODEL), lambda i: (i, 0))
    in_specs = [tile, _layer_spec((1, D_MODEL), layer), _layer_spec((D_MODEL, 2 * D_FF), layer),
                _layer_spec((D_FF, D_MODEL), layer)]
    args = [x, g, w_in, w_out]
    if final:
        in_specs.append(_const_spec((1, D_MODEL)))
        args.append(final_g.reshape(1, D_MODEL))
    return pl.pallas_call(
        functools.partial(_ffn_kernel, final=final),
        grid=(n // TOKEN_TILE,),
        in_specs=in_specs,
        out_specs=tile,
        out_shape=jax.ShapeDtypeStruct((n, D_MODEL), F32),
        scratch_shapes=[pltpu.VMEM((TOKEN_TILE, D_FF), BF16)],
        compiler_params=_params(48),
        name="ffn",
    )(*args)


def _proj_kernel(x_ref, g_ref, w_ref, q_ref, k_ref, v_ref, u_ref):
    h = _rms(x_ref[...], g_ref[...]).astype(BF16)
    z = _dot(h, w_ref[...])
    s = NA_WIDTH
    q_ref[...] = (z[:, :s] * (NA_HEAD_DIM ** -0.5)).astype(BF16)
    k_ref[...] = z[:, s:2 * s].astype(BF16)
    v_ref[...] = z[:, 2 * s:3 * s].astype(BF16)
    u_ref[...] = z[:, 3 * s:3 * s + F_WIDTH].astype(BF16)


def _proj(x, layer, g, w):
    n = x.shape[0]
    cols = 3 * NA_WIDTH + F_WIDTH
    assert w.shape[2] == 2 * cols
    out_tile = pl.BlockSpec((TOKEN_TILE, NA_WIDTH), lambda i: (i, 0))
    out_sds = jax.ShapeDtypeStruct((n, NA_WIDTH), BF16)
    return pl.pallas_call(
        _proj_kernel,
        grid=(n // TOKEN_TILE,),
        in_specs=[pl.BlockSpec((TOKEN_TILE, D_MODEL), lambda i: (i, 0)),
                  _layer_spec((1, D_MODEL), layer), _layer_spec((D_MODEL, cols), layer, 0)],
        out_specs=[out_tile] * 4,
        out_shape=[out_sds] * 4,
        compiler_params=_params(40),
        name="mix_proj",
    )(x, g, w)


def _na_bias_table(rpb, rows):
    kh = min(NA_ROWS, rows)
    half = kh // 2
    rep_rows = list(range(half + 1)) + list(range(rows - half + 1, rows))
    qc = jnp.arange(GRID_W)
    kc = jnp.arange(GRID_W)
    ws = jnp.clip(qc - NA_COLS // 2, 0, GRID_W - NA_COLS)
    ok = (kc[None, :] >= ws[:, None]) & (kc[None, :] < ws[:, None] + NA_COLS)
    pad = GRID_W - NA_COLS
    padded = jnp.pad(rpb.astype(F32), ((0, 0), (0, 0), (pad, pad)))
    toep = jnp.stack([padded[:, :, GRID_W - 1 - q:2 * GRID_W - 1 - q] for q in range(GRID_W)],
                     axis=2)
    toep = jnp.where(ok[None, None], toep, -jnp.inf)
    variants = []
    for r in rep_rows:
        dr0 = min(max(r - half, 0), rows - kh) - r + (NA_ROWS - 1)
        variants.append(jnp.concatenate([toep[:, dr0 + u] for u in range(kh)], axis=-1))
    return jnp.stack(variants, axis=1)


def _na_kernel(q_ref, k_ref, v_ref, bias_ref, hm_ref, o_ref, *, rows):
    blk = pl.program_id(1)
    kh = NA_ROWS
    half = kh // 2
    lane_lo = lax.broadcasted_iota(jnp.int32, (GRID_W, 128), 1) < NA_HEAD_DIM

    def window(i):
        r = blk * NA_ROW_BLOCK + i
        rs = jnp.clip(r - half, 0, rows - kh)
        var = jnp.where(r <= half, r, jnp.where(r > rows - half, r - (rows - 2 * half), half))
        return pl.multiple_of(rs * GRID_W, GRID_W), var

    def score_stage(i):
        koff, _ = window(i)
        ss = []
        for p in range(NA_HEADS // 2):
            lanes = slice(p * 128, (p + 1) * 128)
            qp = q_ref[i * GRID_W:(i + 1) * GRID_W, lanes]
            ql = jnp.concatenate([qp * hm_ref[0], qp * hm_ref[1]], axis=0)
            kw = k_ref[0, pl.ds(koff, kh * GRID_W), lanes]
            ss.append(lax.dot_general(ql, kw, (((1,), (1,)), ((), ())),
                                      preferred_element_type=F32))
        return ss

    def output_stage(i, ss):
        koff, var = window(i)
        es, ls = [], []
        for p in range(NA_HEADS // 2):
            s = ss[p] + jnp.concatenate([bias_ref[2 * p, var], bias_ref[2 * p + 1, var]], axis=0)
            m = jnp.max(s, axis=-1, keepdims=True)
            e = jnp.exp(s - m)
            ls.append(jnp.sum(e, axis=-1, keepdims=True))
            es.append(e.astype(BF16))
        for p in range(NA_HEADS // 2):
            lanes = slice(p * 128, (p + 1) * 128)
            vw = v_ref[0, pl.ds(koff, kh * GRID_W), lanes]
            pv = _dot(es[p], vw) / ls[p]
            o = jnp.where(lane_lo, pv[:GRID_W], pv[GRID_W:])
            o_ref[i * GRID_W:(i + 1) * GRID_W, lanes] = o.astype(BF16)

    ss = score_stage(0)
    for i in range(NA_ROW_BLOCK):
        nxt = score_stage(i + 1) if i + 1 < NA_ROW_BLOCK else None
        output_stage(i, ss)
        ss = nxt


def _na(q, k, v, bias, batch, seq):
    rows = seq // GRID_W
    nvar = bias.shape[1]
    blk_tokens = NA_ROW_BLOCK * GRID_W
    nblk = rows // NA_ROW_BLOCK
    head_mask = (jnp.arange(128)[None, None, :] // NA_HEAD_DIM
                 == jnp.arange(2)[:, None, None]).astype(BF16)
    head_mask = jnp.broadcast_to(head_mask, (2, GRID_W, 128))
    q_tile = pl.BlockSpec((blk_tokens, NA_WIDTH), lambda b, i: (b * nblk + i, 0))
    kv_spec = pl.BlockSpec((1, seq, NA_WIDTH), lambda b, i: (b, 0, 0))
    return pl.pallas_call(
        functools.partial(_na_kernel, rows=rows),
        grid=(batch, nblk),
        in_specs=[q_tile, kv_spec, kv_spec,
                  _const_spec((NA_HEADS, nvar, GRID_W, NA_ROWS * GRID_W)),
                  _const_spec((2, GRID_W, 128))],
        out_specs=q_tile,
        out_shape=jax.ShapeDtypeStruct((batch * seq, NA_WIDTH), BF16),
        compiler_params=_params(56),
        name="na",
    )(q, k.reshape(batch, seq, NA_WIDTH), v.reshape(batch, seq, NA_WIDTH), bias, head_mask)


def _dft_tables():
    two_pi = 2.0 * math.pi
    n = DFT_ROWS * DFT_COLS
    k1 = jnp.arange(DFT_ROWS)
    ang = ((k1[:, None] * k1[None, :]) % DFT_ROWS).astype(F32) * (two_pi / DFT_ROWS)
    f1 = jnp.stack([jnp.cos(ang), -jnp.sin(ang)], axis=1).reshape(2 * DFT_ROWS, DFT_ROWS)
    k2 = jnp.arange(DFT_COLS)
    t2 = jnp.arange(DFT_COLS)
    a = ((k2[:, None] * t2[None, :]) % DFT_COLS).astype(F32) * (two_pi / DFT_COLS)
    b = ((k1[:, None] * t2[None, :]) % n).astype(F32) * (two_pi / n)
    ca, sa = jnp.cos(a)[None], jnp.sin(a)[None]
    cb, sb = jnp.cos(b)[:, None, :], jnp.sin(b)[:, None, :]
    gr = ca * cb - sa * sb
    gi = -(sa * cb + ca * sb)
    g = jnp.concatenate([jnp.concatenate([gr, -gi], axis=2),
                         jnp.concatenate([gi, gr], axis=2)], axis=1)
    c = jnp.arange(F_GROUP_DIM)
    ang = ((c[:, None] * c[None, :]) % F_GROUP_DIM).astype(F32) * (two_pi / F_GROUP_DIM)
    eye = jnp.eye(F_GROUPS, dtype=F32)
    cc = jnp.kron(eye, jnp.cos(ang))
    ss = jnp.kron(eye, jnp.sin(ang))
    return f1.astype(BF16), g.astype(BF16), cc.astype(BF16), ss.astype(BF16)


def _f1_kernel(f_ref, x_ref, o_ref):
    o_ref[0] = _dot(f_ref[...], x_ref[0]).astype(BF16)


def _f3_kernel(a_ref, g_ref, cc_ref, ss_ref, o_ref, zr_ref, zi_ref):
    for j in range(F3_K1_BLOCK):
        z = _dot(g_ref[j], a_ref[0, j])
        zr_ref[j * DFT_COLS:(j + 1) * DFT_COLS] = z[:DFT_COLS].astype(BF16)
        zi_ref[j * DFT_COLS:(j + 1) * DFT_COLS] = z[DFT_COLS:].astype(BF16)
    y = _dot(zr_ref[...], cc_ref[...]) + _dot(zi_ref[...], ss_ref[...])
    scale = 1.0 / math.sqrt(DFT_ROWS * DFT_COLS * F_GROUP_DIM)
    y = (y * scale).astype(BF16)
    for j in range(F3_K1_BLOCK):
        o_ref[0, j] = y[j * DFT_COLS:(j + 1) * DFT_COLS]


def _fourier(u, tables, batch, seq):
    f1, g, cc, ss = tables
    lanes = DFT_COLS * F_WIDTH
    a = pl.pallas_call(
        _f1_kernel,
        grid=(batch, lanes // F1_LANE_BLOCK),
        in_specs=[_const_spec((2 * DFT_ROWS, DFT_ROWS)),
                  pl.BlockSpec((1, DFT_ROWS, F1_LANE_BLOCK), lambda b, i: (b, 0, i))],
        out_specs=pl.BlockSpec((1, 2 * DFT_ROWS, F1_LANE_BLOCK), lambda b, i: (b, 0, i)),
        out_shape=jax.ShapeDtypeStruct((batch, 2 * DFT_ROWS, lanes), BF16),
        compiler_params=_params(32),
        name="dft_rows",
    )(f1, u.reshape(batch, DFT_ROWS, lanes))
    a = a.reshape(batch, DFT_ROWS, 2 * DFT_COLS, F_WIDTH)
    y = pl.pallas_call(
        _f3_kernel,
        grid=(batch, DFT_ROWS // F3_K1_BLOCK),
        in_specs=[pl.BlockSpec((1, F3_K1_BLOCK, 2 * DFT_COLS, F_WIDTH), lambda b, i: (b, i, 0, 0)),
                  pl.BlockSpec((F3_K1_BLOCK, 2 * DFT_COLS, 2 * DFT_COLS), lambda b, i: (i, 0, 0)),
                  _const_spec((F_WIDTH, F_WIDTH)), _const_spec((F_WIDTH, F_WIDTH))],
        out_specs=pl.BlockSpec((1, F3_K1_BLOCK, DFT_COLS, F_WIDTH), lambda b, i: (b, i, 0, 0)),
        out_shape=jax.ShapeDtypeStruct((batch, DFT_ROWS, DFT_COLS, F_WIDTH), BF16),
        scratch_shapes=[pltpu.VMEM((F3_K1_BLOCK * DFT_COLS, F_WIDTH), BF16),
                        pltpu.VMEM((F3_K1_BLOCK * DFT_COLS, F_WIDTH), BF16)],
        compiler_params=_params(32),
        name="dft_cols_channels",
    )(a, g, cc, ss)
    return jnp.transpose(y, (0, 2, 1, 3)).reshape(batch * seq, F_WIDTH)


def _mixout_kernel(x_ref, g_ref, att_ref, fou_ref, wg_ref, gb_ref, wna_ref, wf_ref, wo_ref, o_ref):
    x = x_ref[...]
    h = _rms(x, g_ref[...]).astype(BF16)
    gates = jax.nn.sigmoid(_dot(h, wg_ref[...]) + gb_ref[...])
    y_na = _dot(att_ref[...], wna_ref[...])
    y_f = _dot(fou_ref[...], wf_ref[...])
    m = gates[:, :D_MODEL] * y_na + gates[:, D_MODEL:] * y_f
    o_ref[...] = x + _dot(m.astype(BF16), wo_ref[...])


def _mixout(x, layer, g, att, fou, w_in, gate_bias, w_na, w_f, w_o):
    n = x.shape[0]
    tile = pl.BlockSpec((TOKEN_TILE, D_MODEL), lambda i: (i, 0))
    half = pl.BlockSpec((TOKEN_TILE, NA_WIDTH), lambda i: (i, 0))
    assert w_in.shape[2] == 4 * D_MODEL
    return pl.pallas_call(
        _mixout_kernel,
        grid=(n // TOKEN_TILE,),
        in_specs=[tile, _layer_spec((1, D_MODEL), layer), half, half,
                  _layer_spec((D_MODEL, 2 * D_MODEL), layer, 1),
                  _layer_spec((1, 2 * D_MODEL), layer),
                  _layer_spec((NA_WIDTH, D_MODEL), layer), _layer_spec((F_WIDTH, D_MODEL), layer),
                  _layer_spec((D_MODEL, D_MODEL), layer)],
        out_specs=tile,
        out_shape=jax.ShapeDtypeStruct((n, D_MODEL), F32),
        compiler_params=_params(48),
        name="mix_out",
    )(x, g, att, fou, w_in, gate_bias, w_na, w_f, w_o)


def kernel(x, ffn1_norm, ffn1_w_in, ffn1_w_out, mix_norm, mix_w_in, mix_gate_bias, na_rpb,
           na_w_out, f_w_out, mix_w_o, ffn2_norm, ffn2_w_in, ffn2_w_out, final_norm):
    batch, seq, d = x.shape
    depth = ffn1_norm.shape[0]
    assert d == D_MODEL and seq == DFT_ROWS * DFT_COLS and seq % (GRID_W * NA_ROW_BLOCK) == 0
    assert (batch * seq) % TOKEN_TILE == 0
    rows = seq // GRID_W
    tables = _dft_tables()
    gain = lambda g: g.reshape(depth, 1, D_MODEL)
    ffn1 = (gain(ffn1_norm), ffn1_w_in.astype(BF16), ffn1_w_out.astype(BF16))
    ffn2 = (gain(ffn2_norm), ffn2_w_in.astype(BF16), ffn2_w_out.astype(BF16))
    mix_g = gain(mix_norm)
    w_in = mix_w_in.astype(BF16)
    gate_bias = mix_gate_bias.reshape(depth, 1, 2 * D_MODEL)
    w_na, w_f, w_o = na_w_out.astype(BF16), f_w_out.astype(BF16), mix_w_o.astype(BF16)
    xs = x.reshape(batch * seq, d)
    for l in range(depth):
        xs = _ffn(xs, l, *ffn1)
        q, k, v, u = _proj(xs, l, mix_g, w_in)
        att = _na(q, k, v, _na_bias_table(na_rpb[l], rows), batch, seq)
        fou = _fourier(u, tables, batch, seq)
        xs = _mixout(xs, l, mix_g, att, fou, w_in, gate_bias, w_na, w_f, w_o)
        xs = _ffn(xs, l, *ffn2, final_g=final_norm if l == depth - 1 else None)
    return xs.reshape(batch, seq, d)
```

```python
import functools
import math

import jax
import jax.numpy as jnp
from jax import lax
from jax.experimental import pallas as pl
from jax.experimental.pallas import tpu as pltpu

D_MODEL = 1024
GRID_W = 64
NA_HEADS = 8
NA_HEAD_DIM = 64
NA_WIDTH = NA_HEADS * NA_HEAD_DIM
NA_ROWS = 8
NA_COLS = 16
F_GROUPS = 4
F_GROUP_DIM = 128
F_WIDTH = F_GROUPS * F_GROUP_DIM
D_FF = 2816
RMS_EPS = 1e-6

BF16 = jnp.bfloat16
F32 = jnp.float32

MXU_COLS_V7X = 256
VMEM_BYTES_V7X = 64 * 1024 * 1024

TOKEN_TILE = 512
FF_CHUNK = MXU_COLS_V7X
NA_ROW_BLOCK = 8
DFT_ROWS = 128
DFT_COLS = 64
SUBLANES_F32 = 8
F1_T2_BLOCK = SUBLANES_F32
F3_K1_BLOCK = 8


def _params(vmem_mib):
    return pltpu.CompilerParams(
        dimension_semantics=None,
        vmem_limit_bytes=vmem_mib * 1024 * 1024,
    )


def _const_spec(shape):
    nd = len(shape)
    return pl.BlockSpec(shape, lambda *_: (0,) * nd, pipeline_mode=pl.Buffered(1))


def _layer_spec(shape, layer, col_block=0):
    nd = len(shape)
    index = (layer,) + (0,) * (nd - 1) + (col_block,)
    return pl.BlockSpec((None,) + tuple(shape), lambda *_: index, pipeline_mode=pl.Buffered(1))


def _rms(x, g):
    ms = jnp.mean(x * x, axis=-1, keepdims=True)
    return x * lax.rsqrt(ms + RMS_EPS) * g


def _dot(a, b):
    return jnp.dot(a, b, preferred_element_type=F32)


def _ffn_kernel(x_ref, g_ref, win_ref, wout_ref, *rest, final):
    if final:
        gf_ref, o_ref, act_ref = rest
    else:
        o_ref, act_ref = rest
    x = x_ref[...]
    h = _rms(x, g_ref[...]).astype(BF16)
    for c in range(D_FF // FF_CHUNK):
        lo = c * FF_CHUNK
        g = _dot(h, win_ref[:, lo:lo + FF_CHUNK])
        u = _dot(h, win_ref[:, D_FF + lo:D_FF + lo + FF_CHUNK])
        act_ref[:, lo:lo + FF_CHUNK] = (g * jax.nn.sigmoid(g) * u).astype(BF16)
    y = x + 0.5 * _dot(act_ref[...], wout_ref[...])
    if final:
        y = _rms(y, gf_ref[...])
    o_ref[...] = y


def _ffn(x, layer, g, w_in, w_out, final_g=None):
    n = x.shape[0]
    final = final_g is not None
    tile = pl.BlockSpec((TOKEN_TILE, D_MODEL), lambda i: (i, 0))
    in_specs = [tile, _layer_spec((1, D_MODEL), layer), _layer_spec((D_MODEL, 2 * D_FF), layer),
                _layer_spec((D_FF, D_MODEL), layer)]
    args = [x, g, w_in, w_out]
    if final:
        in_specs.append(_const_spec((1, D_MODEL)))
        args.append(final_g.reshape(1, D_MODEL))
    return pl.pallas_call(
        functools.partial(_ffn_kernel, final=final),
        grid=(n // TOKEN_TILE,),
        in_specs=in_specs,
        out_specs=tile,
        out_shape=jax.ShapeDtypeStruct((n, D_MODEL), F32),
        scratch_shapes=[pltpu.VMEM((TOKEN_TILE, D_FF), BF16)],
        compiler_params=_params(48),
        name="ffn",
    )(*args)


def _proj_kernel(x_ref, g_ref, w_ref, q_ref, k_ref, v_ref, u_ref):
    h = _rms(x_ref[...], g_ref[...]).astype(BF16)
    z = _dot(h, w_ref[...])
    s = NA_WIDTH
    q_ref[...] = (z[:, :s] * (NA_HEAD_DIM ** -0.5)).astype(BF16)
    k_ref[...] = z[:, s:2 * s].astype(BF16)
    v_ref[...] = z[:, 2 * s:3 * s].astype(BF16)
    u_ref[...] = z[:, 3 * s:3 * s + F_WIDTH]


def _proj(x, layer, g, w):
    n = x.shape[0]
    cols = 3 * NA_WIDTH + F_WIDTH
    assert w.shape[2] == 2 * cols
    out_tile = pl.BlockSpec((TOKEN_TILE, NA_WIDTH), lambda i: (i, 0))
    out_sds = jax.ShapeDtypeStruct((n, NA_WIDTH), BF16)
    return pl.pallas_call(
        _proj_kernel,
        grid=(n // TOKEN_TILE,),
        in_specs=[pl.BlockSpec((TOKEN_TILE, D_MODEL), lambda i: (i, 0)),
                  _layer_spec((1, D_MODEL), layer), _layer_spec((D_MODEL, cols), layer, 0)],
        out_specs=[out_tile] * 4,
        out_shape=[out_sds] * 3 + [jax.ShapeDtypeStruct((n, F_WIDTH), F32)],
        compiler_params=_params(40),
        name="mix_proj",
    )(x, g, w)


def _na_bias_table(rpb):
    qc = jnp.arange(GRID_W)
    kc = jnp.arange(GRID_W)
    ws = jnp.clip(qc - NA_COLS // 2, 0, GRID_W - NA_COLS)
    ok = (kc[None, :] >= ws[:, None]) & (kc[None, :] < ws[:, None] + NA_COLS)
    pad = GRID_W - NA_COLS
    padded = jnp.pad(rpb.astype(F32), ((0, 0), (0, 0), (pad, pad)))
    toep = jnp.stack([padded[:, :, GRID_W - 1 - q:2 * GRID_W - 1 - q] for q in range(GRID_W)],
                     axis=2)
    toep = jnp.where(ok[None, None], toep, -jnp.inf)
    return jnp.concatenate([toep[:, :-1], toep[:, 1:]], axis=-1)


def _na_kernel(q_ref, k_ref, v_ref, bias_ref, hm_ref, o_ref, *, rows):
    blk = pl.program_id(1)
    kh = NA_ROWS
    half = kh // 2
    lane_lo = lax.broadcasted_iota(jnp.int32, (GRID_W, 128), 1) < NA_HEAD_DIM

    def window(i):
        r = blk * NA_ROW_BLOCK + i
        rs = jnp.clip(r - half, 0, rows - kh)
        dr0 = rs - r + (NA_ROWS - 1)
        return pl.multiple_of(rs * GRID_W, GRID_W), dr0

    def bias(h, dr0):
        return jnp.concatenate([bias_ref[h, dr0 + 2 * j] for j in range(kh // 2)], axis=-1)

    def score_stage(i):
        koff, _ = window(i)
        ss = []
        for p in range(NA_HEADS // 2):
            lanes = slice(p * 128, (p + 1) * 128)
            qp = q_ref[i * GRID_W:(i + 1) * GRID_W, lanes]
            ql = jnp.concatenate([qp * hm_ref[0], qp * hm_ref[1]], axis=0)
            kw = k_ref[0, pl.ds(koff, kh * GRID_W), lanes]
            ss.append(lax.dot_general(ql, kw, (((1,), (1,)), ((), ())),
                                      preferred_element_type=F32))
        return ss

    def output_stage(i, ss):
        koff, dr0 = window(i)
        es, ls = [], []
        for p in range(NA_HEADS // 2):
            s = ss[p] + jnp.concatenate([bias(2 * p, dr0), bias(2 * p + 1, dr0)], axis=0)
            m = jnp.max(s, axis=-1, keepdims=True)
            e = jnp.exp(s - m)
            ls.append(jnp.sum(e, axis=-1, keepdims=True))
            es.append(e.astype(BF16))
        for p in range(NA_HEADS // 2):
            lanes = slice(p * 128, (p + 1) * 128)
            vw = v_ref[0, pl.ds(koff, kh * GRID_W), lanes]
            pv = _dot(es[p], vw) / ls[p]
            o = jnp.where(lane_lo, pv[:GRID_W], pv[GRID_W:])
            o_ref[i * GRID_W:(i + 1) * GRID_W, lanes] = o.astype(BF16)

    ss = score_stage(0)
    for i in range(NA_ROW_BLOCK):
        nxt = score_stage(i + 1) if i + 1 < NA_ROW_BLOCK else None
        output_stage(i, ss)
        ss = nxt


def _na(q, k, v, bias, batch, seq):
    rows = seq // GRID_W
    assert rows >= NA_ROWS and NA_ROWS % 2 == 0
    blk_tokens = NA_ROW_BLOCK * GRID_W
    nblk = rows // NA_ROW_BLOCK
    head_mask = (jnp.arange(128)[None, None, :] // NA_HEAD_DIM
                 == jnp.arange(2)[:, None, None]).astype(BF16)
    head_mask = jnp.broadcast_to(head_mask, (2, GRID_W, 128))
    q_tile = pl.BlockSpec((blk_tokens, NA_WIDTH), lambda b, i: (b * nblk + i, 0))
    kv_spec = pl.BlockSpec((1, seq, NA_WIDTH), lambda b, i: (b, 0, 0))
    return pl.pallas_call(
        functools.partial(_na_kernel, rows=rows),
        grid=(batch, nblk),
        in_specs=[q_tile, kv_spec, kv_spec,
                  _const_spec((NA_HEADS, 2 * NA_ROWS - 2, GRID_W, 2 * GRID_W)),
                  _const_spec((2, GRID_W, 128))],
        out_specs=q_tile,
        out_shape=jax.ShapeDtypeStruct((batch * seq, NA_WIDTH), BF16),
        compiler_params=_params(56),
        name="na",
    )(q, k.reshape(batch, seq, NA_WIDTH), v.reshape(batch, seq, NA_WIDTH), bias, head_mask)


def _dft_tables():
    two_pi = 2.0 * math.pi
    n = DFT_ROWS * DFT_COLS
    k1 = jnp.arange(DFT_ROWS)
    ang = ((k1[:, None] * k1[None, :]) % DFT_ROWS).astype(F32) * (two_pi / DFT_ROWS)
    eye_t2 = jnp.eye(F1_T2_BLOCK, dtype=F32)
    f1 = jnp.concatenate([jnp.kron(jnp.cos(ang), eye_t2), jnp.kron(-jnp.sin(ang), eye_t2)], axis=0)
    k2 = jnp.arange(DFT_COLS)
    t2 = jnp.arange(DFT_COLS)
    a = ((k2[:, None] * t2[None, :]) % DFT_COLS).astype(F32) * (two_pi / DFT_COLS)
    b = ((k1[:, None] * t2[None, :]) % n).astype(F32) * (two_pi / n)
    ca, sa = jnp.cos(a)[None], jnp.sin(a)[None]
    cb, sb = jnp.cos(b)[:, None, :], jnp.sin(b)[:, None, :]
    gr = ca * cb - sa * sb
    gi = -(sa * cb + ca * sb)
    g = jnp.concatenate([jnp.concatenate([gr, -gi], axis=2),
                         jnp.concatenate([gi, gr], axis=2)], axis=1)
    c = jnp.arange(F_GROUP_DIM)
    ang = ((c[:, None] * c[None, :]) % F_GROUP_DIM).astype(F32) * (two_pi / F_GROUP_DIM)
    eye = jnp.eye(F_GROUPS, dtype=F32)
    cc = jnp.kron(eye, jnp.cos(ang))
    ss = jnp.kron(eye, jnp.sin(ang))
    return f1.astype(BF16), g.astype(BF16), cc.astype(BF16), ss.astype(BF16)


def _f1_kernel(f_ref, x_ref, o_ref):
    x = x_ref[0].reshape(DFT_ROWS * F1_T2_BLOCK, F_WIDTH).astype(BF16)
    z = _dot(f_ref[...], x)
    o_ref[0] = z.reshape(2, DFT_ROWS, F1_T2_BLOCK, F_WIDTH)


def _f3_kernel(a_ref, g_ref, cc_ref, ss_ref, o_ref, zr_ref, zi_ref):
    for j in range(F3_K1_BLOCK):
        a = jnp.concatenate([a_ref[0, 0, j], a_ref[0, 1, j]], axis=0).astype(BF16)
        z = _dot(g_ref[j], a)
        zr_ref[j * DFT_COLS:(j + 1) * DFT_COLS] = z[:DFT_COLS].astype(BF16)
        zi_ref[j * DFT_COLS:(j + 1) * DFT_COLS] = z[DFT_COLS:].astype(BF16)
    y = _dot(zr_ref[...], cc_ref[...]) + _dot(zi_ref[...], ss_ref[...])
    scale = 1.0 / math.sqrt(DFT_ROWS * DFT_COLS * F_GROUP_DIM)
    y = (y * scale).astype(BF16)
    for j in range(F3_K1_BLOCK):
        o_ref[0, j] = y[j * DFT_COLS:(j + 1) * DFT_COLS]


def _fourier(u, tables, batch, seq):
    f1, g, cc, ss = tables
    f1_rows = 2 * DFT_ROWS * F1_T2_BLOCK
    a = pl.pallas_call(
        _f1_kernel,
        grid=(batch, DFT_COLS // F1_T2_BLOCK),
        in_specs=[_const_spec((f1_rows, f1_rows // 2)),
                  pl.BlockSpec((1, DFT_ROWS, F1_T2_BLOCK, F_WIDTH), lambda b, i: (b, 0, i, 0))],
        out_specs=pl.BlockSpec((1, 2, DFT_ROWS, F1_T2_BLOCK, F_WIDTH), lambda b, i: (b, 0, 0, i, 0)),
        out_shape=jax.ShapeDtypeStruct((batch, 2, DFT_ROWS, DFT_COLS, F_WIDTH), F32),
        compiler_params=_params(40),
        name="dft_rows",
    )(f1, u.reshape(batch, DFT_ROWS, DFT_COLS, F_WIDTH))
    y = pl.pallas_call(
        _f3_kernel,
        grid=(batch, DFT_ROWS // F3_K1_BLOCK),
        in_specs=[pl.BlockSpec((1, 2, F3_K1_BLOCK, DFT_COLS, F_WIDTH), lambda b, i: (b, 0, i, 0, 0)),
                  pl.BlockSpec((F3_K1_BLOCK, 2 * DFT_COLS, 2 * DFT_COLS), lambda b, i: (i, 0, 0)),
                  _const_spec((F_WIDTH, F_WIDTH)), _const_spec((F_WIDTH, F_WIDTH))],
        out_specs=pl.BlockSpec((1, F3_K1_BLOCK, DFT_COLS, F_WIDTH), lambda b, i: (b, i, 0, 0)),
        out_shape=jax.ShapeDtypeStruct((batch, DFT_ROWS, DFT_COLS, F_WIDTH), BF16),
        scratch_shapes=[pltpu.VMEM((F3_K1_BLOCK * DFT_COLS, F_WIDTH), BF16),
                        pltpu.VMEM((F3_K1_BLOCK * DFT_COLS, F_WIDTH), BF16)],
        compiler_params=_params(32),
        name="dft_cols_channels",
    )(a, g, cc, ss)
    return jnp.transpose(y, (0, 2, 1, 3)).reshape(batch * seq, F_WIDTH)


def _mixout_kernel(x_ref, g_ref, att_ref, fou_ref, wg_ref, gb_ref, wna_ref, wf_ref, wo_ref, o_ref):
    x = x_ref[...]
    h = _rms(x, g_ref[...]).astype(BF16)
    gates = jax.nn.sigmoid(_dot(h, wg_ref[...]) + gb_ref[...])
    y_na = _dot(att_ref[...], wna_ref[...])
    y_f = _dot(fou_ref[...], wf_ref[...])
    m = gates[:, :D_MODEL] * y_na + gates[:, D_MODEL:] * y_f
    o_ref[...] = x + _dot(m.astype(BF16), wo_ref[...])


def _mixout(x, layer, g, att, fou, w_in, gate_bias, w_na, w_f, w_o):
    n = x.shape[0]
    tile = pl.BlockSpec((TOKEN_TILE, D_MODEL), lambda i: (i, 0))
    half = pl.BlockSpec((TOKEN_TILE, NA_WIDTH), lambda i: (i, 0))
    assert w_in.shape[2] == 4 * D_MODEL
    return pl.pallas_call(
        _mixout_kernel,
        grid=(n // TOKEN_TILE,),
        in_specs=[tile, _layer_spec((1, D_MODEL), layer), half, half,
                  _layer_spec((D_MODEL, 2 * D_MODEL), layer, 1),
                  _layer_spec((1, 2 * D_MODEL), layer),
                  _layer_spec((NA_WIDTH, D_MODEL), layer), _layer_spec((F_WIDTH, D_MODEL), layer),
                  _layer_spec((D_MODEL, D_MODEL), layer)],
        out_specs=tile,
        out_shape=jax.ShapeDtypeStruct((n, D_MODEL), F32),
        compiler_params=_params(48),
        name="mix_out",
    )(x, g, att, fou, w_in, gate_bias, w_na, w_f, w_o)


def kernel(x, ffn1_norm, ffn1_w_in, ffn1_w_out, mix_norm, mix_w_in, mix_gate_bias, na_rpb,
           na_w_out, f_w_out, mix_w_o, ffn2_norm, ffn2_w_in, ffn2_w_out, final_norm):
    batch, seq, d = x.shape
    depth = ffn1_norm.shape[0]
    assert d == D_MODEL and seq == DFT_ROWS * DFT_COLS and seq % (GRID_W * NA_ROW_BLOCK) == 0
    assert (batch * seq) % TOKEN_TILE == 0
    tables = _dft_tables()
    gain = lambda g: g.reshape(depth, 1, D_MODEL)
    ffn1 = (gain(ffn1_norm), ffn1_w_in.astype(BF16), ffn1_w_out.astype(BF16))
    ffn2 = (gain(ffn2_norm), ffn2_w_in.astype(BF16), ffn2_w_out.astype(BF16))
    mix_g = gain(mix_norm)
    w_in = mix_w_in.astype(BF16)
    gate_bias = mix_gate_bias.reshape(depth, 1, 2 * D_MODEL)
    w_na, w_f, w_o = na_w_out.astype(BF16), f_w_out.astype(BF16), mix_w_o.astype(BF16)
    xs = x.reshape(batch * seq, d)
    for l in range(depth):
        xs = _ffn(xs, l, *ffn1)
        q, k, v, u = _proj(xs, l, mix_g, w_in)
        att = _na(q, k, v, _na_bias_table(na_rpb[l]), batch, seq)
        fou = _fourier(u, tables, batch, seq)
        xs = _mixout(xs, l, mix_g, att, fou, w_in, gate_bias, w_na, w_f, w_o)
        xs = _ffn(xs, l, *ffn2, final_g=final_norm if l == depth - 1 else None)
    return xs.reshape(batch, seq, d)
```

```python
import functools
import math

import jax
import jax.numpy as jnp
from jax import lax
from jax.experimental import pallas as pl
from jax.experimental.pallas import tpu as pltpu

D_MODEL = 1024
GRID_W = 64
NA_HEADS = 8
NA_HEAD_DIM = 64
NA_WIDTH = NA_HEADS * NA_HEAD_DIM
NA_ROWS = 8
NA_COLS = 16
F_GROUPS = 4
F_GROUP_DIM = 128
F_WIDTH = F_GROUPS * F_GROUP_DIM
D_FF = 2816
RMS_EPS = 1e-6

BF16 = jnp.bfloat16
F32 = jnp.float32

MXU_COLS_V7X = 256
VMEM_BYTES_V7X = 64 * 1024 * 1024

TOKEN_TILE = 512
FF_CHUNK = MXU_COLS_V7X
NA_ROW_BLOCK = 8
DFT_ROWS = 128
DFT_COLS = 64
SUBLANES_F32 = 8
F1_T2_BLOCK = SUBLANES_F32
F3_K1_BLOCK = 8


def _params(vmem_mib):
    return pltpu.CompilerParams(
        dimension_semantics=None,
        vmem_limit_bytes=vmem_mib * 1024 * 1024,
    )


def _const_spec(shape):
    nd = len(shape)
    return pl.BlockSpec(shape, lambda *_: (0,) * nd, pipeline_mode=pl.Buffered(1))


def _layer_spec(shape, layer, col_block=0):
    nd = len(shape)
    index = (layer,) + (0,) * (nd - 1) + (col_block,)
    return pl.BlockSpec((None,) + tuple(shape), lambda *_: index, pipeline_mode=pl.Buffered(1))


def _rms(x, g):
    ms = jnp.mean(x * x, axis=-1, keepdims=True)
    return x * lax.rsqrt(ms + RMS_EPS) * g


def _dot(a, b):
    return jnp.dot(a, b, preferred_element_type=F32)


def _ffn_kernel(x_ref, g_ref, win_ref, wout_ref, *rest, final):
    if final:
        gf_ref, o_ref, act_ref = rest
    else:
        o_ref, act_ref = rest
    x = x_ref[...]
    h = _rms(x, g_ref[...]).astype(BF16)
    for c in range(D_FF // FF_CHUNK):
        lo = c * FF_CHUNK
        g = _dot(h, win_ref[:, lo:lo + FF_CHUNK])
        u = _dot(h, win_ref[:, D_FF + lo:D_FF + lo + FF_CHUNK])
        act_ref[:, lo:lo + FF_CHUNK] = (g * jax.nn.sigmoid(g) * u).astype(BF16)
    y = x + 0.5 * _dot(act_ref[...], wout_ref[...])
    if final:
        y = _rms(y, gf_ref[...])
    o_ref[...] = y


def _ffn(x, layer, g, w_in, w_out, final_g=None):
    n = x.shape[0]
    final = final_g is not None
    tile = pl.BlockSpec((TOKEN_TILE, D_MODEL), lambda i: (i, 0))
    in_specs = [tile, _layer_spec((1, D_MODEL), layer), _layer_spec((D_MODEL, 2 * D_FF), layer),
                _layer_spec((D_FF, D_MODEL), layer)]
    args = [x, g, w_in, w_out]
    if final:
        in_specs.append(_const_spec((1, D_MODEL)))
        args.append(final_g.reshape(1, D_MODEL))
    return pl.pallas_call(
        functools.partial(_ffn_kernel, final=final),
        grid=(n // TOKEN_TILE,),
        in_specs=in_specs,
        out_specs=tile,
        out_shape=jax.ShapeDtypeStruct((n, D_MODEL), F32),
        scratch_shapes=[pltpu.VMEM((TOKEN_TILE, D_FF), BF16)],
        compiler_params=_params(48),
        name="ffn",
    )(*args)


def _proj_kernel(x_ref, g_ref, w_ref, q_ref, k_ref, v_ref, u_ref):
    h = _rms(x_ref[...], g_ref[...]).astype(BF16)
    z = _dot(h, w_ref[...])
    s = NA_WIDTH
    q_ref[...] = (z[:, :s] * (NA_HEAD_DIM ** -0.5)).astype(BF16)
    k_ref[...] = z[:, s:2 * s].astype(BF16)
    v_ref[...] = z[:, 2 * s:3 * s].astype(BF16)
    u_ref[...] = z[:, 3 * s:3 * s + F_WIDTH]


def _proj(x, layer, g, w):
    n = x.shape[0]
    cols = 3 * NA_WIDTH + F_WIDTH
    assert w.shape[2] == 2 * cols
    out_tile = pl.BlockSpec((TOKEN_TILE, NA_WIDTH), lambda i: (i, 0))
    out_sds = jax.ShapeDtypeStruct((n, NA_WIDTH), BF16)
    return pl.pallas_call(
        _proj_kernel,
        grid=(n // TOKEN_TILE,),
        in_specs=[pl.BlockSpec((TOKEN_TILE, D_MODEL), lambda i: (i, 0)),
                  _layer_spec((1, D_MODEL), layer), _layer_spec((D_MODEL, cols), layer, 0)],
        out_specs=[out_tile] * 4,
        out_shape=[out_sds] * 3 + [jax.ShapeDtypeStruct((n, F_WIDTH), F32)],
        compiler_params=_params(40),
        name="mix_proj",
    )(x, g, w)


def _na_bias_table(rpb):
    qc = jnp.arange(GRID_W)
    kc = jnp.arange(GRID_W)
    ws = jnp.clip(qc - NA_COLS // 2, 0, GRID_W - NA_COLS)
    ok = (kc[None, :] >= ws[:, None]) & (kc[None, :] < ws[:, None] + NA_COLS)
    dc = kc[None, :] - qc[:, None] + (NA_COLS - 1)
    pick = (dc[:, :, None] == jnp.arange(2 * NA_COLS - 1)[None, None, :]).astype(F32)
    toep = jnp.einsum('hdj,qkj->hdqk', rpb.astype(F32), pick,
                      precision=lax.Precision.HIGHEST)
    toep = jnp.where(ok[None, None], toep, -jnp.inf)
    return jnp.concatenate([toep[:, :-1], toep[:, 1:]], axis=-1)


def _na_kernel(q_ref, k_ref, v_ref, bias_ref, hm_ref, o_ref, *, rows):
    blk = pl.program_id(1)
    kh = NA_ROWS
    half = kh // 2
    lane_lo = lax.broadcasted_iota(jnp.int32, (GRID_W, 128), 1) < NA_HEAD_DIM

    def window(i):
        r = blk * NA_ROW_BLOCK + i
        rs = jnp.clip(r - half, 0, rows - kh)
        dr0 = rs - r + (NA_ROWS - 1)
        return pl.multiple_of(rs * GRID_W, GRID_W), dr0

    def bias(h, dr0):
        return jnp.concatenate([bias_ref[h, dr0 + 2 * j] for j in range(kh // 2)], axis=-1)

    def score_stage(i):
        koff, _ = window(i)
        ss = []
        for p in range(NA_HEADS // 2):
            lanes = slice(p * 128, (p + 1) * 128)
            qp = q_ref[i * GRID_W:(i + 1) * GRID_W, lanes]
            ql = jnp.concatenate([qp * hm_ref[0], qp * hm_ref[1]], axis=0)
            kw = k_ref[0, pl.ds(koff, kh * GRID_W), lanes]
            ss.append(lax.dot_general(ql, kw, (((1,), (1,)), ((), ())),
                                      preferred_element_type=F32))
        return ss

    def output_stage(i, ss):
        koff, dr0 = window(i)
        es, ls = [], []
        for p in range(NA_HEADS // 2):
            s = ss[p] + jnp.concatenate([bias(2 * p, dr0), bias(2 * p + 1, dr0)], axis=0)
            m = jnp.max(s, axis=-1, keepdims=True)
            e = jnp.exp(s - m)
            ls.append(jnp.sum(e, axis=-1, keepdims=True))
            es.append(e.astype(BF16))
        for p in range(NA_HEADS // 2):
            lanes = slice(p * 128, (p + 1) * 128)
            vw = v_ref[0, pl.ds(koff, kh * GRID_W), lanes]
            pv = _dot(es[p], vw) / ls[p]
            o = jnp.where(lane_lo, pv[:GRID_W], pv[GRID_W:])
            o_ref[i * GRID_W:(i + 1) * GRID_W, lanes] = o.astype(BF16)

    ss = score_stage(0)
    for i in range(NA_ROW_BLOCK):
        nxt = score_stage(i + 1) if i + 1 < NA_ROW_BLOCK else None
        output_stage(i, ss)
        ss = nxt


def _na(q, k, v, bias, batch, seq):
    rows = seq // GRID_W
    assert rows >= NA_ROWS and NA_ROWS % 2 == 0
    blk_tokens = NA_ROW_BLOCK * GRID_W
    nblk = rows // NA_ROW_BLOCK
    head_mask = (jnp.arange(128)[None, None, :] // NA_HEAD_DIM
                 == jnp.arange(2)[:, None, None]).astype(BF16)
    head_mask = jnp.broadcast_to(head_mask, (2, GRID_W, 128))
    q_tile = pl.BlockSpec((blk_tokens, NA_WIDTH), lambda b, i: (b * nblk + i, 0))
    kv_spec = pl.BlockSpec((1, seq, NA_WIDTH), lambda b, i: (b, 0, 0))
    return pl.pallas_call(
        functools.partial(_na_kernel, rows=rows),
        grid=(batch, nblk),
        in_specs=[q_tile, kv_spec, kv_spec,
                  _const_spec((NA_HEADS, 2 * NA_ROWS - 2, GRID_W, 2 * GRID_W)),
                  _const_spec((2, GRID_W, 128))],
        out_specs=q_tile,
        out_shape=jax.ShapeDtypeStruct((batch * seq, NA_WIDTH), BF16),
        compiler_params=_params(56),
        name="na",
    )(q, k.reshape(batch, seq, NA_WIDTH), v.reshape(batch, seq, NA_WIDTH), bias, head_mask)


def _dft_tables():
    two_pi = 2.0 * math.pi
    n = DFT_ROWS * DFT_COLS
    k1 = jnp.arange(DFT_ROWS)
    ang = ((k1[:, None] * k1[None, :]) % DFT_ROWS).astype(F32) * (two_pi / DFT_ROWS)
    f = jnp.concatenate([jnp.cos(ang), -jnp.sin(ang)], axis=0)
    f_wide = jnp.repeat(f, F1_T2_BLOCK, axis=1)
    same_t2 = (jnp.arange(DFT_ROWS * F1_T2_BLOCK)[None, None, :] % F1_T2_BLOCK
               == jnp.arange(F1_T2_BLOCK)[None, :, None])
    f1 = jnp.where(same_t2, f_wide[:, None, :], 0.0).reshape(
        2 * DFT_ROWS * F1_T2_BLOCK, DFT_ROWS * F1_T2_BLOCK)
    k2 = jnp.arange(DFT_COLS)
    t2 = jnp.arange(DFT_COLS)
    a = ((k2[:, None] * t2[None, :]) % DFT_COLS).astype(F32) * (two_pi / DFT_COLS)
    b = ((k1[:, None] * t2[None, :]) % n).astype(F32) * (two_pi / n)
    ca, sa = jnp.cos(a)[None], jnp.sin(a)[None]
    cb, sb = jnp.cos(b)[:, None, :], jnp.sin(b)[:, None, :]
    gr = ca * cb - sa * sb
    gi = -(sa * cb + ca * sb)
    g = jnp.concatenate([jnp.concatenate([gr, -gi], axis=2),
                         jnp.concatenate([gi, gr], axis=2)], axis=1)
    c = jnp.arange(F_GROUP_DIM)
    ang = ((c[:, None] * c[None, :]) % F_GROUP_DIM).astype(F32) * (two_pi / F_GROUP_DIM)
    eye = jnp.eye(F_GROUPS, dtype=F32)
    cc = jnp.kron(eye, jnp.cos(ang))
    ss = jnp.kron(eye, jnp.sin(ang))
    return f1.astype(BF16), g.astype(BF16), cc.astype(BF16), ss.astype(BF16)


def _f1_kernel(f_ref, x_ref, o_ref):
    x = x_ref[0].reshape(DFT_ROWS * F1_T2_BLOCK, F_WIDTH).astype(BF16)
    z = _dot(f_ref[...], x)
    o_ref[0] = z.reshape(2, DFT_ROWS, F1_T2_BLOCK, F_WIDTH)


def _f3_kernel(a_ref, g_ref, cc_ref, ss_ref, o_ref, zr_ref, zi_ref):
    for j in range(F3_K1_BLOCK):
        a = jnp.concatenate([a_ref[0, 0, j], a_ref[0, 1, j]], axis=0).astype(BF16)
        z = _dot(g_ref[j], a)
        zr_ref[j * DFT_COLS:(j + 1) * DFT_COLS] = z[:DFT_COLS].astype(BF16)
        zi_ref[j * DFT_COLS:(j + 1) * DFT_COLS] = z[DFT_COLS:].astype(BF16)
    y = _dot(zr_ref[...], cc_ref[...]) + _dot(zi_ref[...], ss_ref[...])
    scale = 1.0 / math.sqrt(DFT_ROWS * DFT_COLS * F_GROUP_DIM)
    y = (y * scale).astype(BF16)
    for j in range(F3_K1_BLOCK):
        o_ref[0, j] = y[j * DFT_COLS:(j + 1) * DFT_COLS]


def _fourier(u, tables, batch, seq):
    f1, g, cc, ss = tables
    f1_rows = 2 * DFT_ROWS * F1_T2_BLOCK
    a = pl.pallas_call(
        _f1_kernel,
        grid=(batch, DFT_COLS // F1_T2_BLOCK),
        in_specs=[_const_spec((f1_rows, f1_rows // 2)),
                  pl.BlockSpec((1, DFT_ROWS, F1_T2_BLOCK, F_WIDTH), lambda b, i: (b, 0, i, 0))],
        out_specs=pl.BlockSpec((1, 2, DFT_ROWS, F1_T2_BLOCK, F_WIDTH), lambda b, i: (b, 0, 0, i, 0)),
        out_shape=jax.ShapeDtypeStruct((batch, 2, DFT_ROWS, DFT_COLS, F_WIDTH), F32),
        compiler_params=_params(40),
        name="dft_rows",
    )(f1, u.reshape(batch, DFT_ROWS, DFT_COLS, F_WIDTH))
    y = pl.pallas_call(
        _f3_kernel,
        grid=(batch, DFT_ROWS // F3_K1_BLOCK),
        in_specs=[pl.BlockSpec((1, 2, F3_K1_BLOCK, DFT_COLS, F_WIDTH), lambda b, i: (b, 0, i, 0, 0)),
                  pl.BlockSpec((F3_K1_BLOCK, 2 * DFT_COLS, 2 * DFT_COLS), lambda b, i: (i, 0, 0)),
                  _const_spec((F_WIDTH, F_WIDTH)), _const_spec((F_WIDTH, F_WIDTH))],
        out_specs=pl.BlockSpec((1, F3_K1_BLOCK, DFT_COLS, F_WIDTH), lambda b, i: (b, i, 0, 0)),
        out_shape=jax.ShapeDtypeStruct((batch, DFT_ROWS, DFT_COLS, F_WIDTH), BF16),
        scratch_shapes=[pltpu.VMEM((F3_K1_BLOCK * DFT_COLS, F_WIDTH), BF16),
                        pltpu.VMEM((F3_K1_BLOCK * DFT_COLS, F_WIDTH), BF16)],
        compiler_params=_params(32),
        name="dft_cols_channels",
    )(a, g, cc, ss)
    return jnp.transpose(y, (0, 2, 1, 3)).reshape(batch * seq, F_WIDTH)


def _mixout_kernel(x_ref, g_ref, att_ref, fou_ref, wg_ref, gb_ref, wna_ref, wf_ref, wo_ref, o_ref):
    x = x_ref[...]
    h = _rms(x, g_ref[...]).astype(BF16)
    gates = jax.nn.sigmoid(_dot(h, wg_ref[...]) + gb_ref[...])
    y_na = _dot(att_ref[...], wna_ref[...])
    y_f = _dot(fou_ref[...], wf_ref[...])
    m = gates[:, :D_MODEL] * y_na + gates[:, D_MODEL:] * y_f
    o_ref[...] = x + _dot(m.astype(BF16), wo_ref[...])


def _mixout(x, layer, g, att, fou, w_in, gate_bias, w_na, w_f, w_o):
    n = x.shape[0]
    tile = pl.BlockSpec((TOKEN_TILE, D_MODEL), lambda i: (i, 0))
    half = pl.BlockSpec((TOKEN_TILE, NA_WIDTH), lambda i: (i, 0))
    assert w_in.shape[2] == 4 * D_MODEL
    return pl.pallas_call(
        _mixout_kernel,
        grid=(n // TOKEN_TILE,),
        in_specs=[tile, _layer_spec((1, D_MODEL), layer), half, half,
                  _layer_spec((D_MODEL, 2 * D_MODEL), layer, 1),
                  _layer_spec((1, 2 * D_MODEL), layer),
                  _layer_spec((NA_WIDTH, D_MODEL), layer), _layer_spec((F_WIDTH, D_MODEL), layer),
                  _layer_spec((D_MODEL, D_MODEL), layer)],
        out_specs=tile,
        out_shape=jax.ShapeDtypeStruct((n, D_MODEL), F32),
        compiler_params=_params(48),
        name="mix_out",
    )(x, g, att, fou, w_in, gate_bias, w_na, w_f, w_o)


def kernel(x, ffn1_norm, ffn1_w_in, ffn1_w_out, mix_norm, mix_w_in, mix_gate_bias, na_rpb,
           na_w_out, f_w_out, mix_w_o, ffn2_norm, ffn2_w_in, ffn2_w_out, final_norm):
    batch, seq, d = x.shape
    depth = ffn1_norm.shape[0]
    assert d == D_MODEL and seq == DFT_ROWS * DFT_COLS and seq % (GRID_W * NA_ROW_BLOCK) == 0
    assert (batch * seq) % TOKEN_TILE == 0
    tables = _dft_tables()
    gain = lambda g: g.reshape(depth, 1, D_MODEL)
    ffn1 = (gain(ffn1_norm), ffn1_w_in.astype(BF16), ffn1_w_out.astype(BF16))
    ffn2 = (gain(ffn2_norm), ffn2_w_in.astype(BF16), ffn2_w_out.astype(BF16))
    mix_g = gain(mix_norm)
    w_in = mix_w_in.astype(BF16)
    gate_bias = mix_gate_bias.reshape(depth, 1, 2 * D_MODEL)
    w_na, w_f, w_o = na_w_out.astype(BF16), f_w_out.astype(BF16), mix_w_o.astype(BF16)
    xs = x.reshape(batch * seq, d)
    for l in range(depth):
        xs = _ffn(xs, l, *ffn1)
        q, k, v, u = _proj(xs, l, mix_g, w_in)
        att = _na(q, k, v, _na_bias_table(na_rpb[l]), batch, seq)
        fou = _fourier(u, tables, batch, seq)
        xs = _mixout(xs, l, mix_g, att, fou, w_in, gate_bias, w_na, w_f, w_o)
        xs = _ffn(xs, l, *ffn2, final_g=final_norm if l == depth - 1 else None)
    return xs.reshape(batch, seq, d)
```

```python
import functools
import math

import jax
import jax.numpy as jnp
from jax import lax
from jax.experimental import pallas as pl
from jax.experimental.pallas import tpu as pltpu

D_MODEL = 1024
GRID_W = 64
NA_HEADS = 8
NA_HEAD_DIM = 64
NA_WIDTH = NA_HEADS * NA_HEAD_DIM
NA_ROWS = 8
NA_COLS = 16
F_GROUPS = 4
F_GROUP_DIM = 128
F_WIDTH = F_GROUPS * F_GROUP_DIM
D_FF = 2816
RMS_EPS = 1e-6

BF16 = jnp.bfloat16
F32 = jnp.float32

MXU_COLS_V7X = 256
VMEM_BYTES_V7X = 64 * 1024 * 1024

TOKEN_TILE = 512
FF_CHUNK = MXU_COLS_V7X
NA_ROW_BLOCK = 8
DFT_ROWS = 128
DFT_COLS = 64
SUBLANES_F32 = 8
BF16_SUBLANES = 16
F1_T2_BLOCK = SUBLANES_F32
F3_K1_BLOCK = 8


def _params(vmem_mib):
    return pltpu.CompilerParams(
        dimension_semantics=None,
        vmem_limit_bytes=vmem_mib * 1024 * 1024,
    )


def _const_spec(shape, col_block=0):
    index = (0,) * (len(shape) - 1) + (col_block,)
    return pl.BlockSpec(shape, lambda *_: index, pipeline_mode=pl.Buffered(1))


def _cast_specs(params, steps):
    in_specs, out_specs, out_shapes = [], [], []
    for arr, layer in params:
        _, rows, cols = arr.shape
        nblk = steps
        while rows % nblk or (rows // nblk) % BF16_SUBLANES:
            nblk //= 2
        blk = rows // nblk
        in_specs.append(pl.BlockSpec(
            (None, blk, cols), lambda i, layer=layer, nblk=nblk: (layer, jnp.minimum(i, nblk - 1), 0)))
        out_specs.append(pl.BlockSpec(
            (blk, cols), lambda i, nblk=nblk: (jnp.minimum(i, nblk - 1), 0)))
        out_shapes.append(jax.ShapeDtypeStruct((rows, cols), BF16))
    return in_specs, out_specs, out_shapes


def _cast_blocks(src_refs, dst_refs):
    for src, dst in zip(src_refs, dst_refs):
        dst[...] = src[...].astype(BF16)


def _layer_spec(shape, layer, col_block=0):
    nd = len(shape)
    index = (layer,) + (0,) * (nd - 1) + (col_block,)
    return pl.BlockSpec((None,) + tuple(shape), lambda *_: index, pipeline_mode=pl.Buffered(1))


def _rms(x, g):
    ms = jnp.mean(x * x, axis=-1, keepdims=True)
    return x * lax.rsqrt(ms + RMS_EPS) * g


def _dot(a, b):
    return jnp.dot(a, b, preferred_element_type=F32)


def _ffn_kernel(x_ref, g_ref, win_ref, wout_ref, *rest, final, n_cast):
    rest = list(rest)
    gf_ref = rest.pop(0) if final else None
    cast_src, o_ref, cast_dst, act_ref = (rest[:n_cast], rest[n_cast],
                                          rest[n_cast + 1:2 * n_cast + 1], rest[-1])
    _cast_blocks(cast_src, cast_dst)
    x = x_ref[...]
    h = _rms(x, g_ref[...]).astype(BF16)
    for c in range(D_FF // FF_CHUNK):
        lo = c * FF_CHUNK
        g = _dot(h, win_ref[:, lo:lo + FF_CHUNK])
        u = _dot(h, win_ref[:, D_FF + lo:D_FF + lo + FF_CHUNK])
        act_ref[:, lo:lo + FF_CHUNK] = (g * jax.nn.sigmoid(g) * u).astype(BF16)
    y = x + 0.5 * _dot(act_ref[...], wout_ref[...])
    if final:
        y = _rms(y, gf_ref[...])
    o_ref[...] = y


def _ffn(x, layer, g, w_in, w_out, final_g=None, cast=()):
    n = x.shape[0]
    steps = n // TOKEN_TILE
    final = final_g is not None
    tile = pl.BlockSpec((TOKEN_TILE, D_MODEL), lambda i: (i, 0))
    in_specs = [tile, _layer_spec((1, D_MODEL), layer), _const_spec((D_MODEL, 2 * D_FF)),
                _const_spec((D_FF, D_MODEL))]
    args = [x, g, w_in, w_out]
    if final:
        in_specs.append(_const_spec((1, D_MODEL)))
        args.append(final_g.reshape(1, D_MODEL))
    cast_in, cast_out, cast_shapes = _cast_specs(cast, steps)
    y, *copies = pl.pallas_call(
        functools.partial(_ffn_kernel, final=final, n_cast=len(cast)),
        grid=(steps,),
        in_specs=in_specs + cast_in,
        out_specs=[tile] + cast_out,
        out_shape=[jax.ShapeDtypeStruct((n, D_MODEL), F32)] + cast_shapes,
        scratch_shapes=[pltpu.VMEM((TOKEN_TILE, D_FF), BF16)],
        compiler_params=_params(52),
        name="ffn",
    )(*args, *[arr for arr, _ in cast])
    return y, copies


def _proj_kernel(x_ref, g_ref, w_ref, q_ref, k_ref, v_ref, u_ref):
    h = _rms(x_ref[...], g_ref[...]).astype(BF16)
    z = _dot(h, w_ref[...])
    s = NA_WIDTH
    q_ref[...] = (z[:, :s] * (NA_HEAD_DIM ** -0.5)).astype(BF16)
    k_ref[...] = z[:, s:2 * s].astype(BF16)
    v_ref[...] = z[:, 2 * s:3 * s].astype(BF16)
    u_ref[...] = z[:, 3 * s:3 * s + F_WIDTH]


def _proj(x, layer, g, w):
    n = x.shape[0]
    cols = 3 * NA_WIDTH + F_WIDTH
    assert w.shape[1] == 2 * cols
    out_tile = pl.BlockSpec((TOKEN_TILE, NA_WIDTH), lambda i: (i, 0))
    out_sds = jax.ShapeDtypeStruct((n, NA_WIDTH), BF16)
    return pl.pallas_call(
        _proj_kernel,
        grid=(n // TOKEN_TILE,),
        in_specs=[pl.BlockSpec((TOKEN_TILE, D_MODEL), lambda i: (i, 0)),
                  _layer_spec((1, D_MODEL), layer), _const_spec((D_MODEL, cols), 0)],
        out_specs=[out_tile] * 4,
        out_shape=[out_sds] * 3 + [jax.ShapeDtypeStruct((n, F_WIDTH), F32)],
        compiler_params=_params(40),
        name="mix_proj",
    )(x, g, w)


def _na_bias_table(rpb):
    qc = jnp.arange(GRID_W)
    kc = jnp.arange(GRID_W)
    ws = jnp.clip(qc - NA_COLS // 2, 0, GRID_W - NA_COLS)
    ok = (kc[None, :] >= ws[:, None]) & (kc[None, :] < ws[:, None] + NA_COLS)
    dc = kc[None, :] - qc[:, None] + (NA_COLS - 1)
    pick = (dc[:, :, None] == jnp.arange(2 * NA_COLS - 1)[None, None, :]).astype(F32)
    toep = jnp.einsum('hdj,qkj->hdqk', rpb.astype(F32), pick,
                      precision=lax.Precision.HIGHEST)
    toep = jnp.where(ok[None, None], toep, -jnp.inf)
    return jnp.concatenate([toep[:, :-1], toep[:, 1:]], axis=-1)


def _na_kernel(q_ref, k_ref, v_ref, bias_ref, hm_ref, o_ref, *, rows):
    blk = pl.program_id(1)
    kh = NA_ROWS
    half = kh // 2
    lane_lo = lax.broadcasted_iota(jnp.int32, (GRID_W, 128), 1) < NA_HEAD_DIM

    def window(i):
        r = blk * NA_ROW_BLOCK + i
        rs = jnp.clip(r - half, 0, rows - kh)
        dr0 = rs - r + (NA_ROWS - 1)
        return pl.multiple_of(rs * GRID_W, GRID_W), dr0

    def bias(h, dr0):
        return jnp.concatenate([bias_ref[h, dr0 + 2 * j] for j in range(kh // 2)], axis=-1)

    def score_stage(i):
        koff, _ = window(i)
        ss = []
        for p in range(NA_HEADS // 2):
            lanes = slice(p * 128, (p + 1) * 128)
            qp = q_ref[i * GRID_W:(i + 1) * GRID_W, lanes]
            ql = jnp.concatenate([qp * hm_ref[0], qp * hm_ref[1]], axis=0)
            kw = k_ref[0, pl.ds(koff, kh * GRID_W), lanes]
            ss.append(lax.dot_general(ql, kw, (((1,), (1,)), ((), ())),
                                      preferred_element_type=F32))
        return ss

    def output_stage(i, ss):
        koff, dr0 = window(i)
        es, ls = [], []
        for p in range(NA_HEADS // 2):
            s = ss[p] + jnp.concatenate([bias(2 * p, dr0), bias(2 * p + 1, dr0)], axis=0)
            m = jnp.max(s, axis=-1, keepdims=True)
            e = jnp.exp(s - m)
            ls.append(jnp.sum(e, axis=-1, keepdims=True))
            es.append(e.astype(BF16))
        for p in range(NA_HEADS // 2):
            lanes = slice(p * 128, (p + 1) * 128)
            vw = v_ref[0, pl.ds(koff, kh * GRID_W), lanes]
            pv = _dot(es[p], vw) / ls[p]
            o = jnp.where(lane_lo, pv[:GRID_W], pv[GRID_W:])
            o_ref[i * GRID_W:(i + 1) * GRID_W, lanes] = o.astype(BF16)

    ss = score_stage(0)
    for i in range(NA_ROW_BLOCK):
        nxt = score_stage(i + 1) if i + 1 < NA_ROW_BLOCK else None
        output_stage(i, ss)
        ss = nxt


def _na(q, k, v, bias, batch, seq):
    rows = seq // GRID_W
    assert rows >= NA_ROWS and NA_ROWS % 2 == 0
    blk_tokens = NA_ROW_BLOCK * GRID_W
    nblk = rows // NA_ROW_BLOCK
    head_mask = (jnp.arange(128)[None, None, :] // NA_HEAD_DIM
                 == jnp.arange(2)[:, None, None]).astype(BF16)
    head_mask = jnp.broadcast_to(head_mask, (2, GRID_W, 128))
    q_tile = pl.BlockSpec((blk_tokens, NA_WIDTH), lambda b, i: (b * nblk + i, 0))
    kv_spec = pl.BlockSpec((1, seq, NA_WIDTH), lambda b, i: (b, 0, 0))
    return pl.pallas_call(
        functools.partial(_na_kernel, rows=rows),
        grid=(batch, nblk),
        in_specs=[q_tile, kv_spec, kv_spec,
                  _const_spec((NA_HEADS, 2 * NA_ROWS - 2, GRID_W, 2 * GRID_W)),
                  _const_spec((2, GRID_W, 128))],
        out_specs=q_tile,
        out_shape=jax.ShapeDtypeStruct((batch * seq, NA_WIDTH), BF16),
        compiler_params=_params(56),
        name="na",
    )(q, k.reshape(batch, seq, NA_WIDTH), v.reshape(batch, seq, NA_WIDTH), bias, head_mask)


def _dft_tables():
    two_pi = 2.0 * math.pi
    n = DFT_ROWS * DFT_COLS
    k1 = jnp.arange(DFT_ROWS)
    ang = ((k1[:, None] * k1[None, :]) % DFT_ROWS).astype(F32) * (two_pi / DFT_ROWS)
    f = jnp.concatenate([jnp.cos(ang), -jnp.sin(ang)], axis=0)
    f_wide = jnp.repeat(f, F1_T2_BLOCK, axis=1)
    same_t2 = (jnp.arange(DFT_ROWS * F1_T2_BLOCK)[None, None, :] % F1_T2_BLOCK
               == jnp.arange(F1_T2_BLOCK)[None, :, None])
    f1 = jnp.where(same_t2, f_wide[:, None, :], 0.0).reshape(
        2 * DFT_ROWS * F1_T2_BLOCK, DFT_ROWS * F1_T2_BLOCK)
    k2 = jnp.arange(DFT_COLS)
    t2 = jnp.arange(DFT_COLS)
    a = ((k2[:, None] * t2[None, :]) % DFT_COLS).astype(F32) * (two_pi / DFT_COLS)
    b = ((k1[:, None] * t2[None, :]) % n).astype(F32) * (two_pi / n)
    ca, sa = jnp.cos(a)[None], jnp.sin(a)[None]
    cb, sb = jnp.cos(b)[:, None, :], jnp.sin(b)[:, None, :]
    gr = ca * cb - sa * sb
    gi = -(sa * cb + ca * sb)
    g = jnp.concatenate([jnp.concatenate([gr, -gi], axis=2),
                         jnp.concatenate([gi, gr], axis=2)], axis=1)
    c = jnp.arange(F_GROUP_DIM)
    ang = ((c[:, None] * c[None, :]) % F_GROUP_DIM).astype(F32) * (two_pi / F_GROUP_DIM)
    eye = jnp.eye(F_GROUPS, dtype=F32)
    cc = jnp.kron(eye, jnp.cos(ang))
    ss = jnp.kron(eye, jnp.sin(ang))
    return f1.astype(BF16), g.astype(BF16), cc.astype(BF16), ss.astype(BF16)


def _f1_kernel(f_ref, x_ref, o_ref):
    x = x_ref[0].reshape(DFT_ROWS * F1_T2_BLOCK, F_WIDTH).astype(BF16)
    z = _dot(f_ref[...], x)
    o_ref[0] = z.reshape(2, DFT_ROWS, F1_T2_BLOCK, F_WIDTH)


def _f3_kernel(a_ref, g_ref, cc_ref, ss_ref, o_ref, zr_ref, zi_ref):
    for j in range(F3_K1_BLOCK):
        a = jnp.concatenate([a_ref[0, 0, j], a_ref[0, 1, j]], axis=0).astype(BF16)
        z = _dot(g_ref[j], a)
        zr_ref[j * DFT_COLS:(j + 1) * DFT_COLS] = z[:DFT_COLS].astype(BF16)
        zi_ref[j * DFT_COLS:(j + 1) * DFT_COLS] = z[DFT_COLS:].astype(BF16)
    y = _dot(zr_ref[...], cc_ref[...]) + _dot(zi_ref[...], ss_ref[...])
    scale = 1.0 / math.sqrt(DFT_ROWS * DFT_COLS * F_GROUP_DIM)
    y = (y * scale).astype(BF16)
    for j in range(F3_K1_BLOCK):
        o_ref[0, j] = y[j * DFT_COLS:(j + 1) * DFT_COLS]


def _fourier(u, tables, batch, seq):
    f1, g, cc, ss = tables
    f1_rows = 2 * DFT_ROWS * F1_T2_BLOCK
    a = pl.pallas_call(
        _f1_kernel,
        grid=(batch, DFT_COLS // F1_T2_BLOCK),
        in_specs=[_const_spec((f1_rows, f1_rows // 2)),
                  pl.BlockSpec((1, DFT_ROWS, F1_T2_BLOCK, F_WIDTH), lambda b, i: (b, 0, i, 0))],
        out_specs=pl.BlockSpec((1, 2, DFT_ROWS, F1_T2_BLOCK, F_WIDTH), lambda b, i: (b, 0, 0, i, 0)),
        out_shape=jax.ShapeDtypeStruct((batch, 2, DFT_ROWS, DFT_COLS, F_WIDTH), F32),
        compiler_params=_params(40),
        name="dft_rows",
    )(f1, u.reshape(batch, DFT_ROWS, DFT_COLS, F_WIDTH))
    y = pl.pallas_call(
        _f3_kernel,
        grid=(batch, DFT_ROWS // F3_K1_BLOCK),
        in_specs=[pl.BlockSpec((1, 2, F3_K1_BLOCK, DFT_COLS, F_WIDTH), lambda b, i: (b, 0, i, 0, 0)),
                  pl.BlockSpec((F3_K1_BLOCK, 2 * DFT_COLS, 2 * DFT_COLS), lambda b, i: (i, 0, 0)),
                  _const_spec((F_WIDTH, F_WIDTH)), _const_spec((F_WIDTH, F_WIDTH))],
        out_specs=pl.BlockSpec((1, F3_K1_BLOCK, DFT_COLS, F_WIDTH), lambda b, i: (b, i, 0, 0)),
        out_shape=jax.ShapeDtypeStruct((batch, DFT_ROWS, DFT_COLS, F_WIDTH), BF16),
        scratch_shapes=[pltpu.VMEM((F3_K1_BLOCK * DFT_COLS, F_WIDTH), BF16),
                        pltpu.VMEM((F3_K1_BLOCK * DFT_COLS, F_WIDTH), BF16)],
        compiler_params=_params(32),
        name="dft_cols_channels",
    )(a, g, cc, ss)
    return jnp.transpose(y, (0, 2, 1, 3)).reshape(batch * seq, F_WIDTH)


def _mixout_kernel(x_ref, g_ref, att_ref, fou_ref, wg_ref, gb_ref, wna_ref, wf_ref, wo_ref,
                   *rest, n_cast):
    cast_src, o_ref, cast_dst = rest[:n_cast], rest[n_cast], rest[n_cast + 1:]
    _cast_blocks(cast_src, cast_dst)
    x = x_ref[...]
    h = _rms(x, g_ref[...]).astype(BF16)
    gates = jax.nn.sigmoid(_dot(h, wg_ref[...]) + gb_ref[...])
    y_na = _dot(att_ref[...], wna_ref[...])
    y_f = _dot(fou_ref[...], wf_ref[...])
    m = gates[:, :D_MODEL] * y_na + gates[:, D_MODEL:] * y_f
    o_ref[...] = x + _dot(m.astype(BF16), wo_ref[...])


def _mixout(x, layer, g, att, fou, w_in, gate_bias, w_na, w_f, w_o, cast=()):
    n = x.shape[0]
    steps = n // TOKEN_TILE
    tile = pl.BlockSpec((TOKEN_TILE, D_MODEL), lambda i: (i, 0))
    half = pl.BlockSpec((TOKEN_TILE, NA_WIDTH), lambda i: (i, 0))
    assert w_in.shape[1] == 4 * D_MODEL
    cast_in, cast_out, cast_shapes = _cast_specs(cast, steps)
    y, *copies = pl.pallas_call(
        functools.partial(_mixout_kernel, n_cast=len(cast)),
        grid=(steps,),
        in_specs=[tile, _layer_spec((1, D_MODEL), layer), half, half,
                  _const_spec((D_MODEL, 2 * D_MODEL), 1),
                  _layer_spec((1, 2 * D_MODEL), layer),
                  _const_spec((NA_WIDTH, D_MODEL)), _const_spec((F_WIDTH, D_MODEL)),
                  _const_spec((D_MODEL, D_MODEL))] + cast_in,
        out_specs=[tile] + cast_out,
        out_shape=[jax.ShapeDtypeStruct((n, D_MODEL), F32)] + cast_shapes,
        compiler_params=_params(48),
        name="mix_out",
    )(x, g, att, fou, w_in, gate_bias, w_na, w_f, w_o, *[arr for arr, _ in cast])
    return y, copies


def kernel(x, ffn1_norm, ffn1_w_in, ffn1_w_out, mix_norm, mix_w_in, mix_gate_bias, na_rpb,
           na_w_out, f_w_out, mix_w_o, ffn2_norm, ffn2_w_in, ffn2_w_out, final_norm):
    batch, seq, d = x.shape
    depth = ffn1_norm.shape[0]
    assert d == D_MODEL and seq == DFT_ROWS * DFT_COLS and seq % (GRID_W * NA_ROW_BLOCK) == 0
    assert (batch * seq) % TOKEN_TILE == 0
    tables = _dft_tables()
    gain = lambda g: g.reshape(depth, 1, D_MODEL)
    g1, g2, mix_g = gain(ffn1_norm), gain(ffn2_norm), gain(mix_norm)
    gate_bias = mix_gate_bias.reshape(depth, 1, 2 * D_MODEL)
    xs = x.reshape(batch * seq, d)
    ffn1_w = [ffn1_w_in[0].astype(BF16), ffn1_w_out[0].astype(BF16)]
    for l in range(depth):
        xs, (w_in, w_na, w_f, w_o) = _ffn(
            xs, l, g1, *ffn1_w,
            cast=[(mix_w_in, l), (na_w_out, l), (f_w_out, l), (mix_w_o, l)])
        q, k, v, u = _proj(xs, l, mix_g, w_in)
        att = _na(q, k, v, _na_bias_table(na_rpb[l]), batch, seq)
        fou = _fourier(u, tables, batch, seq)
        xs, ffn2_w = _mixout(xs, l, mix_g, att, fou, w_in, gate_bias, w_na, w_f, w_o,
                             cast=[(ffn2_w_in, l), (ffn2_w_out, l)])
        last = l == depth - 1
        xs, ffn1_w = _ffn(xs, l, g2, *ffn2_w, final_g=final_norm if last else None,
                          cast=[] if last else [(ffn1_w_in, l + 1), (ffn1_w_out, l + 1)])
    return xs.reshape(batch, seq, d)
```

```python
import functools
import math

import jax
import jax.numpy as jnp
from jax import lax
from jax.experimental import pallas as pl
from jax.experimental.pallas import tpu as pltpu

D_MODEL = 1024
GRID_W = 64
NA_HEADS = 8
NA_HEAD_DIM = 64
NA_WIDTH = NA_HEADS * NA_HEAD_DIM
NA_ROWS = 8
NA_COLS = 16
F_GROUPS = 4
F_GROUP_DIM = 128
F_WIDTH = F_GROUPS * F_GROUP_DIM
D_FF = 2816
RMS_EPS = 1e-6

BF16 = jnp.bfloat16
F32 = jnp.float32

MXU_COLS_V7X = 256
VMEM_BYTES_V7X = 64 * 1024 * 1024

TOKEN_TILE = 512
FFN_TOKEN_TILE = 1024
FF_CHUNK = MXU_COLS_V7X
NA_ROW_BLOCK = 8
DFT_ROWS = 128
DFT_COLS = 64
SUBLANES_F32 = 8
BF16_SUBLANES = 16
F1_T2_BLOCK = SUBLANES_F32
F3_K1_BLOCK = 8


def _params(vmem_mib):
    return pltpu.CompilerParams(
        dimension_semantics=None,
        vmem_limit_bytes=vmem_mib * 1024 * 1024,
    )


def _const_spec(shape, col_block=0):
    index = (0,) * (len(shape) - 1) + (col_block,)
    return pl.BlockSpec(shape, lambda *_: index, pipeline_mode=pl.Buffered(1))


def _cast_specs(params, steps):
    in_specs, out_specs, out_shapes = [], [], []
    for arr, layer in params:
        _, rows, cols = arr.shape
        nblk = steps
        while rows % nblk or (rows // nblk) % BF16_SUBLANES:
            nblk //= 2
        blk = rows // nblk
        in_specs.append(pl.BlockSpec(
            (None, blk, cols), lambda i, layer=layer, nblk=nblk: (layer, jnp.minimum(i, nblk - 1), 0)))
        out_specs.append(pl.BlockSpec(
            (blk, cols), lambda i, nblk=nblk: (jnp.minimum(i, nblk - 1), 0)))
        out_shapes.append(jax.ShapeDtypeStruct((rows, cols), BF16))
    return in_specs, out_specs, out_shapes


def _cast_blocks(src_refs, dst_refs):
    for src, dst in zip(src_refs, dst_refs):
        dst[...] = src[...].astype(BF16)


def _layer_spec(shape, layer):
    index = (layer,) + (0,) * len(shape)
    return pl.BlockSpec((None,) + tuple(shape), lambda *_: index, pipeline_mode=pl.Buffered(1))


def _rms(x, g):
    ms = jnp.mean(x * x, axis=-1, keepdims=True)
    return x * lax.rsqrt(ms + RMS_EPS) * g


def _dot(a, b):
    return jnp.dot(a, b, preferred_element_type=F32)


def _ffn_kernel(x_ref, g_ref, win_ref, wout_ref, *rest, final, n_cast):
    rest = list(rest)
    gf_ref = rest.pop(0) if final else None
    cast_src, o_ref, cast_dst, act_ref = (rest[:n_cast], rest[n_cast],
                                          rest[n_cast + 1:2 * n_cast + 1], rest[-1])
    _cast_blocks(cast_src, cast_dst)
    x = x_ref[...]
    h = _rms(x, g_ref[...]).astype(BF16)
    for c in range(D_FF // FF_CHUNK):
        lo = c * FF_CHUNK
        g = _dot(h, win_ref[:, lo:lo + FF_CHUNK])
        u = _dot(h, win_ref[:, D_FF + lo:D_FF + lo + FF_CHUNK])
        act_ref[:, lo:lo + FF_CHUNK] = (g * jax.nn.sigmoid(g) * u).astype(BF16)
    y = x + 0.5 * _dot(act_ref[...], wout_ref[...])
    if final:
        y = _rms(y, gf_ref[...])
    o_ref[...] = y


def _ffn(x, layer, g, w_in, w_out, final_g=None, cast=()):
    n = x.shape[0]
    steps = n // FFN_TOKEN_TILE
    final = final_g is not None
    tile = pl.BlockSpec((FFN_TOKEN_TILE, D_MODEL), lambda i: (i, 0))
    in_specs = [tile, _layer_spec((1, D_MODEL), layer), _const_spec((D_MODEL, 2 * D_FF)),
                _const_spec((D_FF, D_MODEL))]
    args = [x, g, w_in, w_out]
    if final:
        in_specs.append(_const_spec((1, D_MODEL)))
        args.append(final_g.reshape(1, D_MODEL))
    cast_in, cast_out, cast_shapes = _cast_specs(cast, steps)
    y, *copies = pl.pallas_call(
        functools.partial(_ffn_kernel, final=final, n_cast=len(cast)),
        grid=(steps,),
        in_specs=in_specs + cast_in,
        out_specs=[tile] + cast_out,
        out_shape=[jax.ShapeDtypeStruct((n, D_MODEL), F32)] + cast_shapes,
        scratch_shapes=[pltpu.VMEM((FFN_TOKEN_TILE, D_FF), BF16)],
        compiler_params=_params(58),
        name="ffn",
    )(*args, *[arr for arr, _ in cast])
    return y, copies


def _proj_kernel(x_ref, g_ref, w_ref, q_ref, k_ref, v_ref, u_ref):
    h = _rms(x_ref[...], g_ref[...]).astype(BF16)
    z = _dot(h, w_ref[...])
    s = NA_WIDTH
    q_ref[...] = (z[:, :s] * (NA_HEAD_DIM ** -0.5)).astype(BF16)
    k_ref[...] = z[:, s:2 * s].astype(BF16)
    v_ref[...] = z[:, 2 * s:3 * s].astype(BF16)
    u_ref[...] = z[:, 3 * s:3 * s + F_WIDTH]


def _proj(x, layer, g, w):
    n = x.shape[0]
    cols = 3 * NA_WIDTH + F_WIDTH
    assert w.shape[1] == 2 * cols
    out_tile = pl.BlockSpec((TOKEN_TILE, NA_WIDTH), lambda i: (i, 0))
    out_sds = jax.ShapeDtypeStruct((n, NA_WIDTH), BF16)
    return pl.pallas_call(
        _proj_kernel,
        grid=(n // TOKEN_TILE,),
        in_specs=[pl.BlockSpec((TOKEN_TILE, D_MODEL), lambda i: (i, 0)),
                  _layer_spec((1, D_MODEL), layer), _const_spec((D_MODEL, cols), 0)],
        out_specs=[out_tile] * 4,
        out_shape=[out_sds] * 3 + [jax.ShapeDtypeStruct((n, F_WIDTH), F32)],
        compiler_params=_params(40),
        name="mix_proj",
    )(x, g, w)


def _na_bias_table(rpb):
    qc = jnp.arange(GRID_W)
    kc = jnp.arange(GRID_W)
    ws = jnp.clip(qc - NA_COLS // 2, 0, GRID_W - NA_COLS)
    ok = (kc[None, :] >= ws[:, None]) & (kc[None, :] < ws[:, None] + NA_COLS)
    dc = kc[None, :] - qc[:, None] + (NA_COLS - 1)
    pick = (dc[:, :, None] == jnp.arange(2 * NA_COLS - 1)[None, None, :]).astype(F32)
    toep = jnp.einsum('hdj,qkj->hdqk', rpb.astype(F32), pick,
                      precision=lax.Precision.HIGHEST)
    toep = jnp.where(ok[None, None], toep, -jnp.inf)
    return jnp.concatenate([toep[:, :-1], toep[:, 1:]], axis=-1)


def _na_kernel(q_ref, k_ref, v_ref, bias_ref, hm_ref, o_ref, *, rows):
    blk = pl.program_id(1)
    kh = NA_ROWS
    half = kh // 2
    lane_lo = lax.broadcasted_iota(jnp.int32, (GRID_W, 128), 1) < NA_HEAD_DIM

    def window(i):
        r = blk * NA_ROW_BLOCK + i
        rs = jnp.clip(r - half, 0, rows - kh)
        dr0 = rs - r + (NA_ROWS - 1)
        return pl.multiple_of(rs * GRID_W, GRID_W), dr0

    def bias(h, dr0):
        return jnp.concatenate([bias_ref[h, dr0 + 2 * j] for j in range(kh // 2)], axis=-1)

    def score_stage(i):
        koff, _ = window(i)
        ss = []
        for p in range(NA_HEADS // 2):
            lanes = slice(p * 128, (p + 1) * 128)
            qp = q_ref[i * GRID_W:(i + 1) * GRID_W, lanes]
            ql = jnp.concatenate([qp * hm_ref[0], qp * hm_ref[1]], axis=0)
            kw = k_ref[0, pl.ds(koff, kh * GRID_W), lanes]
            ss.append(lax.dot_general(ql, kw, (((1,), (1,)), ((), ())),
                                      preferred_element_type=F32))
        return ss

    def output_stage(i, ss):
        koff, dr0 = window(i)
        es, ls = [], []
        for p in range(NA_HEADS // 2):
            s = ss[p] + jnp.concatenate([bias(2 * p, dr0), bias(2 * p + 1, dr0)], axis=0)
            m = jnp.max(s, axis=-1, keepdims=True)
            e = jnp.exp(s - m)
            ls.append(jnp.sum(e, axis=-1, keepdims=True))
            es.append(e.astype(BF16))
        for p in range(NA_HEADS // 2):
            lanes = slice(p * 128, (p + 1) * 128)
            vw = v_ref[0, pl.ds(koff, kh * GRID_W), lanes]
            pv = _dot(es[p], vw) / ls[p]
            o = jnp.where(lane_lo, pv[:GRID_W], pv[GRID_W:])
            o_ref[i * GRID_W:(i + 1) * GRID_W, lanes] = o.astype(BF16)

    ss = score_stage(0)
    for i in range(NA_ROW_BLOCK):
        nxt = score_stage(i + 1) if i + 1 < NA_ROW_BLOCK else None
        output_stage(i, ss)
        ss = nxt


def _na(q, k, v, bias, batch, seq):
    rows = seq // GRID_W
    assert rows >= NA_ROWS and NA_ROWS % 2 == 0
    blk_tokens = NA_ROW_BLOCK * GRID_W
    nblk = rows // NA_ROW_BLOCK
    head_mask = (jnp.arange(128)[None, None, :] // NA_HEAD_DIM
                 == jnp.arange(2)[:, None, None]).astype(BF16)
    head_mask = jnp.broadcast_to(head_mask, (2, GRID_W, 128))
    q_tile = pl.BlockSpec((blk_tokens, NA_WIDTH), lambda b, i: (b * nblk + i, 0))
    kv_spec = pl.BlockSpec((1, seq, NA_WIDTH), lambda b, i: (b, 0, 0))
    return pl.pallas_call(
        functools.partial(_na_kernel, rows=rows),
        grid=(batch, nblk),
        in_specs=[q_tile, kv_spec, kv_spec,
                  _const_spec((NA_HEADS, 2 * NA_ROWS - 2, GRID_W, 2 * GRID_W)),
                  _const_spec((2, GRID_W, 128))],
        out_specs=q_tile,
        out_shape=jax.ShapeDtypeStruct((batch * seq, NA_WIDTH), BF16),
        compiler_params=_params(56),
        name="na",
    )(q, k.reshape(batch, seq, NA_WIDTH), v.reshape(batch, seq, NA_WIDTH), bias, head_mask)


def _dft_tables():
    two_pi = 2.0 * math.pi
    n = DFT_ROWS * DFT_COLS
    k1 = jnp.arange(DFT_ROWS)
    ang = ((k1[:, None] * k1[None, :]) % DFT_ROWS).astype(F32) * (two_pi / DFT_ROWS)
    f = jnp.concatenate([jnp.cos(ang), -jnp.sin(ang)], axis=0)
    f_wide = jnp.repeat(f, F1_T2_BLOCK, axis=1)
    same_t2 = (jnp.arange(DFT_ROWS * F1_T2_BLOCK)[None, None, :] % F1_T2_BLOCK
               == jnp.arange(F1_T2_BLOCK)[None, :, None])
    f1 = jnp.where(same_t2, f_wide[:, None, :], 0.0).reshape(
        2 * DFT_ROWS * F1_T2_BLOCK, DFT_ROWS * F1_T2_BLOCK)
    k2 = jnp.arange(DFT_COLS)
    t2 = jnp.arange(DFT_COLS)
    a = ((k2[:, None] * t2[None, :]) % DFT_COLS).astype(F32) * (two_pi / DFT_COLS)
    b = ((k1[:, None] * t2[None, :]) % n).astype(F32) * (two_pi / n)
    ca, sa = jnp.cos(a)[None], jnp.sin(a)[None]
    cb, sb = jnp.cos(b)[:, None, :], jnp.sin(b)[:, None, :]
    gr = ca * cb - sa * sb
    gi = -(sa * cb + ca * sb)
    g = jnp.concatenate([jnp.concatenate([gr, -gi], axis=2),
                         jnp.concatenate([gi, gr], axis=2)], axis=1)
    c = jnp.arange(F_GROUP_DIM)
    ang = ((c[:, None] * c[None, :]) % F_GROUP_DIM).astype(F32) * (two_pi / F_GROUP_DIM)
    eye = jnp.eye(MXU_COLS_V7X // F_GROUP_DIM, dtype=F32)
    cc = jnp.kron(eye, jnp.cos(ang))
    ss = jnp.kron(eye, jnp.sin(ang))
    return f1.astype(BF16), g.astype(BF16), cc.astype(BF16), ss.astype(BF16)


def _f1_kernel(f_ref, x_ref, o_ref):
    x = x_ref[0].reshape(DFT_ROWS * F1_T2_BLOCK, F_WIDTH).astype(BF16)
    z = _dot(f_ref[...], x)
    o_ref[0] = z.reshape(2, DFT_ROWS, F1_T2_BLOCK, F_WIDTH).astype(BF16)


def _f3_kernel(a_ref, g_ref, cc_ref, ss_ref, o_ref, zr_ref, zi_ref):
    for j in range(F3_K1_BLOCK):
        a = jnp.concatenate([a_ref[0, 0, j], a_ref[0, 1, j]], axis=0)
        z = _dot(g_ref[j], a)
        zr_ref[j * DFT_COLS:(j + 1) * DFT_COLS] = z[:DFT_COLS].astype(BF16)
        zi_ref[j * DFT_COLS:(j + 1) * DFT_COLS] = z[DFT_COLS:].astype(BF16)
    scale = 1.0 / math.sqrt(DFT_ROWS * DFT_COLS * F_GROUP_DIM)
    for p in range(F_WIDTH // MXU_COLS_V7X):
        cols = slice(p * MXU_COLS_V7X, (p + 1) * MXU_COLS_V7X)
        y = _dot(zr_ref[:, cols], cc_ref[...]) + _dot(zi_ref[:, cols], ss_ref[...])
        y = (y * scale).astype(BF16)
        for j in range(F3_K1_BLOCK):
            o_ref[0, j, :, cols] = y[j * DFT_COLS:(j + 1) * DFT_COLS]


def _fourier(u, tables, batch, seq):
    f1, g, cc, ss = tables
    f1_rows = 2 * DFT_ROWS * F1_T2_BLOCK
    a = pl.pallas_call(
        _f1_kernel,
        grid=(batch, DFT_COLS // F1_T2_BLOCK),
        in_specs=[_const_spec((f1_rows, f1_rows // 2)),
                  pl.BlockSpec((1, DFT_ROWS, F1_T2_BLOCK, F_WIDTH), lambda b, i: (b, 0, i, 0))],
        out_specs=pl.BlockSpec((1, 2, DFT_ROWS, F1_T2_BLOCK, F_WIDTH), lambda b, i: (b, 0, 0, i, 0)),
        out_shape=jax.ShapeDtypeStruct((batch, 2, DFT_ROWS, DFT_COLS, F_WIDTH), BF16),
        compiler_params=_params(40),
        name="dft_rows",
    )(f1, u.reshape(batch, DFT_ROWS, DFT_COLS, F_WIDTH))
    y = pl.pallas_call(
        _f3_kernel,
        grid=(batch, DFT_ROWS // F3_K1_BLOCK),
        in_specs=[pl.BlockSpec((1, 2, F3_K1_BLOCK, DFT_COLS, F_WIDTH), lambda b, i: (b, 0, i, 0, 0)),
                  pl.BlockSpec((F3_K1_BLOCK, 2 * DFT_COLS, 2 * DFT_COLS), lambda b, i: (i, 0, 0)),
                  _const_spec((MXU_COLS_V7X, MXU_COLS_V7X)), _const_spec((MXU_COLS_V7X, MXU_COLS_V7X))],
        out_specs=pl.BlockSpec((1, F3_K1_BLOCK, DFT_COLS, F_WIDTH), lambda b, i: (b, i, 0, 0)),
        out_shape=jax.ShapeDtypeStruct((batch, DFT_ROWS, DFT_COLS, F_WIDTH), BF16),
        scratch_shapes=[pltpu.VMEM((F3_K1_BLOCK * DFT_COLS, F_WIDTH), BF16),
                        pltpu.VMEM((F3_K1_BLOCK * DFT_COLS, F_WIDTH), BF16)],
        compiler_params=_params(32),
        name="dft_cols_channels",
    )(a, g, cc, ss)
    return jnp.transpose(y, (0, 2, 1, 3)).reshape(batch * seq, F_WIDTH)


def _mixout_kernel(x_ref, g_ref, att_ref, fou_ref, wg_ref, gb_ref, wna_ref, wf_ref, wo_ref,
                   *rest, n_cast):
    cast_src, o_ref, cast_dst = rest[:n_cast], rest[n_cast], rest[n_cast + 1:]
    _cast_blocks(cast_src, cast_dst)
    x = x_ref[...]
    h = _rms(x, g_ref[...]).astype(BF16)
    gates = jax.nn.sigmoid(_dot(h, wg_ref[...]) + gb_ref[...])
    y_na = _dot(att_ref[...], wna_ref[...])
    y_f = _dot(fou_ref[...], wf_ref[...])
    m = gates[:, :D_MODEL] * y_na + gates[:, D_MODEL:] * y_f
    o_ref[...] = x + _dot(m.astype(BF16), wo_ref[...])


def _mixout(x, layer, g, att, fou, w_in, gate_bias, w_na, w_f, w_o, cast=()):
    n = x.shape[0]
    steps = n // TOKEN_TILE
    tile = pl.BlockSpec((TOKEN_TILE, D_MODEL), lambda i: (i, 0))
    half = pl.BlockSpec((TOKEN_TILE, NA_WIDTH), lambda i: (i, 0))
    assert w_in.shape[1] == 4 * D_MODEL
    cast_in, cast_out, cast_shapes = _cast_specs(cast, steps)
    y, *copies = pl.pallas_call(
        functools.partial(_mixout_kernel, n_cast=len(cast)),
        grid=(steps,),
        in_specs=[tile, _layer_spec((1, D_MODEL), layer), half, half,
                  _const_spec((D_MODEL, 2 * D_MODEL), 1),
                  _layer_spec((1, 2 * D_MODEL), layer),
                  _const_spec((NA_WIDTH, D_MODEL)), _const_spec((F_WIDTH, D_MODEL)),
                  _const_spec((D_MODEL, D_MODEL))] + cast_in,
        out_specs=[tile] + cast_out,
        out_shape=[jax.ShapeDtypeStruct((n, D_MODEL), F32)] + cast_shapes,
        compiler_params=_params(48),
        name="mix_out",
    )(x, g, att, fou, w_in, gate_bias, w_na, w_f, w_o, *[arr for arr, _ in cast])
    return y, copies


def kernel(x, ffn1_norm, ffn1_w_in, ffn1_w_out, mix_norm, mix_w_in, mix_gate_bias, na_rpb,
           na_w_out, f_w_out, mix_w_o, ffn2_norm, ffn2_w_in, ffn2_w_out, final_norm):
    batch, seq, d = x.shape
    depth = ffn1_norm.shape[0]
    assert d == D_MODEL and seq == DFT_ROWS * DFT_COLS and seq % (GRID_W * NA_ROW_BLOCK) == 0
    assert (batch * seq) % TOKEN_TILE == 0 and (batch * seq) % FFN_TOKEN_TILE == 0
    tables = _dft_tables()
    gain = lambda g: g.reshape(depth, 1, D_MODEL)
    g1, g2, mix_g = gain(ffn1_norm), gain(ffn2_norm), gain(mix_norm)
    gate_bias = mix_gate_bias.reshape(depth, 1, 2 * D_MODEL)
    xs = x.reshape(batch * seq, d)
    ffn1_w = [ffn1_w_in[0].astype(BF16), ffn1_w_out[0].astype(BF16)]
    for l in range(depth):
        xs, (w_in, w_na, w_f, w_o) = _ffn(
            xs, l, g1, *ffn1_w,
            cast=[(mix_w_in, l), (na_w_out, l), (f_w_out, l), (mix_w_o, l)])
        q, k, v, u = _proj(xs, l, mix_g, w_in)
        att = _na(q, k, v, _na_bias_table(na_rpb[l]), batch, seq)
        fou = _fourier(u, tables, batch, seq)
        xs, ffn2_w = _mixout(xs, l, mix_g, att, fou, w_in, gate_bias, w_na, w_f, w_o,
                             cast=[(ffn2_w_in, l), (ffn2_w_out, l)])
        last = l == depth - 1
        xs, ffn1_w = _ffn(xs, l, g2, *ffn2_w, final_g=final_norm if last else None,
                          cast=[] if last else [(ffn1_w_in, l + 1), (ffn1_w_out, l + 1)])
    return xs.reshape(batch, seq, d)
```

```python
import functools
import math

import jax
import jax.numpy as jnp
from jax import lax
from jax.experimental import pallas as pl
from jax.experimental.pallas import tpu as pltpu

D_MODEL = 1024
GRID_W = 64
NA_HEADS = 8
NA_HEAD_DIM = 64
NA_WIDTH = NA_HEADS * NA_HEAD_DIM
NA_ROWS = 8
NA_COLS = 16
F_GROUPS = 4
F_GROUP_DIM = 128
F_WIDTH = F_GROUPS * F_GROUP_DIM
D_FF = 2816
RMS_EPS = 1e-6

BF16 = jnp.bfloat16
F32 = jnp.float32

MXU_COLS_V7X = 256
VMEM_BYTES_V7X = 64 * 1024 * 1024

TOKEN_TILE = 1024
SUB_TILE = 512
FFN_TOKEN_TILE = 1024
FF_CHUNK = MXU_COLS_V7X
NA_ROW_BLOCK = 8
DFT_ROWS = 128
DFT_COLS = 64
SUBLANES_F32 = 8
BF16_SUBLANES = 16
F1_T2_BLOCK = SUBLANES_F32
F3_K1_BLOCK = 8


def _params(vmem_mib):
    return pltpu.CompilerParams(
        dimension_semantics=None,
        vmem_limit_bytes=vmem_mib * 1024 * 1024,
    )


def _const_spec(shape, col_block=0):
    index = (0,) * (len(shape) - 1) + (col_block,)
    return pl.BlockSpec(shape, lambda *_: index, pipeline_mode=pl.Buffered(1))


def _cast_specs(params, steps):
    in_specs, out_specs, out_shapes = [], [], []
    for arr, layer in params:
        _, rows, cols = arr.shape
        nblk = steps
        while rows % nblk or (rows // nblk) % BF16_SUBLANES:
            nblk //= 2
        blk = rows // nblk
        in_specs.append(pl.BlockSpec(
            (None, blk, cols), lambda i, layer=layer, nblk=nblk: (layer, jnp.minimum(i, nblk - 1), 0)))
        out_specs.append(pl.BlockSpec(
            (blk, cols), lambda i, nblk=nblk: (jnp.minimum(i, nblk - 1), 0)))
        out_shapes.append(jax.ShapeDtypeStruct((rows, cols), BF16))
    return in_specs, out_specs, out_shapes


def _cast_blocks(src_refs, dst_refs):
    for src, dst in zip(src_refs, dst_refs):
        dst[...] = src[...].astype(BF16)


def _layer_spec(shape, layer):
    index = (layer,) + (0,) * len(shape)
    return pl.BlockSpec((None,) + tuple(shape), lambda *_: index, pipeline_mode=pl.Buffered(1))


def _rms(x, g):
    ms = jnp.mean(x * x, axis=-1, keepdims=True)
    return x * lax.rsqrt(ms + RMS_EPS) * g


def _dot(a, b):
    return jnp.dot(a, b, preferred_element_type=F32)


def _ffn_kernel(x_ref, g_ref, win_ref, wout_ref, *rest, final, n_cast):
    rest = list(rest)
    gf_ref = rest.pop(0) if final else None
    cast_src, o_ref, cast_dst, act_ref = (rest[:n_cast], rest[n_cast],
                                          rest[n_cast + 1:2 * n_cast + 1], rest[-1])
    _cast_blocks(cast_src, cast_dst)
    x = x_ref[...]
    h = _rms(x, g_ref[...]).astype(BF16)
    for c in range(D_FF // FF_CHUNK):
        lo = c * FF_CHUNK
        g = _dot(h, win_ref[:, lo:lo + FF_CHUNK])
        u = _dot(h, win_ref[:, D_FF + lo:D_FF + lo + FF_CHUNK])
        act_ref[:, lo:lo + FF_CHUNK] = (g * jax.nn.sigmoid(g) * u).astype(BF16)
    y = x + 0.5 * _dot(act_ref[...], wout_ref[...])
    if final:
        y = _rms(y, gf_ref[...])
    o_ref[...] = y


def _ffn(x, layer, g, w_in, w_out, final_g=None, cast=()):
    n = x.shape[0]
    steps = n // FFN_TOKEN_TILE
    final = final_g is not None
    tile = pl.BlockSpec((FFN_TOKEN_TILE, D_MODEL), lambda i: (i, 0))
    in_specs = [tile, _layer_spec((1, D_MODEL), layer), _const_spec((D_MODEL, 2 * D_FF)),
                _const_spec((D_FF, D_MODEL))]
    args = [x, g, w_in, w_out]
    if final:
        in_specs.append(_const_spec((1, D_MODEL)))
        args.append(final_g.reshape(1, D_MODEL))
    cast_in, cast_out, cast_shapes = _cast_specs(cast, steps)
    y, *copies = pl.pallas_call(
        functools.partial(_ffn_kernel, final=final, n_cast=len(cast)),
        grid=(steps,),
        in_specs=in_specs + cast_in,
        out_specs=[tile] + cast_out,
        out_shape=[jax.ShapeDtypeStruct((n, D_MODEL), F32)] + cast_shapes,
        scratch_shapes=[pltpu.VMEM((FFN_TOKEN_TILE, D_FF), BF16)],
        compiler_params=_params(58),
        name="ffn",
    )(*args, *[arr for arr, _ in cast])
    return y, copies


def _proj_kernel(x_ref, g_ref, w_ref, q_ref, k_ref, v_ref, u_ref):
    s = NA_WIDTH
    for t in range(TOKEN_TILE // SUB_TILE):
        rows = slice(t * SUB_TILE, (t + 1) * SUB_TILE)
        h = _rms(x_ref[rows, :], g_ref[...]).astype(BF16)
        z = _dot(h, w_ref[...])
        q_ref[rows, :] = (z[:, :s] * (NA_HEAD_DIM ** -0.5)).astype(BF16)
        k_ref[rows, :] = z[:, s:2 * s].astype(BF16)
        v_ref[rows, :] = z[:, 2 * s:3 * s].astype(BF16)
        u_ref[rows, :] = z[:, 3 * s:3 * s + F_WIDTH]


def _proj(x, layer, g, w):
    n = x.shape[0]
    cols = 3 * NA_WIDTH + F_WIDTH
    assert w.shape[1] == 2 * cols
    out_tile = pl.BlockSpec((TOKEN_TILE, NA_WIDTH), lambda i: (i, 0))
    out_sds = jax.ShapeDtypeStruct((n, NA_WIDTH), BF16)
    return pl.pallas_call(
        _proj_kernel,
        grid=(n // TOKEN_TILE,),
        in_specs=[pl.BlockSpec((TOKEN_TILE, D_MODEL), lambda i: (i, 0)),
                  _layer_spec((1, D_MODEL), layer), _const_spec((D_MODEL, cols), 0)],
        out_specs=[out_tile] * 4,
        out_shape=[out_sds] * 3 + [jax.ShapeDtypeStruct((n, F_WIDTH), F32)],
        compiler_params=_params(40),
        name="mix_proj",
    )(x, g, w)


def _na_bias_table(rpb):
    qc = jnp.arange(GRID_W)
    kc = jnp.arange(GRID_W)
    ws = jnp.clip(qc - NA_COLS // 2, 0, GRID_W - NA_COLS)
    ok = (kc[None, :] >= ws[:, None]) & (kc[None, :] < ws[:, None] + NA_COLS)
    dc = kc[None, :] - qc[:, None] + (NA_COLS - 1)
    pick = (dc[:, :, None] == jnp.arange(2 * NA_COLS - 1)[None, None, :]).astype(F32)
    toep = jnp.einsum('hdj,qkj->hdqk', rpb.astype(F32), pick,
                      precision=lax.Precision.HIGHEST)
    toep = jnp.where(ok[None, None], toep, -jnp.inf)
    return jnp.concatenate([toep[:, :-1], toep[:, 1:]], axis=-1)


def _na_kernel(q_ref, k_ref, v_ref, bias_ref, hm_ref, o_ref, *, rows):
    blk = pl.program_id(1)
    kh = NA_ROWS
    half = kh // 2
    lane_lo = lax.broadcasted_iota(jnp.int32, (GRID_W, 128), 1) < NA_HEAD_DIM

    def window(i):
        r = blk * NA_ROW_BLOCK + i
        rs = jnp.clip(r - half, 0, rows - kh)
        dr0 = rs - r + (NA_ROWS - 1)
        return pl.multiple_of(rs * GRID_W, GRID_W), dr0

    def bias(h, dr0):
        return jnp.concatenate([bias_ref[h, dr0 + 2 * j] for j in range(kh // 2)], axis=-1)

    def score_stage(i):
        koff, _ = window(i)
        ss = []
        for p in range(NA_HEADS // 2):
            lanes = slice(p * 128, (p + 1) * 128)
            qp = q_ref[i * GRID_W:(i + 1) * GRID_W, lanes]
            ql = jnp.concatenate([qp * hm_ref[0], qp * hm_ref[1]], axis=0)
            kw = k_ref[0, pl.ds(koff, kh * GRID_W), lanes]
            ss.append(lax.dot_general(ql, kw, (((1,), (1,)), ((), ())),
                                      preferred_element_type=F32))
        return ss

    def output_stage(i, ss):
        koff, dr0 = window(i)
        es, ls = [], []
        for p in range(NA_HEADS // 2):
            s = ss[p] + jnp.concatenate([bias(2 * p, dr0), bias(2 * p + 1, dr0)], axis=0)
            m = jnp.max(s, axis=-1, keepdims=True)
            e = jnp.exp(s - m)
            ls.append(jnp.sum(e, axis=-1, keepdims=True))
            es.append(e.astype(BF16))
        for p in range(NA_HEADS // 2):
            lanes = slice(p * 128, (p + 1) * 128)
            vw = v_ref[0, pl.ds(koff, kh * GRID_W), lanes]
            pv = _dot(es[p], vw) / ls[p]
            o = jnp.where(lane_lo, pv[:GRID_W], pv[GRID_W:])
            o_ref[i * GRID_W:(i + 1) * GRID_W, lanes] = o.astype(BF16)

    ss = score_stage(0)
    for i in range(NA_ROW_BLOCK):
        nxt = score_stage(i + 1) if i + 1 < NA_ROW_BLOCK else None
        output_stage(i, ss)
        ss = nxt


def _na(q, k, v, bias, batch, seq):
    rows = seq // GRID_W
    assert rows >= NA_ROWS and NA_ROWS % 2 == 0
    blk_tokens = NA_ROW_BLOCK * GRID_W
    nblk = rows // NA_ROW_BLOCK
    head_mask = (jnp.arange(128)[None, None, :] // NA_HEAD_DIM
                 == jnp.arange(2)[:, None, None]).astype(BF16)
    head_mask = jnp.broadcast_to(head_mask, (2, GRID_W, 128))
    q_tile = pl.BlockSpec((blk_tokens, NA_WIDTH), lambda b, i: (b * nblk + i, 0))
    kv_spec = pl.BlockSpec((1, seq, NA_WIDTH), lambda b, i: (b, 0, 0))
    return pl.pallas_call(
        functools.partial(_na_kernel, rows=rows),
        grid=(batch, nblk),
        in_specs=[q_tile, kv_spec, kv_spec,
                  _const_spec((NA_HEADS, 2 * NA_ROWS - 2, GRID_W, 2 * GRID_W)),
                  _const_spec((2, GRID_W, 128))],
        out_specs=q_tile,
        out_shape=jax.ShapeDtypeStruct((batch * seq, NA_WIDTH), BF16),
        compiler_params=_params(56),
        name="na",
    )(q, k.reshape(batch, seq, NA_WIDTH), v.reshape(batch, seq, NA_WIDTH), bias, head_mask)


def _dft_tables():
    two_pi = 2.0 * math.pi
    n = DFT_ROWS * DFT_COLS
    k1 = jnp.arange(DFT_ROWS)
    ang = ((k1[:, None] * k1[None, :]) % DFT_ROWS).astype(F32) * (two_pi / DFT_ROWS)
    f = jnp.concatenate([jnp.cos(ang), -jnp.sin(ang)], axis=0)
    f_wide = jnp.repeat(f, F1_T2_BLOCK, axis=1)
    same_t2 = (jnp.arange(DFT_ROWS * F1_T2_BLOCK)[None, None, :] % F1_T2_BLOCK
               == jnp.arange(F1_T2_BLOCK)[None, :, None])
    f1 = jnp.where(same_t2, f_wide[:, None, :], 0.0).reshape(
        2 * DFT_ROWS * F1_T2_BLOCK, DFT_ROWS * F1_T2_BLOCK)
    k2 = jnp.arange(DFT_COLS)
    t2 = jnp.arange(DFT_COLS)
    a = ((k2[:, None] * t2[None, :]) % DFT_COLS).astype(F32) * (two_pi / DFT_COLS)
    b = ((k1[:, None] * t2[None, :]) % n).astype(F32) * (two_pi / n)
    ca, sa = jnp.cos(a)[None], jnp.sin(a)[None]
    cb, sb = jnp.cos(b)[:, None, :], jnp.sin(b)[:, None, :]
    gr = ca * cb - sa * sb
    gi = -(sa * cb + ca * sb)
    g = jnp.concatenate([jnp.concatenate([gr, -gi], axis=2),
                         jnp.concatenate([gi, gr], axis=2)], axis=1)
    c = jnp.arange(F_GROUP_DIM)
    ang = ((c[:, None] * c[None, :]) % F_GROUP_DIM).astype(F32) * (two_pi / F_GROUP_DIM)
    eye = jnp.eye(MXU_COLS_V7X // F_GROUP_DIM, dtype=F32)
    cc = jnp.kron(eye, jnp.cos(ang))
    ss = jnp.kron(eye, jnp.sin(ang))
    return f1.astype(BF16), g.astype(BF16), cc.astype(BF16), ss.astype(BF16)


def _f1_kernel(f_ref, x_ref, o_ref):
    x = x_ref[0].reshape(DFT_ROWS * F1_T2_BLOCK, F_WIDTH).astype(BF16)
    z = _dot(f_ref[...], x)
    o_ref[0] = z.reshape(2, DFT_ROWS, F1_T2_BLOCK, F_WIDTH).astype(BF16)


def _f3_kernel(a_ref, g_ref, cc_ref, ss_ref, o_ref, zr_ref, zi_ref):
    for j in range(F3_K1_BLOCK):
        a = jnp.concatenate([a_ref[0, 0, j], a_ref[0, 1, j]], axis=0)
        z = _dot(g_ref[j], a)
        zr_ref[j * DFT_COLS:(j + 1) * DFT_COLS] = z[:DFT_COLS].astype(BF16)
        zi_ref[j * DFT_COLS:(j + 1) * DFT_COLS] = z[DFT_COLS:].astype(BF16)
    scale = 1.0 / math.sqrt(DFT_ROWS * DFT_COLS * F_GROUP_DIM)
    for p in range(F_WIDTH // MXU_COLS_V7X):
        cols = slice(p * MXU_COLS_V7X, (p + 1) * MXU_COLS_V7X)
        y = _dot(zr_ref[:, cols], cc_ref[...]) + _dot(zi_ref[:, cols], ss_ref[...])
        y = (y * scale).astype(BF16)
        for j in range(F3_K1_BLOCK):
            o_ref[0, j, :, cols] = y[j * DFT_COLS:(j + 1) * DFT_COLS]


def _fourier(u, tables, batch, seq):
    f1, g, cc, ss = tables
    f1_rows = 2 * DFT_ROWS * F1_T2_BLOCK
    a = pl.pallas_call(
        _f1_kernel,
        grid=(batch, DFT_COLS // F1_T2_BLOCK),
        in_specs=[_const_spec((f1_rows, f1_rows // 2)),
                  pl.BlockSpec((1, DFT_ROWS, F1_T2_BLOCK, F_WIDTH), lambda b, i: (b, 0, i, 0))],
        out_specs=pl.BlockSpec((1, 2, DFT_ROWS, F1_T2_BLOCK, F_WIDTH), lambda b, i: (b, 0, 0, i, 0)),
        out_shape=jax.ShapeDtypeStruct((batch, 2, DFT_ROWS, DFT_COLS, F_WIDTH), BF16),
        compiler_params=_params(40),
        name="dft_rows",
    )(f1, u.reshape(batch, DFT_ROWS, DFT_COLS, F_WIDTH))
    y = pl.pallas_call(
        _f3_kernel,
        grid=(batch, DFT_ROWS // F3_K1_BLOCK),
        in_specs=[pl.BlockSpec((1, 2, F3_K1_BLOCK, DFT_COLS, F_WIDTH), lambda b, i: (b, 0, i, 0, 0)),
                  pl.BlockSpec((F3_K1_BLOCK, 2 * DFT_COLS, 2 * DFT_COLS), lambda b, i: (i, 0, 0)),
                  _const_spec((MXU_COLS_V7X, MXU_COLS_V7X)), _const_spec((MXU_COLS_V7X, MXU_COLS_V7X))],
        out_specs=pl.BlockSpec((1, F3_K1_BLOCK, DFT_COLS, F_WIDTH), lambda b, i: (b, i, 0, 0)),
        out_shape=jax.ShapeDtypeStruct((batch, DFT_ROWS, DFT_COLS, F_WIDTH), BF16),
        scratch_shapes=[pltpu.VMEM((F3_K1_BLOCK * DFT_COLS, F_WIDTH), BF16),
                        pltpu.VMEM((F3_K1_BLOCK * DFT_COLS, F_WIDTH), BF16)],
        compiler_params=_params(32),
        name="dft_cols_channels",
    )(a, g, cc, ss)
    return jnp.transpose(y, (0, 2, 1, 3)).reshape(batch * seq, F_WIDTH)


def _mixout_kernel(x_ref, g_ref, att_ref, fou_ref, wg_ref, gb_ref, wna_ref, wf_ref, wo_ref,
                   *rest, n_cast):
    cast_src, o_ref, cast_dst = rest[:n_cast], rest[n_cast], rest[n_cast + 1:]
    _cast_blocks(cast_src, cast_dst)
    for t in range(TOKEN_TILE // SUB_TILE):
        rows = slice(t * SUB_TILE, (t + 1) * SUB_TILE)
        x = x_ref[rows, :]
        h = _rms(x, g_ref[...]).astype(BF16)
        gates = jax.nn.sigmoid(_dot(h, wg_ref[...]) + gb_ref[...])
        y_na = _dot(att_ref[rows, :], wna_ref[...])
        y_f = _dot(fou_ref[rows, :], wf_ref[...])
        m = gates[:, :D_MODEL] * y_na + gates[:, D_MODEL:] * y_f
        o_ref[rows, :] = x + _dot(m.astype(BF16), wo_ref[...])


def _mixout(x, layer, g, att, fou, w_in, gate_bias, w_na, w_f, w_o, cast=()):
    n = x.shape[0]
    steps = n // TOKEN_TILE
    tile = pl.BlockSpec((TOKEN_TILE, D_MODEL), lambda i: (i, 0))
    half = pl.BlockSpec((TOKEN_TILE, NA_WIDTH), lambda i: (i, 0))
    assert w_in.shape[1] == 4 * D_MODEL
    cast_in, cast_out, cast_shapes = _cast_specs(cast, steps)
    y, *copies = pl.pallas_call(
        functools.partial(_mixout_kernel, n_cast=len(cast)),
        grid=(steps,),
        in_specs=[tile, _layer_spec((1, D_MODEL), layer), half, half,
                  _const_spec((D_MODEL, 2 * D_MODEL), 1),
                  _layer_spec((1, 2 * D_MODEL), layer),
                  _const_spec((NA_WIDTH, D_MODEL)), _const_spec((F_WIDTH, D_MODEL)),
                  _const_spec((D_MODEL, D_MODEL))] + cast_in,
        out_specs=[tile] + cast_out,
        out_shape=[jax.ShapeDtypeStruct((n, D_MODEL), F32)] + cast_shapes,
        compiler_params=_params(48),
        name="mix_out",
    )(x, g, att, fou, w_in, gate_bias, w_na, w_f, w_o, *[arr for arr, _ in cast])
    return y, copies


def kernel(x, ffn1_norm, ffn1_w_in, ffn1_w_out, mix_norm, mix_w_in, mix_gate_bias, na_rpb,
           na_w_out, f_w_out, mix_w_o, ffn2_norm, ffn2_w_in, ffn2_w_out, final_norm):
    batch, seq, d = x.shape
    depth = ffn1_norm.shape[0]
    assert d == D_MODEL and seq == DFT_ROWS * DFT_COLS and seq % (GRID_W * NA_ROW_BLOCK) == 0
    assert (batch * seq) % TOKEN_TILE == 0 and (batch * seq) % FFN_TOKEN_TILE == 0
    tables = _dft_tables()
    gain = lambda g: g.reshape(depth, 1, D_MODEL)
    g1, g2, mix_g = gain(ffn1_norm), gain(ffn2_norm), gain(mix_norm)
    gate_bias = mix_gate_bias.reshape(depth, 1, 2 * D_MODEL)
    xs = x.reshape(batch * seq, d)
    ffn1_w = [ffn1_w_in[0].astype(BF16), ffn1_w_out[0].astype(BF16)]
    for l in range(depth):
        xs, (w_in, w_na, w_f, w_o) = _ffn(
            xs, l, g1, *ffn1_w,
            cast=[(mix_w_in, l), (na_w_out, l), (f_w_out, l), (mix_w_o, l)])
        q, k, v, u = _proj(xs, l, mix_g, w_in)
        att = _na(q, k, v, _na_bias_table(na_rpb[l]), batch, seq)
        fou = _fourier(u, tables, batch, seq)
        xs, ffn2_w = _mixout(xs, l, mix_g, att, fou, w_in, gate_bias, w_na, w_f, w_o,
                             cast=[(ffn2_w_in, l), (ffn2_w_out, l)])
        last = l == depth - 1
        xs, ffn1_w = _ffn(xs, l, g2, *ffn2_w, final_g=final_norm if last else None,
                          cast=[] if last else [(ffn1_w_in, l + 1), (ffn1_w_out, l + 1)])
    return xs.reshape(batch, seq, d)
```

```python
import functools
import math

import jax
import jax.numpy as jnp
from jax import lax
from jax.experimental import pallas as pl
from jax.experimental.pallas import tpu as pltpu

D_MODEL = 1024
GRID_W = 64
NA_HEADS = 8
NA_HEAD_DIM = 64
NA_WIDTH = NA_HEADS * NA_HEAD_DIM
NA_ROWS = 8
NA_COLS = 16
F_GROUPS = 4
F_GROUP_DIM = 128
F_WIDTH = F_GROUPS * F_GROUP_DIM
D_FF = 2816
RMS_EPS = 1e-6

BF16 = jnp.bfloat16
F32 = jnp.float32

MXU_COLS_V7X = 256
VMEM_BYTES_V7X = 64 * 1024 * 1024

TOKEN_TILE = 1024
SUB_TILE = 512
FFN_TOKEN_TILE = 1024
FF_CHUNK = MXU_COLS_V7X
NA_ROW_BLOCK = 16
DFT_ROWS = 128
DFT_COLS = 64
SUBLANES_F32 = 8
BF16_SUBLANES = 16
F1_T2_BLOCK = SUBLANES_F32
F3_K1_BLOCK = 16


def _params(vmem_mib):
    return pltpu.CompilerParams(
        dimension_semantics=None,
        vmem_limit_bytes=vmem_mib * 1024 * 1024,
    )


def _const_spec(shape, col_block=0):
    index = (0,) * (len(shape) - 1) + (col_block,)
    return pl.BlockSpec(shape, lambda *_: index, pipeline_mode=pl.Buffered(1))


def _cast_specs(params, steps):
    in_specs, out_specs, out_shapes = [], [], []
    for arr, layer in params:
        _, rows, cols = arr.shape
        nblk = steps
        while rows % nblk or (rows // nblk) % BF16_SUBLANES:
            nblk //= 2
        blk = rows // nblk
        in_specs.append(pl.BlockSpec(
            (None, blk, cols), lambda i, layer=layer, nblk=nblk: (layer, jnp.minimum(i, nblk - 1), 0)))
        out_specs.append(pl.BlockSpec(
            (blk, cols), lambda i, nblk=nblk: (jnp.minimum(i, nblk - 1), 0)))
        out_shapes.append(jax.ShapeDtypeStruct((rows, cols), BF16))
    return in_specs, out_specs, out_shapes


def _cast_blocks(src_refs, dst_refs):
    for src, dst in zip(src_refs, dst_refs):
        dst[...] = src[...].astype(BF16)


def _layer_spec(shape, layer):
    index = (layer,) + (0,) * len(shape)
    return pl.BlockSpec((None,) + tuple(shape), lambda *_: index, pipeline_mode=pl.Buffered(1))


def _rms(x, g):
    ms = jnp.mean(x * x, axis=-1, keepdims=True)
    return x * lax.rsqrt(ms + RMS_EPS) * g


def _dot(a, b):
    return jnp.dot(a, b, preferred_element_type=F32)


def _ffn_kernel(x_ref, g_ref, win_ref, wout_ref, *rest, final, n_cast):
    rest = list(rest)
    gf_ref = rest.pop(0) if final else None
    cast_src, o_ref, cast_dst, act_ref = (rest[:n_cast], rest[n_cast],
                                          rest[n_cast + 1:2 * n_cast + 1], rest[-1])
    _cast_blocks(cast_src, cast_dst)
    x = x_ref[...]
    h = _rms(x, g_ref[...]).astype(BF16)
    for c in range(D_FF // FF_CHUNK):
        lo = c * FF_CHUNK
        g = _dot(h, win_ref[:, lo:lo + FF_CHUNK])
        u = _dot(h, win_ref[:, D_FF + lo:D_FF + lo + FF_CHUNK])
        act_ref[:, lo:lo + FF_CHUNK] = (g * jax.nn.sigmoid(g) * u).astype(BF16)
    y = x + 0.5 * _dot(act_ref[...], wout_ref[...])
    if final:
        y = _rms(y, gf_ref[...])
    o_ref[...] = y


def _ffn(x, layer, g, w_in, w_out, final_g=None, cast=()):
    n = x.shape[0]
    steps = n // FFN_TOKEN_TILE
    final = final_g is not None
    tile = pl.BlockSpec((FFN_TOKEN_TILE, D_MODEL), lambda i: (i, 0))
    in_specs = [tile, _layer_spec((1, D_MODEL), layer), _const_spec((D_MODEL, 2 * D_FF)),
                _const_spec((D_FF, D_MODEL))]
    args = [x, g, w_in, w_out]
    if final:
        in_specs.append(_const_spec((1, D_MODEL)))
        args.append(final_g.reshape(1, D_MODEL))
    cast_in, cast_out, cast_shapes = _cast_specs(cast, steps)
    y, *copies = pl.pallas_call(
        functools.partial(_ffn_kernel, final=final, n_cast=len(cast)),
        grid=(steps,),
        in_specs=in_specs + cast_in,
        out_specs=[tile] + cast_out,
        out_shape=[jax.ShapeDtypeStruct((n, D_MODEL), F32)] + cast_shapes,
        scratch_shapes=[pltpu.VMEM((FFN_TOKEN_TILE, D_FF), BF16)],
        compiler_params=_params(58),
        name="ffn",
    )(*args, *[arr for arr, _ in cast])
    return y, copies


def _proj_kernel(x_ref, g_ref, w_ref, q_ref, k_ref, v_ref, u_ref):
    s = NA_WIDTH
    for t in range(TOKEN_TILE // SUB_TILE):
        rows = slice(t * SUB_TILE, (t + 1) * SUB_TILE)
        h = _rms(x_ref[rows, :], g_ref[...]).astype(BF16)
        z = _dot(h, w_ref[...])
        q_ref[rows, :] = (z[:, :s] * (NA_HEAD_DIM ** -0.5)).astype(BF16)
        k_ref[rows, :] = z[:, s:2 * s].astype(BF16)
        v_ref[rows, :] = z[:, 2 * s:3 * s].astype(BF16)
        u_ref[rows, :] = z[:, 3 * s:3 * s + F_WIDTH]


def _proj(x, layer, g, w):
    n = x.shape[0]
    cols = 3 * NA_WIDTH + F_WIDTH
    assert w.shape[1] == 2 * cols
    out_tile = pl.BlockSpec((TOKEN_TILE, NA_WIDTH), lambda i: (i, 0))
    out_sds = jax.ShapeDtypeStruct((n, NA_WIDTH), BF16)
    return pl.pallas_call(
        _proj_kernel,
        grid=(n // TOKEN_TILE,),
        in_specs=[pl.BlockSpec((TOKEN_TILE, D_MODEL), lambda i: (i, 0)),
                  _layer_spec((1, D_MODEL), layer), _const_spec((D_MODEL, cols), 0)],
        out_specs=[out_tile] * 4,
        out_shape=[out_sds] * 3 + [jax.ShapeDtypeStruct((n, F_WIDTH), F32)],
        compiler_params=_params(40),
        name="mix_proj",
    )(x, g, w)


def _na_bias_tables(rpb):
    ncol = 2 * NA_COLS - 1
    qc = jnp.arange(GRID_W)
    lane = jnp.arange(2 * GRID_W)
    kc, side = lane % GRID_W, lane // GRID_W
    ws = jnp.clip(qc - NA_COLS // 2, 0, GRID_W - NA_COLS)
    ok = (kc[None, :] >= ws[:, None]) & (kc[None, :] < ws[:, None] + NA_COLS)
    dc = kc[None, :] - qc[:, None] + (NA_COLS - 1)
    src = jnp.where(ok, side[None, :] * ncol + dc, -1)
    pick = (src[:, :, None] == jnp.arange(2 * ncol)[None, None, :]).astype(F32)
    both = jnp.concatenate([rpb[:, :, :-1], rpb[:, :, 1:]], axis=-1).astype(F32)
    table = jnp.einsum('lhdm,qkm->lhdqk', both, pick, precision=lax.Precision.HIGHEST)
    return jnp.where(ok[None, None, None], table, -jnp.inf)


def _na_kernel(q_ref, k_ref, v_ref, bias_ref, hm_ref, o_ref, *, rows):
    blk = pl.program_id(1)
    kh = NA_ROWS
    half = kh // 2
    lane_lo = lax.broadcasted_iota(jnp.int32, (GRID_W, 128), 1) < NA_HEAD_DIM

    def window(i):
        r = blk * NA_ROW_BLOCK + i
        rs = jnp.clip(r - half, 0, rows - kh)
        dr0 = rs - r + (NA_ROWS - 1)
        return pl.multiple_of(rs * GRID_W, GRID_W), dr0

    def bias(h, dr0):
        return jnp.concatenate([bias_ref[h, dr0 + 2 * j] for j in range(kh // 2)], axis=-1)

    def score_stage(i):
        koff, _ = window(i)
        ss = []
        for p in range(NA_HEADS // 2):
            lanes = slice(p * 128, (p + 1) * 128)
            qp = q_ref[i * GRID_W:(i + 1) * GRID_W, lanes]
            ql = jnp.concatenate([qp * hm_ref[0], qp * hm_ref[1]], axis=0)
            kw = k_ref[0, pl.ds(koff, kh * GRID_W), lanes]
            ss.append(lax.dot_general(ql, kw, (((1,), (1,)), ((), ())),
                                      preferred_element_type=F32))
        return ss

    def output_stage(i, ss):
        koff, dr0 = window(i)
        es, ls = [], []
        for p in range(NA_HEADS // 2):
            s = ss[p] + jnp.concatenate([bias(2 * p, dr0), bias(2 * p + 1, dr0)], axis=0)
            m = jnp.max(s, axis=-1, keepdims=True)
            e = jnp.exp(s - m)
            ls.append(jnp.sum(e, axis=-1, keepdims=True))
            es.append(e.astype(BF16))
        for p in range(NA_HEADS // 2):
            lanes = slice(p * 128, (p + 1) * 128)
            vw = v_ref[0, pl.ds(koff, kh * GRID_W), lanes]
            pv = _dot(es[p], vw) / ls[p]
            o = jnp.where(lane_lo, pv[:GRID_W], pv[GRID_W:])
            o_ref[i * GRID_W:(i + 1) * GRID_W, lanes] = o.astype(BF16)

    ss = score_stage(0)
    for i in range(NA_ROW_BLOCK):
        nxt = score_stage(i + 1) if i + 1 < NA_ROW_BLOCK else None
        output_stage(i, ss)
        ss = nxt


def _na(q, k, v, layer, bias, batch, seq):
    rows = seq // GRID_W
    assert rows >= NA_ROWS and NA_ROWS % 2 == 0
    blk_tokens = NA_ROW_BLOCK * GRID_W
    nblk = rows // NA_ROW_BLOCK
    head_mask = (jnp.arange(128)[None, None, :] // NA_HEAD_DIM
                 == jnp.arange(2)[:, None, None]).astype(BF16)
    head_mask = jnp.broadcast_to(head_mask, (2, GRID_W, 128))
    q_tile = pl.BlockSpec((blk_tokens, NA_WIDTH), lambda b, i: (b * nblk + i, 0))
    kv_spec = pl.BlockSpec((1, seq, NA_WIDTH), lambda b, i: (b, 0, 0))
    return pl.pallas_call(
        functools.partial(_na_kernel, rows=rows),
        grid=(batch, nblk),
        in_specs=[q_tile, kv_spec, kv_spec,
                  _layer_spec((NA_HEADS, 2 * NA_ROWS - 2, GRID_W, 2 * GRID_W), layer),
                  _const_spec((2, GRID_W, 128))],
        out_specs=q_tile,
        out_shape=jax.ShapeDtypeStruct((batch * seq, NA_WIDTH), BF16),
        compiler_params=_params(56),
        name="na",
    )(q, k.reshape(batch, seq, NA_WIDTH), v.reshape(batch, seq, NA_WIDTH), bias, head_mask)


def _dft_tables():
    two_pi = 2.0 * math.pi
    n = DFT_ROWS * DFT_COLS
    k1 = jnp.arange(DFT_ROWS)
    ang = ((k1[:, None] * k1[None, :]) % DFT_ROWS).astype(F32) * (two_pi / DFT_ROWS)
    f = jnp.concatenate([jnp.cos(ang), -jnp.sin(ang)], axis=0)
    f_wide = jnp.repeat(f, F1_T2_BLOCK, axis=1)
    same_t2 = (jnp.arange(DFT_ROWS * F1_T2_BLOCK)[None, None, :] % F1_T2_BLOCK
               == jnp.arange(F1_T2_BLOCK)[None, :, None])
    f1 = jnp.where(same_t2, f_wide[:, None, :], 0.0).reshape(
        2 * DFT_ROWS * F1_T2_BLOCK, DFT_ROWS * F1_T2_BLOCK)
    k2 = jnp.arange(DFT_COLS)
    t2 = jnp.arange(DFT_COLS)
    a = ((k2[:, None] * t2[None, :]) % DFT_COLS).astype(F32) * (two_pi / DFT_COLS)
    b = ((k1[:, None] * t2[None, :]) % n).astype(F32) * (two_pi / n)
    ca, sa = jnp.cos(a)[None], jnp.sin(a)[None]
    cb, sb = jnp.cos(b)[:, None, :], jnp.sin(b)[:, None, :]
    gr = ca * cb - sa * sb
    gi = -(sa * cb + ca * sb)
    g = jnp.concatenate([jnp.concatenate([gr, -gi], axis=2),
                         jnp.concatenate([gi, gr], axis=2)], axis=1)
    c = jnp.arange(F_GROUP_DIM)
    ang = ((c[:, None] * c[None, :]) % F_GROUP_DIM).astype(F32) * (two_pi / F_GROUP_DIM)
    eye = jnp.eye(MXU_COLS_V7X // F_GROUP_DIM, dtype=F32)
    cc = jnp.kron(eye, jnp.cos(ang))
    ss = jnp.kron(eye, jnp.sin(ang))
    return f1.astype(BF16), g.astype(BF16), cc.astype(BF16), ss.astype(BF16)


def _f1_kernel(f_ref, x_ref, o_ref):
    x = x_ref[0].reshape(DFT_ROWS * F1_T2_BLOCK, F_WIDTH).astype(BF16)
    z = _dot(f_ref[...], x)
    o_ref[0] = z.reshape(2, DFT_ROWS, F1_T2_BLOCK, F_WIDTH).astype(BF16)


def _f3_kernel(a_ref, g_ref, cc_ref, ss_ref, o_ref, zr_ref, zi_ref):
    for j in range(F3_K1_BLOCK):
        a = jnp.concatenate([a_ref[0, 0, j], a_ref[0, 1, j]], axis=0)
        z = _dot(g_ref[j], a)
        zr_ref[j * DFT_COLS:(j + 1) * DFT_COLS] = z[:DFT_COLS].astype(BF16)
        zi_ref[j * DFT_COLS:(j + 1) * DFT_COLS] = z[DFT_COLS:].astype(BF16)
    scale = 1.0 / math.sqrt(DFT_ROWS * DFT_COLS * F_GROUP_DIM)
    for p in range(F_WIDTH // MXU_COLS_V7X):
        cols = slice(p * MXU_COLS_V7X, (p + 1) * MXU_COLS_V7X)
        y = _dot(zr_ref[:, cols], cc_ref[...]) + _dot(zi_ref[:, cols], ss_ref[...])
        y = (y * scale).astype(BF16)
        for j in range(F3_K1_BLOCK):
            o_ref[0, j, :, cols] = y[j * DFT_COLS:(j + 1) * DFT_COLS]


def _fourier(u, tables, batch, seq):
    f1, g, cc, ss = tables
    f1_rows = 2 * DFT_ROWS * F1_T2_BLOCK
    a = pl.pallas_call(
        _f1_kernel,
        grid=(batch, DFT_COLS // F1_T2_BLOCK),
        in_specs=[_const_spec((f1_rows, f1_rows // 2)),
                  pl.BlockSpec((1, DFT_ROWS, F1_T2_BLOCK, F_WIDTH), lambda b, i: (b, 0, i, 0))],
        out_specs=pl.BlockSpec((1, 2, DFT_ROWS, F1_T2_BLOCK, F_WIDTH), lambda b, i: (b, 0, 0, i, 0)),
        out_shape=jax.ShapeDtypeStruct((batch, 2, DFT_ROWS, DFT_COLS, F_WIDTH), BF16),
        compiler_params=_params(40),
        name="dft_rows",
    )(f1, u.reshape(batch, DFT_ROWS, DFT_COLS, F_WIDTH))
    y = pl.pallas_call(
        _f3_kernel,
        grid=(batch, DFT_ROWS // F3_K1_BLOCK),
        in_specs=[pl.BlockSpec((1, 2, F3_K1_BLOCK, DFT_COLS, F_WIDTH), lambda b, i: (b, 0, i, 0, 0)),
                  pl.BlockSpec((F3_K1_BLOCK, 2 * DFT_COLS, 2 * DFT_COLS), lambda b, i: (i, 0, 0)),
                  _const_spec((MXU_COLS_V7X, MXU_COLS_V7X)), _const_spec((MXU_COLS_V7X, MXU_COLS_V7X))],
        out_specs=pl.BlockSpec((1, F3_K1_BLOCK, DFT_COLS, F_WIDTH), lambda b, i: (b, i, 0, 0)),
        out_shape=jax.ShapeDtypeStruct((batch, DFT_ROWS, DFT_COLS, F_WIDTH), BF16),
        scratch_shapes=[pltpu.VMEM((F3_K1_BLOCK * DFT_COLS, F_WIDTH), BF16),
                        pltpu.VMEM((F3_K1_BLOCK * DFT_COLS, F_WIDTH), BF16)],
        compiler_params=_params(32),
        name="dft_cols_channels",
    )(a, g, cc, ss)
    return jnp.transpose(y, (0, 2, 1, 3)).reshape(batch * seq, F_WIDTH)


def _mixout_kernel(x_ref, g_ref, att_ref, fou_ref, wg_ref, gb_ref, wna_ref, wf_ref, wo_ref,
                   *rest, n_cast):
    cast_src, o_ref, cast_dst = rest[:n_cast], rest[n_cast], rest[n_cast + 1:]
    _cast_blocks(cast_src, cast_dst)
    for t in range(TOKEN_TILE // SUB_TILE):
        rows = slice(t * SUB_TILE, (t + 1) * SUB_TILE)
        x = x_ref[rows, :]
        h = _rms(x, g_ref[...]).astype(BF16)
        gates = jax.nn.sigmoid(_dot(h, wg_ref[...]) + gb_ref[...])
        y_na = _dot(att_ref[rows, :], wna_ref[...])
        y_f = _dot(fou_ref[rows, :], wf_ref[...])
        m = gates[:, :D_MODEL] * y_na + gates[:, D_MODEL:] * y_f
        o_ref[rows, :] = x + _dot(m.astype(BF16), wo_ref[...])


def _mixout(x, layer, g, att, fou, w_in, gate_bias, w_na, w_f, w_o, cast=()):
    n = x.shape[0]
    steps = n // TOKEN_TILE
    tile = pl.BlockSpec((TOKEN_TILE, D_MODEL), lambda i: (i, 0))
    half = pl.BlockSpec((TOKEN_TILE, NA_WIDTH), lambda i: (i, 0))
    assert w_in.shape[1] == 4 * D_MODEL
    cast_in, cast_out, cast_shapes = _cast_specs(cast, steps)
    y, *copies = pl.pallas_call(
        functools.partial(_mixout_kernel, n_cast=len(cast)),
        grid=(steps,),
        in_specs=[tile, _layer_spec((1, D_MODEL), layer), half, half,
                  _const_spec((D_MODEL, 2 * D_MODEL), 1),
                  _layer_spec((1, 2 * D_MODEL), layer),
                  _const_spec((NA_WIDTH, D_MODEL)), _const_spec((F_WIDTH, D_MODEL)),
                  _const_spec((D_MODEL, D_MODEL))] + cast_in,
        out_specs=[tile] + cast_out,
        out_shape=[jax.ShapeDtypeStruct((n, D_MODEL), F32)] + cast_shapes,
        compiler_params=_params(48),
        name="mix_out",
    )(x, g, att, fou, w_in, gate_bias, w_na, w_f, w_o, *[arr for arr, _ in cast])
    return y, copies


def kernel(x, ffn1_norm, ffn1_w_in, ffn1_w_out, mix_norm, mix_w_in, mix_gate_bias, na_rpb,
           na_w_out, f_w_out, mix_w_o, ffn2_norm, ffn2_w_in, ffn2_w_out, final_norm):
    batch, seq, d = x.shape
    depth = ffn1_norm.shape[0]
    assert d == D_MODEL and seq == DFT_ROWS * DFT_COLS and seq % (GRID_W * NA_ROW_BLOCK) == 0
    assert (batch * seq) % TOKEN_TILE == 0 and (batch * seq) % FFN_TOKEN_TILE == 0
    tables = _dft_tables()
    na_bias = _na_bias_tables(na_rpb)
    gain = lambda g: g.reshape(depth, 1, D_MODEL)
    g1, g2, mix_g = gain(ffn1_norm), gain(ffn2_norm), gain(mix_norm)
    gate_bias = mix_gate_bias.reshape(depth, 1, 2 * D_MODEL)
    xs = x.reshape(batch * seq, d)
    ffn1_w = [ffn1_w_in[0].astype(BF16), ffn1_w_out[0].astype(BF16)]
    for l in range(depth):
        xs, (w_in, w_na, w_f, w_o) = _ffn(
            xs, l, g1, *ffn1_w,
            cast=[(mix_w_in, l), (na_w_out, l), (f_w_out, l), (mix_w_o, l)])
        q, k, v, u = _proj(xs, l, mix_g, w_in)
        att = _na(q, k, v, l, na_bias, batch, seq)
        fou = _fourier(u, tables, batch, seq)
        xs, ffn2_w = _mixout(xs, l, mix_g, att, fou, w_in, gate_bias, w_na, w_f, w_o,
                             cast=[(ffn2_w_in, l), (ffn2_w_out, l)])
        last = l == depth - 1
        xs, ffn1_w = _ffn(xs, l, g2, *ffn2_w, final_g=final_norm if last else None,
                          cast=[] if last else [(ffn1_w_in, l + 1), (ffn1_w_out, l + 1)])
    return xs.reshape(batch, seq, d)
```

```python
import functools
import math

import jax
import jax.numpy as jnp
from jax import lax
from jax.experimental import pallas as pl
from jax.experimental.pallas import tpu as pltpu

D_MODEL = 1024
GRID_W = 64
NA_HEADS = 8
NA_HEAD_DIM = 64
NA_WIDTH = NA_HEADS * NA_HEAD_DIM
NA_ROWS = 8
NA_COLS = 16
F_GROUPS = 4
F_GROUP_DIM = 128
F_WIDTH = F_GROUPS * F_GROUP_DIM
D_FF = 2816
RMS_EPS = 1e-6

BF16 = jnp.bfloat16
F32 = jnp.float32

MXU_COLS_V7X = 256
VMEM_BYTES_V7X = 64 * 1024 * 1024

TOKEN_TILE = 1024
SUB_TILE = 512
FFN_TOKEN_TILE = 1024
FF_CHUNK = MXU_COLS_V7X
NA_ROW_BLOCK = 16
DFT_ROWS = 64
DFT_COLS = 128
SUBLANES_F32 = 8
BF16_SUBLANES = 16
F1_T2_BLOCK = SUBLANES_F32
F3_K1_BLOCK = 16


def _params(vmem_mib):
    return pltpu.CompilerParams(
        dimension_semantics=None,
        vmem_limit_bytes=vmem_mib * 1024 * 1024,
    )


def _const_spec(shape, col_block=0):
    index = (0,) * (len(shape) - 1) + (col_block,)
    return pl.BlockSpec(shape, lambda *_: index, pipeline_mode=pl.Buffered(1))


def _cast_specs(params, steps):
    in_specs, out_specs, out_shapes = [], [], []
    for arr, layer in params:
        _, rows, cols = arr.shape
        nblk = steps
        while rows % nblk or (rows // nblk) % BF16_SUBLANES:
            nblk //= 2
        blk = rows // nblk
        in_specs.append(pl.BlockSpec(
            (None, blk, cols), lambda i, layer=layer, nblk=nblk: (layer, jnp.minimum(i, nblk - 1), 0)))
        out_specs.append(pl.BlockSpec(
            (blk, cols), lambda i, nblk=nblk: (jnp.minimum(i, nblk - 1), 0)))
        out_shapes.append(jax.ShapeDtypeStruct((rows, cols), BF16))
    return in_specs, out_specs, out_shapes


def _cast_blocks(src_refs, dst_refs):
    for src, dst in zip(src_refs, dst_refs):
        dst[...] = src[...].astype(BF16)


def _layer_spec(shape, layer):
    index = (layer,) + (0,) * len(shape)
    return pl.BlockSpec((None,) + tuple(shape), lambda *_: index, pipeline_mode=pl.Buffered(1))


def _rms(x, g):
    ms = jnp.mean(x * x, axis=-1, keepdims=True)
    return x * lax.rsqrt(ms + RMS_EPS) * g


def _dot(a, b):
    return jnp.dot(a, b, preferred_element_type=F32)


def _ffn_kernel(x_ref, g_ref, win_ref, wout_ref, *rest, final, n_cast):
    rest = list(rest)
    gf_ref = rest.pop(0) if final else None
    cast_src, o_ref, cast_dst, act_ref = (rest[:n_cast], rest[n_cast],
                                          rest[n_cast + 1:2 * n_cast + 1], rest[-1])
    _cast_blocks(cast_src, cast_dst)
    x = x_ref[...]
    h = _rms(x, g_ref[...]).astype(BF16)
    for c in range(D_FF // FF_CHUNK):
        lo = c * FF_CHUNK
        g = _dot(h, win_ref[:, lo:lo + FF_CHUNK])
        u = _dot(h, win_ref[:, D_FF + lo:D_FF + lo + FF_CHUNK])
        act_ref[:, lo:lo + FF_CHUNK] = (g * jax.nn.sigmoid(g) * u).astype(BF16)
    y = x + 0.5 * _dot(act_ref[...], wout_ref[...])
    if final:
        y = _rms(y, gf_ref[...])
    o_ref[...] = y


def _ffn(x, layer, g, w_in, w_out, final_g=None, cast=()):
    n = x.shape[0]
    steps = n // FFN_TOKEN_TILE
    final = final_g is not None
    tile = pl.BlockSpec((FFN_TOKEN_TILE, D_MODEL), lambda i: (i, 0))
    in_specs = [tile, _layer_spec((1, D_MODEL), layer), _const_spec((D_MODEL, 2 * D_FF)),
                _const_spec((D_FF, D_MODEL))]
    args = [x, g, w_in, w_out]
    if final:
        in_specs.append(_const_spec((1, D_MODEL)))
        args.append(final_g.reshape(1, D_MODEL))
    cast_in, cast_out, cast_shapes = _cast_specs(cast, steps)
    y, *copies = pl.pallas_call(
        functools.partial(_ffn_kernel, final=final, n_cast=len(cast)),
        grid=(steps,),
        in_specs=in_specs + cast_in,
        out_specs=[tile] + cast_out,
        out_shape=[jax.ShapeDtypeStruct((n, D_MODEL), F32)] + cast_shapes,
        scratch_shapes=[pltpu.VMEM((FFN_TOKEN_TILE, D_FF), BF16)],
        compiler_params=_params(58),
        name="ffn",
    )(*args, *[arr for arr, _ in cast])
    return y, copies


def _proj_kernel(x_ref, g_ref, w_ref, q_ref, k_ref, v_ref, u_ref):
    s = NA_WIDTH
    for t in range(TOKEN_TILE // SUB_TILE):
        rows = slice(t * SUB_TILE, (t + 1) * SUB_TILE)
        h = _rms(x_ref[rows, :], g_ref[...]).astype(BF16)
        z = _dot(h, w_ref[...])
        q_ref[rows, :] = (z[:, :s] * (NA_HEAD_DIM ** -0.5)).astype(BF16)
        k_ref[rows, :] = z[:, s:2 * s].astype(BF16)
        v_ref[rows, :] = z[:, 2 * s:3 * s].astype(BF16)
        u_ref[rows, :] = z[:, 3 * s:3 * s + F_WIDTH]


def _proj(x, layer, g, w):
    n = x.shape[0]
    cols = 3 * NA_WIDTH + F_WIDTH
    assert w.shape[1] == 2 * cols
    out_tile = pl.BlockSpec((TOKEN_TILE, NA_WIDTH), lambda i: (i, 0))
    out_sds = jax.ShapeDtypeStruct((n, NA_WIDTH), BF16)
    return pl.pallas_call(
        _proj_kernel,
        grid=(n // TOKEN_TILE,),
        in_specs=[pl.BlockSpec((TOKEN_TILE, D_MODEL), lambda i: (i, 0)),
                  _layer_spec((1, D_MODEL), layer), _const_spec((D_MODEL, cols), 0)],
        out_specs=[out_tile] * 4,
        out_shape=[out_sds] * 3 + [jax.ShapeDtypeStruct((n, F_WIDTH), F32)],
        compiler_params=_params(40),
        name="mix_proj",
    )(x, g, w)


def _na_bias_tables(rpb):
    ncol = 2 * NA_COLS - 1
    qc = jnp.arange(GRID_W)
    lane = jnp.arange(2 * GRID_W)
    kc, side = lane % GRID_W, lane // GRID_W
    ws = jnp.clip(qc - NA_COLS // 2, 0, GRID_W - NA_COLS)
    ok = (kc[None, :] >= ws[:, None]) & (kc[None, :] < ws[:, None] + NA_COLS)
    dc = kc[None, :] - qc[:, None] + (NA_COLS - 1)
    src = jnp.where(ok, side[None, :] * ncol + dc, -1)
    pick = (src[:, :, None] == jnp.arange(2 * ncol)[None, None, :]).astype(F32)
    both = jnp.concatenate([rpb[:, :, :-1], rpb[:, :, 1:]], axis=-1).astype(F32)
    table = jnp.einsum('lhdm,qkm->lhdqk', both, pick, precision=lax.Precision.HIGHEST)
    return jnp.where(ok[None, None, None], table, -jnp.inf)


def _na_kernel(q_ref, k_ref, v_ref, bias_ref, hm_ref, o_ref, *, rows):
    blk = pl.program_id(1)
    kh = NA_ROWS
    half = kh // 2
    lane_lo = lax.broadcasted_iota(jnp.int32, (GRID_W, 128), 1) < NA_HEAD_DIM

    def window(i):
        r = blk * NA_ROW_BLOCK + i
        rs = jnp.clip(r - half, 0, rows - kh)
        dr0 = rs - r + (NA_ROWS - 1)
        return pl.multiple_of(rs * GRID_W, GRID_W), dr0

    def bias(h, dr0):
        return jnp.concatenate([bias_ref[h, dr0 + 2 * j] for j in range(kh // 2)], axis=-1)

    def score_stage(i):
        koff, _ = window(i)
        ss = []
        for p in range(NA_HEADS // 2):
            lanes = slice(p * 128, (p + 1) * 128)
            qp = q_ref[i * GRID_W:(i + 1) * GRID_W, lanes]
            ql = jnp.concatenate([qp * hm_ref[0], qp * hm_ref[1]], axis=0)
            kw = k_ref[0, pl.ds(koff, kh * GRID_W), lanes]
            ss.append(lax.dot_general(ql, kw, (((1,), (1,)), ((), ())),
                                      preferred_element_type=F32))
        return ss

    def output_stage(i, ss):
        koff, dr0 = window(i)
        es, ls = [], []
        for p in range(NA_HEADS // 2):
            s = ss[p] + jnp.concatenate([bias(2 * p, dr0), bias(2 * p + 1, dr0)], axis=0)
            m = jnp.max(s, axis=-1, keepdims=True)
            e = jnp.exp(s - m)
            ls.append(jnp.sum(e, axis=-1, keepdims=True))
            es.append(e.astype(BF16))
        for p in range(NA_HEADS // 2):
            lanes = slice(p * 128, (p + 1) * 128)
            vw = v_ref[0, pl.ds(koff, kh * GRID_W), lanes]
            pv = _dot(es[p], vw) / ls[p]
            o = jnp.where(lane_lo, pv[:GRID_W], pv[GRID_W:])
            o_ref[i * GRID_W:(i + 1) * GRID_W, lanes] = o.astype(BF16)

    ss = score_stage(0)
    for i in range(NA_ROW_BLOCK):
        nxt = score_stage(i + 1) if i + 1 < NA_ROW_BLOCK else None
        output_stage(i, ss)
        ss = nxt


def _na(q, k, v, layer, bias, batch, seq):
    rows = seq // GRID_W
    assert rows >= NA_ROWS and NA_ROWS % 2 == 0
    blk_tokens = NA_ROW_BLOCK * GRID_W
    nblk = rows // NA_ROW_BLOCK
    head_mask = (jnp.arange(128)[None, None, :] // NA_HEAD_DIM
                 == jnp.arange(2)[:, None, None]).astype(BF16)
    head_mask = jnp.broadcast_to(head_mask, (2, GRID_W, 128))
    q_tile = pl.BlockSpec((blk_tokens, NA_WIDTH), lambda b, i: (b * nblk + i, 0))
    kv_spec = pl.BlockSpec((1, seq, NA_WIDTH), lambda b, i: (b, 0, 0))
    return pl.pallas_call(
        functools.partial(_na_kernel, rows=rows),
        grid=(batch, nblk),
        in_specs=[q_tile, kv_spec, kv_spec,
                  _layer_spec((NA_HEADS, 2 * NA_ROWS - 2, GRID_W, 2 * GRID_W), layer),
                  _const_spec((2, GRID_W, 128))],
        out_specs=q_tile,
        out_shape=jax.ShapeDtypeStruct((batch * seq, NA_WIDTH), BF16),
        compiler_params=_params(56),
        name="na",
    )(q, k.reshape(batch, seq, NA_WIDTH), v.reshape(batch, seq, NA_WIDTH), bias, head_mask)


def _dft_tables():
    two_pi = 2.0 * math.pi
    n = DFT_ROWS * DFT_COLS
    k1 = jnp.arange(DFT_ROWS)
    ang = ((k1[:, None] * k1[None, :]) % DFT_ROWS).astype(F32) * (two_pi / DFT_ROWS)
    f = jnp.concatenate([jnp.cos(ang), -jnp.sin(ang)], axis=0)
    f_wide = jnp.repeat(f, F1_T2_BLOCK, axis=1)
    same_t2 = (jnp.arange(DFT_ROWS * F1_T2_BLOCK)[None, None, :] % F1_T2_BLOCK
               == jnp.arange(F1_T2_BLOCK)[None, :, None])
    f1 = jnp.where(same_t2, f_wide[:, None, :], 0.0).reshape(
        2 * DFT_ROWS * F1_T2_BLOCK, DFT_ROWS * F1_T2_BLOCK)
    k2 = jnp.arange(DFT_COLS)
    t2 = jnp.arange(DFT_COLS)
    a = ((k2[:, None] * t2[None, :]) % DFT_COLS).astype(F32) * (two_pi / DFT_COLS)
    b = ((k1[:, None] * t2[None, :]) % n).astype(F32) * (two_pi / n)
    ca, sa = jnp.cos(a)[None], jnp.sin(a)[None]
    cb, sb = jnp.cos(b)[:, None, :], jnp.sin(b)[:, None, :]
    gr = ca * cb - sa * sb
    gi = -(sa * cb + ca * sb)
    g = jnp.concatenate([jnp.concatenate([gr, -gi], axis=2),
                         jnp.concatenate([gi, gr], axis=2)], axis=1)
    c = jnp.arange(F_GROUP_DIM)
    ang = ((c[:, None] * c[None, :]) % F_GROUP_DIM).astype(F32) * (two_pi / F_GROUP_DIM)
    eye = jnp.eye(MXU_COLS_V7X // F_GROUP_DIM, dtype=F32)
    cc = jnp.kron(eye, jnp.cos(ang))
    ss = jnp.kron(eye, jnp.sin(ang))
    return f1.astype(BF16), g.astype(BF16), cc.astype(BF16), ss.astype(BF16)


def _f1_kernel(f_ref, x_ref, o_ref):
    x = x_ref[0].reshape(DFT_ROWS * F1_T2_BLOCK, F_WIDTH).astype(BF16)
    z = _dot(f_ref[...], x)
    o_ref[0] = z.reshape(2, DFT_ROWS, F1_T2_BLOCK, F_WIDTH).astype(BF16)


def _f3_kernel(a_ref, g_ref, cc_ref, ss_ref, o_ref, zr_ref, zi_ref):
    for j in range(F3_K1_BLOCK):
        a = jnp.concatenate([a_ref[0, 0, j], a_ref[0, 1, j]], axis=0)
        z = _dot(g_ref[j], a)
        zr_ref[j * DFT_COLS:(j + 1) * DFT_COLS] = z[:DFT_COLS].astype(BF16)
        zi_ref[j * DFT_COLS:(j + 1) * DFT_COLS] = z[DFT_COLS:].astype(BF16)
    scale = 1.0 / math.sqrt(DFT_ROWS * DFT_COLS * F_GROUP_DIM)
    for p in range(F_WIDTH // MXU_COLS_V7X):
        cols = slice(p * MXU_COLS_V7X, (p + 1) * MXU_COLS_V7X)
        y = _dot(zr_ref[:, cols], cc_ref[...]) + _dot(zi_ref[:, cols], ss_ref[...])
        y = (y * scale).astype(BF16)
        for j in range(F3_K1_BLOCK):
            o_ref[0, j, :, cols] = y[j * DFT_COLS:(j + 1) * DFT_COLS]


def _fourier(u, tables, batch, seq):
    f1, g, cc, ss = tables
    f1_rows = 2 * DFT_ROWS * F1_T2_BLOCK
    a = pl.pallas_call(
        _f1_kernel,
        grid=(batch, DFT_COLS // F1_T2_BLOCK),
        in_specs=[_const_spec((f1_rows, f1_rows // 2)),
                  pl.BlockSpec((1, DFT_ROWS, F1_T2_BLOCK, F_WIDTH), lambda b, i: (b, 0, i, 0))],
        out_specs=pl.BlockSpec((1, 2, DFT_ROWS, F1_T2_BLOCK, F_WIDTH), lambda b, i: (b, 0, 0, i, 0)),
        out_shape=jax.ShapeDtypeStruct((batch, 2, DFT_ROWS, DFT_COLS, F_WIDTH), BF16),
        compiler_params=_params(40),
        name="dft_rows",
    )(f1, u.reshape(batch, DFT_ROWS, DFT_COLS, F_WIDTH))
    y = pl.pallas_call(
        _f3_kernel,
        grid=(batch, DFT_ROWS // F3_K1_BLOCK),
        in_specs=[pl.BlockSpec((1, 2, F3_K1_BLOCK, DFT_COLS, F_WIDTH), lambda b, i: (b, 0, i, 0, 0)),
                  pl.BlockSpec((F3_K1_BLOCK, 2 * DFT_COLS, 2 * DFT_COLS), lambda b, i: (i, 0, 0)),
                  _const_spec((MXU_COLS_V7X, MXU_COLS_V7X)), _const_spec((MXU_COLS_V7X, MXU_COLS_V7X))],
        out_specs=pl.BlockSpec((1, F3_K1_BLOCK, DFT_COLS, F_WIDTH), lambda b, i: (b, i, 0, 0)),
        out_shape=jax.ShapeDtypeStruct((batch, DFT_ROWS, DFT_COLS, F_WIDTH), BF16),
        scratch_shapes=[pltpu.VMEM((F3_K1_BLOCK * DFT_COLS, F_WIDTH), BF16),
                        pltpu.VMEM((F3_K1_BLOCK * DFT_COLS, F_WIDTH), BF16)],
        compiler_params=_params(32),
        name="dft_cols_channels",
    )(a, g, cc, ss)
    return jnp.transpose(y, (0, 2, 1, 3)).reshape(batch * seq, F_WIDTH)


def _mixout_kernel(x_ref, g_ref, att_ref, fou_ref, wg_ref, gb_ref, wna_ref, wf_ref, wo_ref,
                   *rest, n_cast):
    cast_src, o_ref, cast_dst = rest[:n_cast], rest[n_cast], rest[n_cast + 1:]
    _cast_blocks(cast_src, cast_dst)
    for t in range(TOKEN_TILE // SUB_TILE):
        rows = slice(t * SUB_TILE, (t + 1) * SUB_TILE)
        x = x_ref[rows, :]
        h = _rms(x, g_ref[...]).astype(BF16)
        gates = jax.nn.sigmoid(_dot(h, wg_ref[...]) + gb_ref[...])
        y_na = _dot(att_ref[rows, :], wna_ref[...])
        y_f = _dot(fou_ref[rows, :], wf_ref[...])
        m = gates[:, :D_MODEL] * y_na + gates[:, D_MODEL:] * y_f
        o_ref[rows, :] = x + _dot(m.astype(BF16), wo_ref[...])


def _mixout(x, layer, g, att, fou, w_in, gate_bias, w_na, w_f, w_o, cast=()):
    n = x.shape[0]
    steps = n // TOKEN_TILE
    tile = pl.BlockSpec((TOKEN_TILE, D_MODEL), lambda i: (i, 0))
    half = pl.BlockSpec((TOKEN_TILE, NA_WIDTH), lambda i: (i, 0))
    assert w_in.shape[1] == 4 * D_MODEL
    cast_in, cast_out, cast_shapes = _cast_specs(cast, steps)
    y, *copies = pl.pallas_call(
        functools.partial(_mixout_kernel, n_cast=len(cast)),
        grid=(steps,),
        in_specs=[tile, _layer_spec((1, D_MODEL), layer), half, half,
                  _const_spec((D_MODEL, 2 * D_MODEL), 1),
                  _layer_spec((1, 2 * D_MODEL), layer),
                  _const_spec((NA_WIDTH, D_MODEL)), _const_spec((F_WIDTH, D_MODEL)),
                  _const_spec((D_MODEL, D_MODEL))] + cast_in,
        out_specs=[tile] + cast_out,
        out_shape=[jax.ShapeDtypeStruct((n, D_MODEL), F32)] + cast_shapes,
        compiler_params=_params(48),
        name="mix_out",
    )(x, g, att, fou, w_in, gate_bias, w_na, w_f, w_o, *[arr for arr, _ in cast])
    return y, copies


def kernel(x, ffn1_norm, ffn1_w_in, ffn1_w_out, mix_norm, mix_w_in, mix_gate_bias, na_rpb,
           na_w_out, f_w_out, mix_w_o, ffn2_norm, ffn2_w_in, ffn2_w_out, final_norm):
    batch, seq, d = x.shape
    depth = ffn1_norm.shape[0]
    assert d == D_MODEL and seq == DFT_ROWS * DFT_COLS and seq % (GRID_W * NA_ROW_BLOCK) == 0
    assert (batch * seq) % TOKEN_TILE == 0 and (batch * seq) % FFN_TOKEN_TILE == 0
    tables = _dft_tables()
    na_bias = _na_bias_tables(na_rpb)
    gain = lambda g: g.reshape(depth, 1, D_MODEL)
    g1, g2, mix_g = gain(ffn1_norm), gain(ffn2_norm), gain(mix_norm)
    gate_bias = mix_gate_bias.reshape(depth, 1, 2 * D_MODEL)
    xs = x.reshape(batch * seq, d)
    ffn1_w = [ffn1_w_in[0].astype(BF16), ffn1_w_out[0].astype(BF16)]
    for l in range(depth):
        xs, (w_in, w_na, w_f, w_o) = _ffn(
            xs, l, g1, *ffn1_w,
            cast=[(mix_w_in, l), (na_w_out, l), (f_w_out, l), (mix_w_o, l)])
        q, k, v, u = _proj(xs, l, mix_g, w_in)
        att = _na(q, k, v, l, na_bias, batch, seq)
        fou = _fourier(u, tables, batch, seq)
        xs, ffn2_w = _mixout(xs, l, mix_g, att, fou, w_in, gate_bias, w_na, w_f, w_o,
                             cast=[(ffn2_w_in, l), (ffn2_w_out, l)])
        last = l == depth - 1
        xs, ffn1_w = _ffn(xs, l, g2, *ffn2_w, final_g=final_norm if last else None,
                          cast=[] if last else [(ffn1_w_in, l + 1), (ffn1_w_out, l + 1)])
    return xs.reshape(batch, seq, d)
```

```python
import functools
import math

import jax
import jax.numpy as jnp
from jax import lax
from jax.experimental import pallas as pl
from jax.experimental.pallas import tpu as pltpu

D_MODEL = 1024
GRID_W = 64
NA_HEADS = 8
NA_HEAD_DIM = 64
NA_WIDTH = NA_HEADS * NA_HEAD_DIM
NA_ROWS = 8
NA_COLS = 16
F_GROUPS = 4
F_GROUP_DIM = 128
F_WIDTH = F_GROUPS * F_GROUP_DIM
D_FF = 2816
RMS_EPS = 1e-6

BF16 = jnp.bfloat16
F32 = jnp.float32

MXU_COLS_V7X = 256
VMEM_BYTES_V7X = 64 * 1024 * 1024

TOKEN_TILE = 1024
SUB_TILE = 512
FFN_TOKEN_TILE = 1024
FF_CHUNK = MXU_COLS_V7X
NA_ROW_BLOCK = 16
DFT_ROWS = 64
DFT_COLS = 128
SUBLANES_F32 = 8
BF16_SUBLANES = 16
F1_T2_BLOCK = SUBLANES_F32
F1_T2_STEP = 32
F3_K1_BLOCK = 16


def _params(vmem_mib):
    return pltpu.CompilerParams(
        dimension_semantics=None,
        vmem_limit_bytes=vmem_mib * 1024 * 1024,
    )


def _const_spec(shape, col_block=0):
    index = (0,) * (len(shape) - 1) + (col_block,)
    return pl.BlockSpec(shape, lambda *_: index, pipeline_mode=pl.Buffered(1))


def _cast_specs(params, steps):
    in_specs, out_specs, out_shapes = [], [], []
    for arr, layer in params:
        _, rows, cols = arr.shape
        nblk = steps
        while rows % nblk or (rows // nblk) % BF16_SUBLANES:
            nblk //= 2
        blk = rows // nblk
        in_specs.append(pl.BlockSpec(
            (None, blk, cols), lambda i, layer=layer, nblk=nblk: (layer, jnp.minimum(i, nblk - 1), 0)))
        out_specs.append(pl.BlockSpec(
            (blk, cols), lambda i, nblk=nblk: (jnp.minimum(i, nblk - 1), 0)))
        out_shapes.append(jax.ShapeDtypeStruct((rows, cols), BF16))
    return in_specs, out_specs, out_shapes


def _cast_blocks(src_refs, dst_refs):
    for src, dst in zip(src_refs, dst_refs):
        dst[...] = src[...].astype(BF16)


def _layer_spec(shape, layer):
    index = (layer,) + (0,) * len(shape)
    return pl.BlockSpec((None,) + tuple(shape), lambda *_: index, pipeline_mode=pl.Buffered(1))


def _rms(x, g):
    ms = jnp.mean(x * x, axis=-1, keepdims=True)
    return x * lax.rsqrt(ms + RMS_EPS) * g


def _dot(a, b):
    return jnp.dot(a, b, preferred_element_type=F32)


def _ffn_kernel(x_ref, g_ref, win_ref, wout_ref, *rest, final, n_cast):
    rest = list(rest)
    gf_ref = rest.pop(0) if final else None
    cast_src, o_ref, cast_dst, act_ref = (rest[:n_cast], rest[n_cast],
                                          rest[n_cast + 1:2 * n_cast + 1], rest[-1])
    _cast_blocks(cast_src, cast_dst)
    x = x_ref[...]
    h = _rms(x, g_ref[...]).astype(BF16)
    for c in range(D_FF // FF_CHUNK):
        lo = c * FF_CHUNK
        g = _dot(h, win_ref[:, lo:lo + FF_CHUNK])
        u = _dot(h, win_ref[:, D_FF + lo:D_FF + lo + FF_CHUNK])
        act_ref[:, lo:lo + FF_CHUNK] = (g * jax.nn.sigmoid(g) * u).astype(BF16)
    y = x + 0.5 * _dot(act_ref[...], wout_ref[...])
    if final:
        y = _rms(y, gf_ref[...])
    o_ref[...] = y


def _ffn(x, layer, g, w_in, w_out, final_g=None, cast=()):
    n = x.shape[0]
    steps = n // FFN_TOKEN_TILE
    final = final_g is not None
    tile = pl.BlockSpec((FFN_TOKEN_TILE, D_MODEL), lambda i: (i, 0))
    in_specs = [tile, _layer_spec((1, D_MODEL), layer), _const_spec((D_MODEL, 2 * D_FF)),
                _const_spec((D_FF, D_MODEL))]
    args = [x, g, w_in, w_out]
    if final:
        in_specs.append(_const_spec((1, D_MODEL)))
        args.append(final_g.reshape(1, D_MODEL))
    cast_in, cast_out, cast_shapes = _cast_specs(cast, steps)
    y, *copies = pl.pallas_call(
        functools.partial(_ffn_kernel, final=final, n_cast=len(cast)),
        grid=(steps,),
        in_specs=in_specs + cast_in,
        out_specs=[tile] + cast_out,
        out_shape=[jax.ShapeDtypeStruct((n, D_MODEL), F32)] + cast_shapes,
        scratch_shapes=[pltpu.VMEM((FFN_TOKEN_TILE, D_FF), BF16)],
        compiler_params=_params(58),
        name="ffn",
    )(*args, *[arr for arr, _ in cast])
    return y, copies


def _proj_kernel(x_ref, g_ref, w_ref, q_ref, k_ref, v_ref, u_ref):
    s = NA_WIDTH
    for t in range(TOKEN_TILE // SUB_TILE):
        rows = slice(t * SUB_TILE, (t + 1) * SUB_TILE)
        h = _rms(x_ref[rows, :], g_ref[...]).astype(BF16)
        z = _dot(h, w_ref[...])
        q_ref[rows, :] = (z[:, :s] * (NA_HEAD_DIM ** -0.5)).astype(BF16)
        k_ref[rows, :] = z[:, s:2 * s].astype(BF16)
        v_ref[rows, :] = z[:, 2 * s:3 * s].astype(BF16)
        u_ref[rows, :] = z[:, 3 * s:3 * s + F_WIDTH]


def _proj(x, layer, g, w):
    n = x.shape[0]
    cols = 3 * NA_WIDTH + F_WIDTH
    assert w.shape[1] == 2 * cols
    out_tile = pl.BlockSpec((TOKEN_TILE, NA_WIDTH), lambda i: (i, 0))
    out_sds = jax.ShapeDtypeStruct((n, NA_WIDTH), BF16)
    return pl.pallas_call(
        _proj_kernel,
        grid=(n // TOKEN_TILE,),
        in_specs=[pl.BlockSpec((TOKEN_TILE, D_MODEL), lambda i: (i, 0)),
                  _layer_spec((1, D_MODEL), layer), _const_spec((D_MODEL, cols), 0)],
        out_specs=[out_tile] * 4,
        out_shape=[out_sds] * 3 + [jax.ShapeDtypeStruct((n, F_WIDTH), F32)],
        compiler_params=_params(40),
        name="mix_proj",
    )(x, g, w)


def _na_bias_tables(rpb):
    ncol = 2 * NA_COLS - 1
    qc = jnp.arange(GRID_W)
    lane = jnp.arange(2 * GRID_W)
    kc, side = lane % GRID_W, lane // GRID_W
    ws = jnp.clip(qc - NA_COLS // 2, 0, GRID_W - NA_COLS)
    ok = (kc[None, :] >= ws[:, None]) & (kc[None, :] < ws[:, None] + NA_COLS)
    dc = kc[None, :] - qc[:, None] + (NA_COLS - 1)
    src = jnp.where(ok, side[None, :] * ncol + dc, -1)
    pick = (src[:, :, None] == jnp.arange(2 * ncol)[None, None, :]).astype(F32)
    both = jnp.concatenate([rpb[:, :, :-1], rpb[:, :, 1:]], axis=-1).astype(F32)
    table = jnp.einsum('lhdm,qkm->lhdqk', both, pick, precision=lax.Precision.HIGHEST)
    return jnp.where(ok[None, None, None], table, -jnp.inf)


def _na_kernel(q_ref, k_ref, v_ref, bias_ref, hm_ref, o_ref, *, rows):
    blk = pl.program_id(1)
    kh = NA_ROWS
    half = kh // 2
    lane_lo = lax.broadcasted_iota(jnp.int32, (GRID_W, 128), 1) < NA_HEAD_DIM

    def window(i):
        r = blk * NA_ROW_BLOCK + i
        rs = jnp.clip(r - half, 0, rows - kh)
        dr0 = rs - r + (NA_ROWS - 1)
        return pl.multiple_of(rs * GRID_W, GRID_W), dr0

    def bias(h, dr0):
        return jnp.concatenate([bias_ref[h, dr0 + 2 * j] for j in range(kh // 2)], axis=-1)

    def score_stage(i):
        koff, _ = window(i)
        ss = []
        for p in range(NA_HEADS // 2):
            lanes = slice(p * 128, (p + 1) * 128)
            qp = q_ref[i * GRID_W:(i + 1) * GRID_W, lanes]
            ql = jnp.concatenate([qp * hm_ref[0], qp * hm_ref[1]], axis=0)
            kw = k_ref[0, pl.ds(koff, kh * GRID_W), lanes]
            ss.append(lax.dot_general(ql, kw, (((1,), (1,)), ((), ())),
                                      preferred_element_type=F32))
        return ss

    def output_stage(i, ss):
        koff, dr0 = window(i)
        es, ls = [], []
        for p in range(NA_HEADS // 2):
            s = ss[p] + jnp.concatenate([bias(2 * p, dr0), bias(2 * p + 1, dr0)], axis=0)
            m = jnp.max(s, axis=-1, keepdims=True)
            e = jnp.exp(s - m)
            ls.append(jnp.sum(e, axis=-1, keepdims=True))
            es.append(e.astype(BF16))
        for p in range(NA_HEADS // 2):
            lanes = slice(p * 128, (p + 1) * 128)
            vw = v_ref[0, pl.ds(koff, kh * GRID_W), lanes]
            pv = _dot(es[p], vw) / ls[p]
            o = jnp.where(lane_lo, pv[:GRID_W], pv[GRID_W:])
            o_ref[i * GRID_W:(i + 1) * GRID_W, lanes] = o.astype(BF16)

    ss = score_stage(0)
    for i in range(NA_ROW_BLOCK):
        nxt = score_stage(i + 1) if i + 1 < NA_ROW_BLOCK else None
        output_stage(i, ss)
        ss = nxt


def _na(q, k, v, layer, bias, batch, seq):
    rows = seq // GRID_W
    assert rows >= NA_ROWS and NA_ROWS % 2 == 0
    blk_tokens = NA_ROW_BLOCK * GRID_W
    nblk = rows // NA_ROW_BLOCK
    head_mask = (jnp.arange(128)[None, None, :] // NA_HEAD_DIM
                 == jnp.arange(2)[:, None, None]).astype(BF16)
    head_mask = jnp.broadcast_to(head_mask, (2, GRID_W, 128))
    q_tile = pl.BlockSpec((blk_tokens, NA_WIDTH), lambda b, i: (b * nblk + i, 0))
    kv_spec = pl.BlockSpec((1, seq, NA_WIDTH), lambda b, i: (b, 0, 0))
    return pl.pallas_call(
        functools.partial(_na_kernel, rows=rows),
        grid=(batch, nblk),
        in_specs=[q_tile, kv_spec, kv_spec,
                  _layer_spec((NA_HEADS, 2 * NA_ROWS - 2, GRID_W, 2 * GRID_W), layer),
                  _const_spec((2, GRID_W, 128))],
        out_specs=q_tile,
        out_shape=jax.ShapeDtypeStruct((batch * seq, NA_WIDTH), BF16),
        compiler_params=_params(56),
        name="na",
    )(q, k.reshape(batch, seq, NA_WIDTH), v.reshape(batch, seq, NA_WIDTH), bias, head_mask)


def _dft_tables():
    two_pi = 2.0 * math.pi
    n = DFT_ROWS * DFT_COLS
    k1 = jnp.arange(DFT_ROWS)
    ang = ((k1[:, None] * k1[None, :]) % DFT_ROWS).astype(F32) * (two_pi / DFT_ROWS)
    f = jnp.concatenate([jnp.cos(ang), -jnp.sin(ang)], axis=0)
    f_wide = jnp.repeat(f, F1_T2_BLOCK, axis=1)
    same_t2 = (jnp.arange(DFT_ROWS * F1_T2_BLOCK)[None, None, :] % F1_T2_BLOCK
               == jnp.arange(F1_T2_BLOCK)[None, :, None])
    f1 = jnp.where(same_t2, f_wide[:, None, :], 0.0).reshape(
        2 * DFT_ROWS * F1_T2_BLOCK, DFT_ROWS * F1_T2_BLOCK)
    k2 = jnp.arange(DFT_COLS)
    t2 = jnp.arange(DFT_COLS)
    a = ((k2[:, None] * t2[None, :]) % DFT_COLS).astype(F32) * (two_pi / DFT_COLS)
    b = ((k1[:, None] * t2[None, :]) % n).astype(F32) * (two_pi / n)
    ca, sa = jnp.cos(a)[None], jnp.sin(a)[None]
    cb, sb = jnp.cos(b)[:, None, :], jnp.sin(b)[:, None, :]
    gr = ca * cb - sa * sb
    gi = -(sa * cb + ca * sb)
    g = jnp.concatenate([jnp.concatenate([gr, -gi], axis=2),
                         jnp.concatenate([gi, gr], axis=2)], axis=1)
    c = jnp.arange(F_GROUP_DIM)
    ang = ((c[:, None] * c[None, :]) % F_GROUP_DIM).astype(F32) * (two_pi / F_GROUP_DIM)
    eye = jnp.eye(MXU_COLS_V7X // F_GROUP_DIM, dtype=F32)
    cc = jnp.kron(eye, jnp.cos(ang))
    ss = jnp.kron(eye, jnp.sin(ang))
    return f1.astype(BF16), g.astype(BF16), cc.astype(BF16), ss.astype(BF16)


def _f1_kernel(f_ref, x_ref, o_ref):
    per_store = BF16_SUBLANES // F1_T2_BLOCK
    for s in range(F1_T2_STEP // BF16_SUBLANES):
        zs = []
        for j in range(per_store):
            lo = (s * per_store + j) * F1_T2_BLOCK
            x = x_ref[0, :, lo:lo + F1_T2_BLOCK, :]
            x = x.reshape(DFT_ROWS * F1_T2_BLOCK, F_WIDTH).astype(BF16)
            z = _dot(f_ref[...], x)
            zs.append(z.reshape(2, DFT_ROWS, F1_T2_BLOCK, F_WIDTH))
        cols = slice(s * BF16_SUBLANES, (s + 1) * BF16_SUBLANES)
        o_ref[0, :, :, cols, :] = jnp.concatenate(zs, axis=2).astype(BF16)


def _f3_kernel(a_ref, g_ref, cc_ref, ss_ref, o_ref, zr_ref, zi_ref):
    for j in range(F3_K1_BLOCK):
        a = jnp.concatenate([a_ref[0, 0, j], a_ref[0, 1, j]], axis=0)
        z = _dot(g_ref[j], a)
        zr_ref[j * DFT_COLS:(j + 1) * DFT_COLS] = z[:DFT_COLS].astype(BF16)
        zi_ref[j * DFT_COLS:(j + 1) * DFT_COLS] = z[DFT_COLS:].astype(BF16)
    scale = 1.0 / math.sqrt(DFT_ROWS * DFT_COLS * F_GROUP_DIM)
    for p in range(F_WIDTH // MXU_COLS_V7X):
        cols = slice(p * MXU_COLS_V7X, (p + 1) * MXU_COLS_V7X)
        y = _dot(zr_ref[:, cols], cc_ref[...]) + _dot(zi_ref[:, cols], ss_ref[...])
        y = (y * scale).astype(BF16)
        for j in range(F3_K1_BLOCK):
            o_ref[0, j, :, cols] = y[j * DFT_COLS:(j + 1) * DFT_COLS]


def _fourier(u, tables, batch, seq):
    f1, g, cc, ss = tables
    f1_rows = 2 * DFT_ROWS * F1_T2_BLOCK
    a = pl.pallas_call(
        _f1_kernel,
        grid=(batch, DFT_COLS // F1_T2_STEP),
        in_specs=[_const_spec((f1_rows, f1_rows // 2)),
                  pl.BlockSpec((1, DFT_ROWS, F1_T2_STEP, F_WIDTH), lambda b, i: (b, 0, i, 0))],
        out_specs=pl.BlockSpec((1, 2, DFT_ROWS, F1_T2_STEP, F_WIDTH), lambda b, i: (b, 0, 0, i, 0)),
        out_shape=jax.ShapeDtypeStruct((batch, 2, DFT_ROWS, DFT_COLS, F_WIDTH), BF16),
        compiler_params=_params(40),
        name="dft_rows",
    )(f1, u.reshape(batch, DFT_ROWS, DFT_COLS, F_WIDTH))
    y = pl.pallas_call(
        _f3_kernel,
        grid=(batch, DFT_ROWS // F3_K1_BLOCK),
        in_specs=[pl.BlockSpec((1, 2, F3_K1_BLOCK, DFT_COLS, F_WIDTH), lambda b, i: (b, 0, i, 0, 0)),
                  pl.BlockSpec((F3_K1_BLOCK, 2 * DFT_COLS, 2 * DFT_COLS), lambda b, i: (i, 0, 0)),
                  _const_spec((MXU_COLS_V7X, MXU_COLS_V7X)), _const_spec((MXU_COLS_V7X, MXU_COLS_V7X))],
        out_specs=pl.BlockSpec((1, F3_K1_BLOCK, DFT_COLS, F_WIDTH), lambda b, i: (b, i, 0, 0)),
        out_shape=jax.ShapeDtypeStruct((batch, DFT_ROWS, DFT_COLS, F_WIDTH), BF16),
        scratch_shapes=[pltpu.VMEM((F3_K1_BLOCK * DFT_COLS, F_WIDTH), BF16),
                        pltpu.VMEM((F3_K1_BLOCK * DFT_COLS, F_WIDTH), BF16)],
        compiler_params=_params(32),
        name="dft_cols_channels",
    )(a, g, cc, ss)
    return jnp.transpose(y, (0, 2, 1, 3)).reshape(batch * seq, F_WIDTH)


def _mixout_kernel(x_ref, g_ref, att_ref, fou_ref, wg_ref, gb_ref, wna_ref, wf_ref, wo_ref,
                   *rest, n_cast):
    cast_src, o_ref, cast_dst = rest[:n_cast], rest[n_cast], rest[n_cast + 1:]
    _cast_blocks(cast_src, cast_dst)
    for t in range(TOKEN_TILE // SUB_TILE):
        rows = slice(t * SUB_TILE, (t + 1) * SUB_TILE)
        x = x_ref[rows, :]
        h = _rms(x, g_ref[...]).astype(BF16)
        gates = jax.nn.sigmoid(_dot(h, wg_ref[...]) + gb_ref[...])
        y_na = _dot(att_ref[rows, :], wna_ref[...])
        y_f = _dot(fou_ref[rows, :], wf_ref[...])
        m = gates[:, :D_MODEL] * y_na + gates[:, D_MODEL:] * y_f
        o_ref[rows, :] = x + _dot(m.astype(BF16), wo_ref[...])


def _mixout(x, layer, g, att, fou, w_in, gate_bias, w_na, w_f, w_o, cast=()):
    n = x.shape[0]
    steps = n // TOKEN_TILE
    tile = pl.BlockSpec((TOKEN_TILE, D_MODEL), lambda i: (i, 0))
    half = pl.BlockSpec((TOKEN_TILE, NA_WIDTH), lambda i: (i, 0))
    assert w_in.shape[1] == 4 * D_MODEL
    cast_in, cast_out, cast_shapes = _cast_specs(cast, steps)
    y, *copies = pl.pallas_call(
        functools.partial(_mixout_kernel, n_cast=len(cast)),
        grid=(steps,),
        in_specs=[tile, _layer_spec((1, D_MODEL), layer), half, half,
                  _const_spec((D_MODEL, 2 * D_MODEL), 1),
                  _layer_spec((1, 2 * D_MODEL), layer),
                  _const_spec((NA_WIDTH, D_MODEL)), _const_spec((F_WIDTH, D_MODEL)),
                  _const_spec((D_MODEL, D_MODEL))] + cast_in,
        out_specs=[tile] + cast_out,
        out_shape=[jax.ShapeDtypeStruct((n, D_MODEL), F32)] + cast_shapes,
        compiler_params=_params(48),
        name="mix_out",
    )(x, g, att, fou, w_in, gate_bias, w_na, w_f, w_o, *[arr for arr, _ in cast])
    return y, copies


def kernel(x, ffn1_norm, ffn1_w_in, ffn1_w_out, mix_norm, mix_w_in, mix_gate_bias, na_rpb,
           na_w_out, f_w_out, mix_w_o, ffn2_norm, ffn2_w_in, ffn2_w_out, final_norm):
    batch, seq, d = x.shape
    depth = ffn1_norm.shape[0]
    assert d == D_MODEL and seq == DFT_ROWS * DFT_COLS and seq % (GRID_W * NA_ROW_BLOCK) == 0
    assert (batch * seq) % TOKEN_TILE == 0 and (batch * seq) % FFN_TOKEN_TILE == 0
    tables = _dft_tables()
    na_bias = _na_bias_tables(na_rpb)
    gain = lambda g: g.reshape(depth, 1, D_MODEL)
    g1, g2, mix_g = gain(ffn1_norm), gain(ffn2_norm), gain(mix_norm)
    gate_bias = mix_gate_bias.reshape(depth, 1, 2 * D_MODEL)
    xs = x.reshape(batch * seq, d)
    ffn1_w = [ffn1_w_in[0].astype(BF16), ffn1_w_out[0].astype(BF16)]
    for l in range(depth):
        xs, (w_in, w_na, w_f, w_o) = _ffn(
            xs, l, g1, *ffn1_w,
            cast=[(mix_w_in, l), (na_w_out, l), (f_w_out, l), (mix_w_o, l)])
        q, k, v, u = _proj(xs, l, mix_g, w_in)
        att = _na(q, k, v, l, na_bias, batch, seq)
        fou = _fourier(u, tables, batch, seq)
        xs, ffn2_w = _mixout(xs, l, mix_g, att, fou, w_in, gate_bias, w_na, w_f, w_o,
                             cast=[(ffn2_w_in, l), (ffn2_w_out, l)])
        last = l == depth - 1
        xs, ffn1_w = _ffn(xs, l, g2, *ffn2_w, final_g=final_norm if last else None,
                          cast=[] if last else [(ffn1_w_in, l + 1), (ffn1_w_out, l + 1)])
    return xs.reshape(batch, seq, d)
```

```python
import functools
import math

import jax
import jax.numpy as jnp
from jax import lax
from jax.experimental import pallas as pl
from jax.experimental.pallas import tpu as pltpu

D_MODEL = 1024
GRID_W = 64
NA_HEADS = 8
NA_HEAD_DIM = 64
NA_WIDTH = NA_HEADS * NA_HEAD_DIM
NA_ROWS = 8
NA_COLS = 16
F_GROUPS = 4
F_GROUP_DIM = 128
F_WIDTH = F_GROUPS * F_GROUP_DIM
D_FF = 2816
RMS_EPS = 1e-6

BF16 = jnp.bfloat16
F32 = jnp.float32

MXU_COLS_V7X = 256
VMEM_BYTES_V7X = 64 * 1024 * 1024

TOKEN_TILE = 1024
SUB_TILE = 512
FFN_TOKEN_TILE = 1024
FF_CHUNK = MXU_COLS_V7X
NA_ROW_BLOCK = 16
DFT_ROWS = 64
DFT_COLS = 128
SUBLANES_F32 = 8
BF16_SUBLANES = 16
F1_T2_BLOCK = SUBLANES_F32
F1_T2_STEP = 32
F3_K1_BLOCK = 16


def _params(vmem_mib):
    return pltpu.CompilerParams(
        dimension_semantics=None,
        vmem_limit_bytes=vmem_mib * 1024 * 1024,
    )


def _const_spec(shape, col_block=0):
    index = (0,) * (len(shape) - 1) + (col_block,)
    return pl.BlockSpec(shape, lambda *_: index, pipeline_mode=pl.Buffered(1))


def _cast_specs(params, steps):
    in_specs, out_specs, out_shapes = [], [], []
    for arr, layer in params:
        _, rows, cols = arr.shape
        nblk = steps
        while rows % nblk or (rows // nblk) % BF16_SUBLANES:
            nblk //= 2
        blk = rows // nblk
        in_specs.append(pl.BlockSpec(
            (None, blk, cols), lambda i, layer=layer, nblk=nblk: (layer, jnp.minimum(i, nblk - 1), 0)))
        out_specs.append(pl.BlockSpec(
            (blk, cols), lambda i, nblk=nblk: (jnp.minimum(i, nblk - 1), 0)))
        out_shapes.append(jax.ShapeDtypeStruct((rows, cols), BF16))
    return in_specs, out_specs, out_shapes


def _cast_blocks(src_refs, dst_refs):
    for src, dst in zip(src_refs, dst_refs):
        dst[...] = src[...].astype(BF16)


def _layer_spec(shape, layer):
    index = (layer,) + (0,) * len(shape)
    return pl.BlockSpec((None,) + tuple(shape), lambda *_: index, pipeline_mode=pl.Buffered(1))


def _rms(x, g):
    ms = jnp.mean(x * x, axis=-1, keepdims=True)
    return x * lax.rsqrt(ms + RMS_EPS) * g


def _dot(a, b):
    return jnp.dot(a, b, preferred_element_type=F32)


def _ffn_kernel(x_ref, g_ref, win_ref, wout_ref, *rest, final, n_cast):
    rest = list(rest)
    gf_ref = rest.pop(0) if final else None
    cast_src, o_ref, cast_dst, act_ref = (rest[:n_cast], rest[n_cast],
                                          rest[n_cast + 1:2 * n_cast + 1], rest[-1])
    _cast_blocks(cast_src, cast_dst)
    x = x_ref[...]
    h = _rms(x, g_ref[...]).astype(BF16)
    for c in range(D_FF // FF_CHUNK):
        lo = c * FF_CHUNK
        g = _dot(h, win_ref[:, lo:lo + FF_CHUNK])
        u = _dot(h, win_ref[:, D_FF + lo:D_FF + lo + FF_CHUNK])
        act_ref[:, lo:lo + FF_CHUNK] = (g * jax.nn.sigmoid(g) * u).astype(BF16)
    y = x + 0.5 * _dot(act_ref[...], wout_ref[...])
    if final:
        y = _rms(y, gf_ref[...])
    o_ref[...] = y


def _ffn(x, layer, g, w_in, w_out, final_g=None, cast=()):
    n = x.shape[0]
    steps = n // FFN_TOKEN_TILE
    final = final_g is not None
    tile = pl.BlockSpec((FFN_TOKEN_TILE, D_MODEL), lambda i: (i, 0))
    in_specs = [tile, _layer_spec((1, D_MODEL), layer), _const_spec((D_MODEL, 2 * D_FF)),
                _const_spec((D_FF, D_MODEL))]
    args = [x, g, w_in, w_out]
    if final:
        in_specs.append(_const_spec((1, D_MODEL)))
        args.append(final_g.reshape(1, D_MODEL))
    cast_in, cast_out, cast_shapes = _cast_specs(cast, steps)
    y, *copies = pl.pallas_call(
        functools.partial(_ffn_kernel, final=final, n_cast=len(cast)),
        grid=(steps,),
        in_specs=in_specs + cast_in,
        out_specs=[tile] + cast_out,
        out_shape=[jax.ShapeDtypeStruct((n, D_MODEL), F32)] + cast_shapes,
        scratch_shapes=[pltpu.VMEM((FFN_TOKEN_TILE, D_FF), BF16)],
        compiler_params=_params(46),
        name="ffn",
    )(*args, *[arr for arr, _ in cast])
    return y, copies


def _proj_kernel(x_ref, g_ref, w_ref, q_ref, k_ref, v_ref, u_ref):
    s = NA_WIDTH
    for t in range(TOKEN_TILE // SUB_TILE):
        rows = slice(t * SUB_TILE, (t + 1) * SUB_TILE)
        h = _rms(x_ref[rows, :], g_ref[...]).astype(BF16)
        z = _dot(h, w_ref[...])
        q_ref[rows, :] = (z[:, :s] * (NA_HEAD_DIM ** -0.5)).astype(BF16)
        k_ref[rows, :] = z[:, s:2 * s].astype(BF16)
        v_ref[rows, :] = z[:, 2 * s:3 * s].astype(BF16)
        u_ref[rows, :] = z[:, 3 * s:3 * s + F_WIDTH]


def _proj(x, layer, g, w):
    n = x.shape[0]
    cols = 3 * NA_WIDTH + F_WIDTH
    assert w.shape[1] == 2 * cols
    out_tile = pl.BlockSpec((TOKEN_TILE, NA_WIDTH), lambda i: (i, 0))
    out_sds = jax.ShapeDtypeStruct((n, NA_WIDTH), BF16)
    return pl.pallas_call(
        _proj_kernel,
        grid=(n // TOKEN_TILE,),
        in_specs=[pl.BlockSpec((TOKEN_TILE, D_MODEL), lambda i: (i, 0)),
                  _layer_spec((1, D_MODEL), layer), _const_spec((D_MODEL, cols), 0)],
        out_specs=[out_tile] * 4,
        out_shape=[out_sds] * 3 + [jax.ShapeDtypeStruct((n, F_WIDTH), F32)],
        compiler_params=_params(25),
        name="mix_proj",
    )(x, g, w)


def _na_bias_tables(rpb):
    ncol = 2 * NA_COLS - 1
    qc = jnp.arange(GRID_W)
    lane = jnp.arange(2 * GRID_W)
    kc, side = lane % GRID_W, lane // GRID_W
    ws = jnp.clip(qc - NA_COLS // 2, 0, GRID_W - NA_COLS)
    ok = (kc[None, :] >= ws[:, None]) & (kc[None, :] < ws[:, None] + NA_COLS)
    dc = kc[None, :] - qc[:, None] + (NA_COLS - 1)
    src = jnp.where(ok, side[None, :] * ncol + dc, -1)
    pick = (src[:, :, None] == jnp.arange(2 * ncol)[None, None, :]).astype(F32)
    both = jnp.concatenate([rpb[:, :, :-1], rpb[:, :, 1:]], axis=-1).astype(F32)
    table = jnp.einsum('lhdm,qkm->lhdqk', both, pick, precision=lax.Precision.HIGHEST)
    return jnp.where(ok[None, None, None], table, -jnp.inf)


def _na_kernel(q_ref, k_ref, v_ref, bias_ref, hm_ref, o_ref, *, rows):
    blk = pl.program_id(1)
    kh = NA_ROWS
    half = kh // 2
    lane_lo = lax.broadcasted_iota(jnp.int32, (GRID_W, 128), 1) < NA_HEAD_DIM

    def window(i):
        r = blk * NA_ROW_BLOCK + i
        rs = jnp.clip(r - half, 0, rows - kh)
        dr0 = rs - r + (NA_ROWS - 1)
        return pl.multiple_of(rs * GRID_W, GRID_W), dr0

    def bias(h, dr0):
        return jnp.concatenate([bias_ref[h, dr0 + 2 * j] for j in range(kh // 2)], axis=-1)

    def score_stage(i):
        koff, _ = window(i)
        ss = []
        for p in range(NA_HEADS // 2):
            lanes = slice(p * 128, (p + 1) * 128)
            qp = q_ref[i * GRID_W:(i + 1) * GRID_W, lanes]
            ql = jnp.concatenate([qp * hm_ref[0], qp * hm_ref[1]], axis=0)
            kw = k_ref[0, pl.ds(koff, kh * GRID_W), lanes]
            ss.append(lax.dot_general(ql, kw, (((1,), (1,)), ((), ())),
                                      preferred_element_type=F32))
        return ss

    def output_stage(i, ss):
        koff, dr0 = window(i)
        es, ls = [], []
        for p in range(NA_HEADS // 2):
            s = ss[p] + jnp.concatenate([bias(2 * p, dr0), bias(2 * p + 1, dr0)], axis=0)
            m = jnp.max(s, axis=-1, keepdims=True)
            e = jnp.exp(s - m)
            ls.append(jnp.sum(e, axis=-1, keepdims=True))
            es.append(e.astype(BF16))
        for p in range(NA_HEADS // 2):
            lanes = slice(p * 128, (p + 1) * 128)
            vw = v_ref[0, pl.ds(koff, kh * GRID_W), lanes]
            pv = _dot(es[p], vw) / ls[p]
            o = jnp.where(lane_lo, pv[:GRID_W], pv[GRID_W:])
            o_ref[i * GRID_W:(i + 1) * GRID_W, lanes] = o.astype(BF16)

    ss = score_stage(0)
    for i in range(NA_ROW_BLOCK):
        nxt = score_stage(i + 1) if i + 1 < NA_ROW_BLOCK else None
        output_stage(i, ss)
        ss = nxt


def _na(q, k, v, layer, bias, batch, seq):
    rows = seq // GRID_W
    assert rows >= NA_ROWS and NA_ROWS % 2 == 0
    blk_tokens = NA_ROW_BLOCK * GRID_W
    nblk = rows // NA_ROW_BLOCK
    head_mask = (jnp.arange(128)[None, None, :] // NA_HEAD_DIM
                 == jnp.arange(2)[:, None, None]).astype(BF16)
    head_mask = jnp.broadcast_to(head_mask, (2, GRID_W, 128))
    q_tile = pl.BlockSpec((blk_tokens, NA_WIDTH), lambda b, i: (b * nblk + i, 0))
    kv_spec = pl.BlockSpec((1, seq, NA_WIDTH), lambda b, i: (b, 0, 0))
    return pl.pallas_call(
        functools.partial(_na_kernel, rows=rows),
        grid=(batch, nblk),
        in_specs=[q_tile, kv_spec, kv_spec,
                  _layer_spec((NA_HEADS, 2 * NA_ROWS - 2, GRID_W, 2 * GRID_W), layer),
                  _const_spec((2, GRID_W, 128))],
        out_specs=q_tile,
        out_shape=jax.ShapeDtypeStruct((batch * seq, NA_WIDTH), BF16),
        compiler_params=_params(56),
        name="na",
    )(q, k.reshape(batch, seq, NA_WIDTH), v.reshape(batch, seq, NA_WIDTH), bias, head_mask)


def _dft_tables():
    two_pi = 2.0 * math.pi
    n = DFT_ROWS * DFT_COLS
    k1 = jnp.arange(DFT_ROWS)
    ang = ((k1[:, None] * k1[None, :]) % DFT_ROWS).astype(F32) * (two_pi / DFT_ROWS)
    f = jnp.concatenate([jnp.cos(ang), -jnp.sin(ang)], axis=0)
    f_wide = jnp.repeat(f, F1_T2_BLOCK, axis=1)
    same_t2 = (jnp.arange(DFT_ROWS * F1_T2_BLOCK)[None, None, :] % F1_T2_BLOCK
               == jnp.arange(F1_T2_BLOCK)[None, :, None])
    f1 = jnp.where(same_t2, f_wide[:, None, :], 0.0).reshape(
        2 * DFT_ROWS * F1_T2_BLOCK, DFT_ROWS * F1_T2_BLOCK)
    k2 = jnp.arange(DFT_COLS)
    t2 = jnp.arange(DFT_COLS)
    a = ((k2[:, None] * t2[None, :]) % DFT_COLS).astype(F32) * (two_pi / DFT_COLS)
    b = ((k1[:, None] * t2[None, :]) % n).astype(F32) * (two_pi / n)
    ca, sa = jnp.cos(a)[None], jnp.sin(a)[None]
    cb, sb = jnp.cos(b)[:, None, :], jnp.sin(b)[:, None, :]
    gr = ca * cb - sa * sb
    gi = -(sa * cb + ca * sb)
    g = jnp.concatenate([jnp.concatenate([gr, -gi], axis=2),
                         jnp.concatenate([gi, gr], axis=2)], axis=1)
    c = jnp.arange(F_GROUP_DIM)
    ang = ((c[:, None] * c[None, :]) % F_GROUP_DIM).astype(F32) * (two_pi / F_GROUP_DIM)
    eye = jnp.eye(MXU_COLS_V7X // F_GROUP_DIM, dtype=F32)
    cc = jnp.kron(eye, jnp.cos(ang))
    ss = jnp.kron(eye, jnp.sin(ang))
    return f1.astype(BF16), g.astype(BF16), cc.astype(BF16), ss.astype(BF16)


def _f1_kernel(f_ref, x_ref, o_ref):
    per_store = BF16_SUBLANES // F1_T2_BLOCK
    for s in range(F1_T2_STEP // BF16_SUBLANES):
        zs = []
        for j in range(per_store):
            lo = (s * per_store + j) * F1_T2_BLOCK
            x = x_ref[0, :, lo:lo + F1_T2_BLOCK, :]
            x = x.reshape(DFT_ROWS * F1_T2_BLOCK, F_WIDTH).astype(BF16)
            z = _dot(f_ref[...], x)
            zs.append(z.reshape(2, DFT_ROWS, F1_T2_BLOCK, F_WIDTH))
        cols = slice(s * BF16_SUBLANES, (s + 1) * BF16_SUBLANES)
        o_ref[0, :, :, cols, :] = jnp.concatenate(zs, axis=2).astype(BF16)


def _f3_kernel(a_ref, g_ref, cc_ref, ss_ref, o_ref, zr_ref, zi_ref):
    for j in range(F3_K1_BLOCK):
        a = jnp.concatenate([a_ref[0, 0, j], a_ref[0, 1, j]], axis=0)
        z = _dot(g_ref[j], a)
        zr_ref[j * DFT_COLS:(j + 1) * DFT_COLS] = z[:DFT_COLS].astype(BF16)
        zi_ref[j * DFT_COLS:(j + 1) * DFT_COLS] = z[DFT_COLS:].astype(BF16)
    scale = 1.0 / math.sqrt(DFT_ROWS * DFT_COLS * F_GROUP_DIM)
    for p in range(F_WIDTH // MXU_COLS_V7X):
        cols = slice(p * MXU_COLS_V7X, (p + 1) * MXU_COLS_V7X)
        y = _dot(zr_ref[:, cols], cc_ref[...]) + _dot(zi_ref[:, cols], ss_ref[...])
        y = (y * scale).astype(BF16)
        for j in range(F3_K1_BLOCK):
            o_ref[0, j, :, cols] = y[j * DFT_COLS:(j + 1) * DFT_COLS]


def _fourier(u, tables, batch, seq):
    f1, g, cc, ss = tables
    f1_rows = 2 * DFT_ROWS * F1_T2_BLOCK
    a = pl.pallas_call(
        _f1_kernel,
        grid=(batch, DFT_COLS // F1_T2_STEP),
        in_specs=[_const_spec((f1_rows, f1_rows // 2)),
                  pl.BlockSpec((1, DFT_ROWS, F1_T2_STEP, F_WIDTH), lambda b, i: (b, 0, i, 0))],
        out_specs=pl.BlockSpec((1, 2, DFT_ROWS, F1_T2_STEP, F_WIDTH), lambda b, i: (b, 0, 0, i, 0)),
        out_shape=jax.ShapeDtypeStruct((batch, 2, DFT_ROWS, DFT_COLS, F_WIDTH), BF16),
        compiler_params=_params(18),
        name="dft_rows",
    )(f1, u.reshape(batch, DFT_ROWS, DFT_COLS, F_WIDTH))
    y = pl.pallas_call(
        _f3_kernel,
        grid=(batch, DFT_ROWS // F3_K1_BLOCK),
        in_specs=[pl.BlockSpec((1, 2, F3_K1_BLOCK, DFT_COLS, F_WIDTH), lambda b, i: (b, 0, i, 0, 0)),
                  pl.BlockSpec((F3_K1_BLOCK, 2 * DFT_COLS, 2 * DFT_COLS), lambda b, i: (i, 0, 0)),
                  _const_spec((MXU_COLS_V7X, MXU_COLS_V7X)), _const_spec((MXU_COLS_V7X, MXU_COLS_V7X))],
        out_specs=pl.BlockSpec((1, F3_K1_BLOCK, DFT_COLS, F_WIDTH), lambda b, i: (b, i, 0, 0)),
        out_shape=jax.ShapeDtypeStruct((batch, DFT_ROWS, DFT_COLS, F_WIDTH), BF16),
        scratch_shapes=[pltpu.VMEM((F3_K1_BLOCK * DFT_COLS, F_WIDTH), BF16),
                        pltpu.VMEM((F3_K1_BLOCK * DFT_COLS, F_WIDTH), BF16)],
        compiler_params=_params(22),
        name="dft_cols_channels",
    )(a, g, cc, ss)
    return jnp.transpose(y, (0, 2, 1, 3)).reshape(batch * seq, F_WIDTH)


def _mixout_kernel(x_ref, g_ref, att_ref, fou_ref, wg_ref, gb_ref, wna_ref, wf_ref, wo_ref,
                   *rest, n_cast):
    cast_src, o_ref, cast_dst = rest[:n_cast], rest[n_cast], rest[n_cast + 1:]
    _cast_blocks(cast_src, cast_dst)
    for t in range(TOKEN_TILE // SUB_TILE):
        rows = slice(t * SUB_TILE, (t + 1) * SUB_TILE)
        x = x_ref[rows, :]
        h = _rms(x, g_ref[...]).astype(BF16)
        gates = jax.nn.sigmoid(_dot(h, wg_ref[...]) + gb_ref[...])
        y_na = _dot(att_ref[rows, :], wna_ref[...])
        y_f = _dot(fou_ref[rows, :], wf_ref[...])
        m = gates[:, :D_MODEL] * y_na + gates[:, D_MODEL:] * y_f
        o_ref[rows, :] = x + _dot(m.astype(BF16), wo_ref[...])


def _mixout(x, layer, g, att, fou, w_in, gate_bias, w_na, w_f, w_o, cast=()):
    n = x.shape[0]
    steps = n // TOKEN_TILE
    tile = pl.BlockSpec((TOKEN_TILE, D_MODEL), lambda i: (i, 0))
    half = pl.BlockSpec((TOKEN_TILE, NA_WIDTH), lambda i: (i, 0))
    assert w_in.shape[1] == 4 * D_MODEL
    cast_in, cast_out, cast_shapes = _cast_specs(cast, steps)
    y, *copies = pl.pallas_call(
        functools.partial(_mixout_kernel, n_cast=len(cast)),
        grid=(steps,),
        in_specs=[tile, _layer_spec((1, D_MODEL), layer), half, half,
                  _const_spec((D_MODEL, 2 * D_MODEL), 1),
                  _layer_spec((1, 2 * D_MODEL), layer),
                  _const_spec((NA_WIDTH, D_MODEL)), _const_spec((F_WIDTH, D_MODEL)),
                  _const_spec((D_MODEL, D_MODEL))] + cast_in,
        out_specs=[tile] + cast_out,
        out_shape=[jax.ShapeDtypeStruct((n, D_MODEL), F32)] + cast_shapes,
        compiler_params=_params(36),
        name="mix_out",
    )(x, g, att, fou, w_in, gate_bias, w_na, w_f, w_o, *[arr for arr, _ in cast])
    return y, copies


def kernel(x, ffn1_norm, ffn1_w_in, ffn1_w_out, mix_norm, mix_w_in, mix_gate_bias, na_rpb,
           na_w_out, f_w_out, mix_w_o, ffn2_norm, ffn2_w_in, ffn2_w_out, final_norm):
    batch, seq, d = x.shape
    depth = ffn1_norm.shape[0]
    assert d == D_MODEL and seq == DFT_ROWS * DFT_COLS and seq % (GRID_W * NA_ROW_BLOCK) == 0
    assert (batch * seq) % TOKEN_TILE == 0 and (batch * seq) % FFN_TOKEN_TILE == 0
    tables = _dft_tables()
    na_bias = _na_bias_tables(na_rpb)
    gain = lambda g: g.reshape(depth, 1, D_MODEL)
    g1, g2, mix_g = gain(ffn1_norm), gain(ffn2_norm), gain(mix_norm)
    gate_bias = mix_gate_bias.reshape(depth, 1, 2 * D_MODEL)
    xs = x.reshape(batch * seq, d)
    ffn1_w = [ffn1_w_in[0].astype(BF16), ffn1_w_out[0].astype(BF16)]
    for l in range(depth):
        xs, (w_in, w_na, w_f, w_o) = _ffn(
            xs, l, g1, *ffn1_w,
            cast=[(mix_w_in, l), (na_w_out, l), (f_w_out, l), (mix_w_o, l)])
        q, k, v, u = _proj(xs, l, mix_g, w_in)
        att = _na(q, k, v, l, na_bias, batch, seq)
        fou = _fourier(u, tables, batch, seq)
        xs, ffn2_w = _mixout(xs, l, mix_g, att, fou, w_in, gate_bias, w_na, w_f, w_o,
                             cast=[(ffn2_w_in, l), (ffn2_w_out, l)])
        last = l == depth - 1
        xs, ffn1_w = _ffn(xs, l, g2, *ffn2_w, final_g=final_norm if last else None,
                          cast=[] if last else [(ffn1_w_in, l + 1), (ffn1_w_out, l + 1)])
    return xs.reshape(batch, seq, d)
```

```python
import functools
import math

import jax
import jax.numpy as jnp
from jax import lax
from jax.experimental import pallas as pl
from jax.experimental.pallas import tpu as pltpu

D_MODEL = 1024
GRID_W = 64
NA_HEADS = 8
NA_HEAD_DIM = 64
NA_WIDTH = NA_HEADS * NA_HEAD_DIM
NA_ROWS = 8
NA_COLS = 16
F_GROUPS = 4
F_GROUP_DIM = 128
F_WIDTH = F_GROUPS * F_GROUP_DIM
D_FF = 2816
RMS_EPS = 1e-6

BF16 = jnp.bfloat16
F32 = jnp.float32

MXU_COLS_V7X = 256
VMEM_BYTES_V7X = 64 * 1024 * 1024

TOKEN_TILE = 1024
SUB_TILE = 512
FFN_TOKEN_TILE = 1024
FF_CHUNK = MXU_COLS_V7X
NA_ROW_BLOCK = 16
DFT_ROWS = 64
DFT_COLS = 128
SUBLANES_F32 = 8
BF16_SUBLANES = 16
F1_T2_BLOCK = SUBLANES_F32
F1_T2_STEP = 32
F3_K1_BLOCK = 16
F1_STEPS = DFT_COLS // F1_T2_STEP
F3_STEPS = DFT_ROWS // F3_K1_BLOCK


def _params(vmem_mib):
    return pltpu.CompilerParams(
        dimension_semantics=None,
        vmem_limit_bytes=vmem_mib * 1024 * 1024,
    )


def _const_spec(shape, col_block=0):
    index = (0,) * (len(shape) - 1) + (col_block,)
    return pl.BlockSpec(shape, lambda *_: index, pipeline_mode=pl.Buffered(1))


def _cast_specs(params, steps):
    in_specs, out_specs, out_shapes = [], [], []
    for arr, layer in params:
        _, rows, cols = arr.shape
        nblk = steps
        while rows % nblk or (rows // nblk) % BF16_SUBLANES:
            nblk //= 2
        blk = rows // nblk
        in_specs.append(pl.BlockSpec(
            (None, blk, cols), lambda i, layer=layer, nblk=nblk: (layer, jnp.minimum(i, nblk - 1), 0)))
        out_specs.append(pl.BlockSpec(
            (blk, cols), lambda i, nblk=nblk: (jnp.minimum(i, nblk - 1), 0)))
        out_shapes.append(jax.ShapeDtypeStruct((rows, cols), BF16))
    return in_specs, out_specs, out_shapes


def _cast_blocks(src_refs, dst_refs):
    for src, dst in zip(src_refs, dst_refs):
        dst[...] = src[...].astype(BF16)


def _layer_spec(shape, layer):
    index = (layer,) + (0,) * len(shape)
    return pl.BlockSpec((None,) + tuple(shape), lambda *_: index, pipeline_mode=pl.Buffered(1))


def _rms(x, g):
    ms = jnp.mean(x * x, axis=-1, keepdims=True)
    return x * lax.rsqrt(ms + RMS_EPS) * g


def _dot(a, b):
    return jnp.dot(a, b, preferred_element_type=F32)


def _ffn_kernel(x_ref, g_ref, win_ref, wout_ref, *rest, final, n_cast):
    rest = list(rest)
    gf_ref = rest.pop(0) if final else None
    cast_src, o_ref, cast_dst, act_ref = (rest[:n_cast], rest[n_cast],
                                          rest[n_cast + 1:2 * n_cast + 1], rest[-1])
    _cast_blocks(cast_src, cast_dst)
    x = x_ref[...]
    h = _rms(x, g_ref[...]).astype(BF16)
    for c in range(D_FF // FF_CHUNK):
        lo = c * FF_CHUNK
        g = _dot(h, win_ref[:, lo:lo + FF_CHUNK])
        u = _dot(h, win_ref[:, D_FF + lo:D_FF + lo + FF_CHUNK])
        act_ref[:, lo:lo + FF_CHUNK] = (g * jax.nn.sigmoid(g) * u).astype(BF16)
    y = x + 0.5 * _dot(act_ref[...], wout_ref[...])
    if final:
        y = _rms(y, gf_ref[...])
    o_ref[...] = y


def _ffn(x, layer, g, w_in, w_out, final_g=None, cast=()):
    n = x.shape[0]
    steps = n // FFN_TOKEN_TILE
    final = final_g is not None
    tile = pl.BlockSpec((FFN_TOKEN_TILE, D_MODEL), lambda i: (i, 0))
    in_specs = [tile, _layer_spec((1, D_MODEL), layer), _const_spec((D_MODEL, 2 * D_FF)),
                _const_spec((D_FF, D_MODEL))]
    args = [x, g, w_in, w_out]
    if final:
        in_specs.append(_const_spec((1, D_MODEL)))
        args.append(final_g.reshape(1, D_MODEL))
    cast_in, cast_out, cast_shapes = _cast_specs(cast, steps)
    y, *copies = pl.pallas_call(
        functools.partial(_ffn_kernel, final=final, n_cast=len(cast)),
        grid=(steps,),
        in_specs=in_specs + cast_in,
        out_specs=[tile] + cast_out,
        out_shape=[jax.ShapeDtypeStruct((n, D_MODEL), F32)] + cast_shapes,
        scratch_shapes=[pltpu.VMEM((FFN_TOKEN_TILE, D_FF), BF16)],
        compiler_params=_params(58),
        name="ffn",
    )(*args, *[arr for arr, _ in cast])
    return y, copies


def _proj_kernel(x_ref, g_ref, w_ref, q_ref, k_ref, v_ref, u_ref):
    s = NA_WIDTH
    for t in range(TOKEN_TILE // SUB_TILE):
        rows = slice(t * SUB_TILE, (t + 1) * SUB_TILE)
        h = _rms(x_ref[rows, :], g_ref[...]).astype(BF16)
        z = _dot(h, w_ref[...])
        q_ref[rows, :] = (z[:, :s] * (NA_HEAD_DIM ** -0.5)).astype(BF16)
        k_ref[rows, :] = z[:, s:2 * s].astype(BF16)
        v_ref[rows, :] = z[:, 2 * s:3 * s].astype(BF16)
        u_ref[rows, :] = z[:, 3 * s:3 * s + F_WIDTH]


def _proj(x, layer, g, w):
    n = x.shape[0]
    cols = 3 * NA_WIDTH + F_WIDTH
    assert w.shape[1] == 2 * cols
    out_tile = pl.BlockSpec((TOKEN_TILE, NA_WIDTH), lambda i: (i, 0))
    out_sds = jax.ShapeDtypeStruct((n, NA_WIDTH), BF16)
    return pl.pallas_call(
        _proj_kernel,
        grid=(n // TOKEN_TILE,),
        in_specs=[pl.BlockSpec((TOKEN_TILE, D_MODEL), lambda i: (i, 0)),
                  _layer_spec((1, D_MODEL), layer), _const_spec((D_MODEL, cols), 0)],
        out_specs=[out_tile] * 4,
        out_shape=[out_sds] * 3 + [jax.ShapeDtypeStruct((n, F_WIDTH), F32)],
        compiler_params=_params(40),
        name="mix_proj",
    )(x, g, w)


def _na_bias_tables(rpb):
    ncol = 2 * NA_COLS - 1
    qc = jnp.arange(GRID_W)
    lane = jnp.arange(2 * GRID_W)
    kc, side = lane % GRID_W, lane // GRID_W
    ws = jnp.clip(qc - NA_COLS // 2, 0, GRID_W - NA_COLS)
    ok = (kc[None, :] >= ws[:, None]) & (kc[None, :] < ws[:, None] + NA_COLS)
    dc = kc[None, :] - qc[:, None] + (NA_COLS - 1)
    src = jnp.where(ok, side[None, :] * ncol + dc, -1)
    pick = (src[:, :, None] == jnp.arange(2 * ncol)[None, None, :]).astype(F32)
    both = jnp.concatenate([rpb[:, :, :-1], rpb[:, :, 1:]], axis=-1).astype(F32)
    table = jnp.einsum('lhdm,qkm->lhdqk', both, pick, precision=lax.Precision.HIGHEST)
    return jnp.where(ok[None, None, None], table, -jnp.inf)


def _na_kernel(q_ref, k_ref, v_ref, bias_ref, hm_ref, o_ref, *, rows):
    blk = pl.program_id(1)
    kh = NA_ROWS
    half = kh // 2
    lane_lo = lax.broadcasted_iota(jnp.int32, (GRID_W, 128), 1) < NA_HEAD_DIM

    def window(i):
        r = blk * NA_ROW_BLOCK + i
        rs = jnp.clip(r - half, 0, rows - kh)
        dr0 = rs - r + (NA_ROWS - 1)
        return pl.multiple_of(rs * GRID_W, GRID_W), dr0

    def bias(h, dr0):
        return jnp.concatenate([bias_ref[h, dr0 + 2 * j] for j in range(kh // 2)], axis=-1)

    def score_stage(i):
        koff, _ = window(i)
        ss = []
        for p in range(NA_HEADS // 2):
            lanes = slice(p * 128, (p + 1) * 128)
            qp = q_ref[i * GRID_W:(i + 1) * GRID_W, lanes]
            ql = jnp.concatenate([qp * hm_ref[0], qp * hm_ref[1]], axis=0)
            kw = k_ref[0, pl.ds(koff, kh * GRID_W), lanes]
            ss.append(lax.dot_general(ql, kw, (((1,), (1,)), ((), ())),
                                      preferred_element_type=F32))
        return ss

    def output_stage(i, ss):
        koff, dr0 = window(i)
        es, ls = [], []
        for p in range(NA_HEADS // 2):
            s = ss[p] + jnp.concatenate([bias(2 * p, dr0), bias(2 * p + 1, dr0)], axis=0)
            m = jnp.max(s, axis=-1, keepdims=True)
            e = jnp.exp(s - m)
            ls.append(jnp.sum(e, axis=-1, keepdims=True))
            es.append(e.astype(BF16))
        for p in range(NA_HEADS // 2):
            lanes = slice(p * 128, (p + 1) * 128)
            vw = v_ref[0, pl.ds(koff, kh * GRID_W), lanes]
            pv = _dot(es[p], vw) / ls[p]
            o = jnp.where(lane_lo, pv[:GRID_W], pv[GRID_W:])
            o_ref[i * GRID_W:(i + 1) * GRID_W, lanes] = o.astype(BF16)

    ss = score_stage(0)
    for i in range(NA_ROW_BLOCK):
        nxt = score_stage(i + 1) if i + 1 < NA_ROW_BLOCK else None
        output_stage(i, ss)
        ss = nxt


def _na(q, k, v, layer, bias, batch, seq):
    rows = seq // GRID_W
    assert rows >= NA_ROWS and NA_ROWS % 2 == 0
    blk_tokens = NA_ROW_BLOCK * GRID_W
    nblk = rows // NA_ROW_BLOCK
    head_mask = (jnp.arange(128)[None, None, :] // NA_HEAD_DIM
                 == jnp.arange(2)[:, None, None]).astype(BF16)
    head_mask = jnp.broadcast_to(head_mask, (2, GRID_W, 128))
    q_tile = pl.BlockSpec((blk_tokens, NA_WIDTH), lambda b, i: (b * nblk + i, 0))
    kv_spec = pl.BlockSpec((1, seq, NA_WIDTH), lambda b, i: (b, 0, 0))
    return pl.pallas_call(
        functools.partial(_na_kernel, rows=rows),
        grid=(batch, nblk),
        in_specs=[q_tile, kv_spec, kv_spec,
                  _layer_spec((NA_HEADS, 2 * NA_ROWS - 2, GRID_W, 2 * GRID_W), layer),
                  _const_spec((2, GRID_W, 128))],
        out_specs=q_tile,
        out_shape=jax.ShapeDtypeStruct((batch * seq, NA_WIDTH), BF16),
        compiler_params=_params(56),
        name="na",
    )(q, k.reshape(batch, seq, NA_WIDTH), v.reshape(batch, seq, NA_WIDTH), bias, head_mask)


def _dft_tables():
    two_pi = 2.0 * math.pi
    n = DFT_ROWS * DFT_COLS
    k1 = jnp.arange(DFT_ROWS)
    ang = ((k1[:, None] * k1[None, :]) % DFT_ROWS).astype(F32) * (two_pi / DFT_ROWS)
    f = jnp.concatenate([jnp.cos(ang), -jnp.sin(ang)], axis=0)
    f_wide = jnp.repeat(f, F1_T2_BLOCK, axis=1)
    same_t2 = (jnp.arange(DFT_ROWS * F1_T2_BLOCK)[None, None, :] % F1_T2_BLOCK
               == jnp.arange(F1_T2_BLOCK)[None, :, None])
    f1 = jnp.where(same_t2, f_wide[:, None, :], 0.0).reshape(
        2 * DFT_ROWS * F1_T2_BLOCK, DFT_ROWS * F1_T2_BLOCK)
    k2 = jnp.arange(DFT_COLS)
    t2 = jnp.arange(DFT_COLS)
    a = ((k2[:, None] * t2[None, :]) % DFT_COLS).astype(F32) * (two_pi / DFT_COLS)
    b = ((k1[:, None] * t2[None, :]) % n).astype(F32) * (two_pi / n)
    ca, sa = jnp.cos(a)[None], jnp.sin(a)[None]
    cb, sb = jnp.cos(b)[:, None, :], jnp.sin(b)[:, None, :]
    gr = ca * cb - sa * sb
    gi = -(sa * cb + ca * sb)
    g = jnp.concatenate([jnp.concatenate([gr, -gi], axis=2),
                         jnp.concatenate([gi, gr], axis=2)], axis=1)
    c = jnp.arange(F_GROUP_DIM)
    ang = ((c[:, None] * c[None, :]) % F_GROUP_DIM).astype(F32) * (two_pi / F_GROUP_DIM)
    eye = jnp.eye(MXU_COLS_V7X // F_GROUP_DIM, dtype=F32)
    cc = jnp.kron(eye, jnp.cos(ang))
    ss = jnp.kron(eye, jnp.sin(ang))
    return f1.astype(BF16), g.astype(BF16), cc.astype(BF16), ss.astype(BF16)


def _fnet_kernel(f_ref, x_ref, g_ref, cc_ref, ss_ref, o_ref, a_ref, zr_ref, zi_ref):
    step = pl.program_id(1)

    @pl.when(step < F1_STEPS)
    def _stage1():
        per_store = BF16_SUBLANES // F1_T2_BLOCK
        base = step * F1_T2_STEP
        for s in range(F1_T2_STEP // BF16_SUBLANES):
            zs = []
            for j in range(per_store):
                lo = (s * per_store + j) * F1_T2_BLOCK
                x = x_ref[0, :, lo:lo + F1_T2_BLOCK, :]
                x = x.reshape(DFT_ROWS * F1_T2_BLOCK, F_WIDTH).astype(BF16)
                z = _dot(f_ref[...], x)
                zs.append(z.reshape(2, DFT_ROWS, F1_T2_BLOCK, F_WIDTH))
            t2 = pl.multiple_of(base + s * BF16_SUBLANES, BF16_SUBLANES)
            a_ref[:, :, pl.ds(t2, BF16_SUBLANES), :] = jnp.concatenate(zs, axis=2).astype(BF16)

    @pl.when(step >= F1_STEPS)
    def _stage2():
        k1 = (step - F1_STEPS) * F3_K1_BLOCK
        for j in range(F3_K1_BLOCK):
            a = jnp.concatenate([a_ref[0, k1 + j], a_ref[1, k1 + j]], axis=0)
            z = _dot(g_ref[j], a)
            zr_ref[j * DFT_COLS:(j + 1) * DFT_COLS] = z[:DFT_COLS].astype(BF16)
            zi_ref[j * DFT_COLS:(j + 1) * DFT_COLS] = z[DFT_COLS:].astype(BF16)
        scale = 1.0 / math.sqrt(DFT_ROWS * DFT_COLS * F_GROUP_DIM)
        for p in range(F_WIDTH // MXU_COLS_V7X):
            cols = slice(p * MXU_COLS_V7X, (p + 1) * MXU_COLS_V7X)
            y = _dot(zr_ref[:, cols], cc_ref[...]) + _dot(zi_ref[:, cols], ss_ref[...])
            y = (y * scale).astype(BF16)
            for j in range(F3_K1_BLOCK):
                o_ref[0, j, :, cols] = y[j * DFT_COLS:(j + 1) * DFT_COLS]


def _fourier(u, tables, batch, seq):
    f1, g, cc, ss = tables
    f1_rows = 2 * DFT_ROWS * F1_T2_BLOCK
    y = pl.pallas_call(
        _fnet_kernel,
        grid=(batch, F1_STEPS + F3_STEPS),
        in_specs=[_const_spec((f1_rows, f1_rows // 2)),
                  pl.BlockSpec((1, DFT_ROWS, F1_T2_STEP, F_WIDTH),
                               lambda b, i: (b, 0, jnp.minimum(i, F1_STEPS - 1), 0)),
                  pl.BlockSpec((F3_K1_BLOCK, 2 * DFT_COLS, 2 * DFT_COLS),
                               lambda b, i: (jnp.maximum(i - F1_STEPS, 0), 0, 0)),
                  _const_spec((MXU_COLS_V7X, MXU_COLS_V7X)), _const_spec((MXU_COLS_V7X, MXU_COLS_V7X))],
        out_specs=pl.BlockSpec((1, F3_K1_BLOCK, DFT_COLS, F_WIDTH),
                               lambda b, i: (b, jnp.maximum(i - F1_STEPS, 0), 0, 0)),
        out_shape=jax.ShapeDtypeStruct((batch, DFT_ROWS, DFT_COLS, F_WIDTH), BF16),
        scratch_shapes=[pltpu.VMEM((2, DFT_ROWS, DFT_COLS, F_WIDTH), BF16),
                        pltpu.VMEM((F3_K1_BLOCK * DFT_COLS, F_WIDTH), BF16),
                        pltpu.VMEM((F3_K1_BLOCK * DFT_COLS, F_WIDTH), BF16)],
        compiler_params=_params(52),
        name="fnet",
    )(f1, u.reshape(batch, DFT_ROWS, DFT_COLS, F_WIDTH), g, cc, ss)
    return jnp.transpose(y, (0, 2, 1, 3)).reshape(batch * seq, F_WIDTH)


def _mixout_kernel(x_ref, g_ref, att_ref, fou_ref, wg_ref, gb_ref, wna_ref, wf_ref, wo_ref,
                   *rest, n_cast):
    cast_src, o_ref, cast_dst = rest[:n_cast], rest[n_cast], rest[n_cast + 1:]
    _cast_blocks(cast_src, cast_dst)
    for t in range(TOKEN_TILE // SUB_TILE):
        rows = slice(t * SUB_TILE, (t + 1) * SUB_TILE)
        x = x_ref[rows, :]
        h = _rms(x, g_ref[...]).astype(BF16)
        gates = jax.nn.sigmoid(_dot(h, wg_ref[...]) + gb_ref[...])
        y_na = _dot(att_ref[rows, :], wna_ref[...])
        y_f = _dot(fou_ref[rows, :], wf_ref[...])
        m = gates[:, :D_MODEL] * y_na + gates[:, D_MODEL:] * y_f
        o_ref[rows, :] = x + _dot(m.astype(BF16), wo_ref[...])


def _mixout(x, layer, g, att, fou, w_in, gate_bias, w_na, w_f, w_o, cast=()):
    n = x.shape[0]
    steps = n // TOKEN_TILE
    tile = pl.BlockSpec((TOKEN_TILE, D_MODEL), lambda i: (i, 0))
    half = pl.BlockSpec((TOKEN_TILE, NA_WIDTH), lambda i: (i, 0))
    assert w_in.shape[1] == 4 * D_MODEL
    cast_in, cast_out, cast_shapes = _cast_specs(cast, steps)
    y, *copies = pl.pallas_call(
        functools.partial(_mixout_kernel, n_cast=len(cast)),
        grid=(steps,),
        in_specs=[tile, _layer_spec((1, D_MODEL), layer), half, half,
                  _const_spec((D_MODEL, 2 * D_MODEL), 1),
                  _layer_spec((1, 2 * D_MODEL), layer),
                  _const_spec((NA_WIDTH, D_MODEL)), _const_spec((F_WIDTH, D_MODEL)),
                  _const_spec((D_MODEL, D_MODEL))] + cast_in,
        out_specs=[tile] + cast_out,
        out_shape=[jax.ShapeDtypeStruct((n, D_MODEL), F32)] + cast_shapes,
        compiler_params=_params(48),
        name="mix_out",
    )(x, g, att, fou, w_in, gate_bias, w_na, w_f, w_o, *[arr for arr, _ in cast])
    return y, copies


def kernel(x, ffn1_norm, ffn1_w_in, ffn1_w_out, mix_norm, mix_w_in, mix_gate_bias, na_rpb,
           na_w_out, f_w_out, mix_w_o, ffn2_norm, ffn2_w_in, ffn2_w_out, final_norm):
    batch, seq, d = x.shape
    depth = ffn1_norm.shape[0]
    assert d == D_MODEL and seq == DFT_ROWS * DFT_COLS and seq % (GRID_W * NA_ROW_BLOCK) == 0
    assert (batch * seq) % TOKEN_TILE == 0 and (batch * seq) % FFN_TOKEN_TILE == 0
    tables = _dft_tables()
    na_bias = _na_bias_tables(na_rpb)
    gain = lambda g: g.reshape(depth, 1, D_MODEL)
    g1, g2, mix_g = gain(ffn1_norm), gain(ffn2_norm), gain(mix_norm)
    gate_bias = mix_gate_bias.reshape(depth, 1, 2 * D_MODEL)
    xs = x.reshape(batch * seq, d)
    ffn1_w = [ffn1_w_in[0].astype(BF16), ffn1_w_out[0].astype(BF16)]
    for l in range(depth):
        xs, (w_in, w_na, w_f, w_o) = _ffn(
            xs, l, g1, *ffn1_w,
            cast=[(mix_w_in, l), (na_w_out, l), (f_w_out, l), (mix_w_o, l)])
        q, k, v, u = _proj(xs, l, mix_g, w_in)
        att = _na(q, k, v, l, na_bias, batch, seq)
        fou = _fourier(u, tables, batch, seq)
        xs, ffn2_w = _mixout(xs, l, mix_g, att, fou, w_in, gate_bias, w_na, w_f, w_o,
                             cast=[(ffn2_w_in, l), (ffn2_w_out, l)])
        last = l == depth - 1
        xs, ffn1_w = _ffn(xs, l, g2, *ffn2_w, final_g=final_norm if last else None,
                          cast=[] if last else [(ffn1_w_in, l + 1), (ffn1_w_out, l + 1)])
    return xs.reshape(batch, seq, d)
```

```python
import functools
import math

import jax
import jax.numpy as jnp
from jax import lax
from jax.experimental import pallas as pl
from jax.experimental.pallas import tpu as pltpu

D_MODEL = 1024
GRID_W = 64
NA_HEADS = 8
NA_HEAD_DIM = 64
NA_WIDTH = NA_HEADS * NA_HEAD_DIM
NA_ROWS = 8
NA_COLS = 16
F_GROUPS = 4
F_GROUP_DIM = 128
F_WIDTH = F_GROUPS * F_GROUP_DIM
D_FF = 2816
RMS_EPS = 1e-6

BF16 = jnp.bfloat16
F32 = jnp.float32

MXU_COLS_V7X = 256
VMEM_BYTES_V7X = 64 * 1024 * 1024

TOKEN_TILE = 1024
SUB_TILE = 256
FFN_TOKEN_TILE = 1024
FF_CHUNK = MXU_COLS_V7X
NA_ROW_BLOCK = 16
DFT_ROWS = 64
DFT_COLS = 128
SUBLANES_F32 = 8
BF16_SUBLANES = 16
F1_T2_BLOCK = SUBLANES_F32
F1_T2_STEP = 32
F3_K1_BLOCK = 16
F1_STEPS = DFT_COLS // F1_T2_STEP
F3_STEPS = DFT_ROWS // F3_K1_BLOCK


def _params(vmem_mib):
    assert vmem_mib * 1024 * 1024 <= VMEM_BYTES_V7X
    return pltpu.CompilerParams(
        dimension_semantics=None,
        vmem_limit_bytes=vmem_mib * 1024 * 1024,
    )


def _const_spec(shape, col_block=0):
    index = (0,) * (len(shape) - 1) + (col_block,)
    return pl.BlockSpec(shape, lambda *_: index, pipeline_mode=pl.Buffered(1))


def _cast_specs(params, steps):
    in_specs, out_specs, out_shapes = [], [], []
    for arr, layer in params:
        _, rows, cols = arr.shape
        nblk = steps
        while rows % nblk or (rows // nblk) % BF16_SUBLANES:
            nblk //= 2
        blk = rows // nblk
        in_specs.append(pl.BlockSpec(
            (None, blk, cols), lambda i, layer=layer, nblk=nblk: (layer, jnp.minimum(i, nblk - 1), 0)))
        out_specs.append(pl.BlockSpec(
            (blk, cols), lambda i, nblk=nblk: (jnp.minimum(i, nblk - 1), 0)))
        out_shapes.append(jax.ShapeDtypeStruct((rows, cols), BF16))
    return in_specs, out_specs, out_shapes


def _cast_blocks(src_refs, dst_refs):
    for src, dst in zip(src_refs, dst_refs):
        dst[...] = src[...].astype(BF16)


def _layer_spec(shape, layer):
    index = (layer,) + (0,) * len(shape)
    return pl.BlockSpec((None,) + tuple(shape), lambda *_: index, pipeline_mode=pl.Buffered(1))


def _rms(x, g):
    ms = jnp.mean(x * x, axis=-1, keepdims=True)
    return x * lax.rsqrt(ms + RMS_EPS) * g


def _dot(a, b):
    return jnp.dot(a, b, preferred_element_type=F32)


def _ffn_kernel(x_ref, g_ref, win_ref, wout_ref, *rest, final, n_cast):
    rest = list(rest)
    gf_ref = rest.pop(0) if final else None
    cast_src, o_ref, cast_dst, act_ref = (rest[:n_cast], rest[n_cast],
                                          rest[n_cast + 1:2 * n_cast + 1], rest[-1])
    _cast_blocks(cast_src, cast_dst)
    x = x_ref[...]
    h = _rms(x, g_ref[...]).astype(BF16)
    for c in range(D_FF // FF_CHUNK):
        lo = c * FF_CHUNK
        g = _dot(h, win_ref[:, lo:lo + FF_CHUNK])
        u = _dot(h, win_ref[:, D_FF + lo:D_FF + lo + FF_CHUNK])
        act_ref[:, lo:lo + FF_CHUNK] = (g * jax.nn.sigmoid(g) * u).astype(BF16)
    y = x + 0.5 * _dot(act_ref[...], wout_ref[...])
    if final:
        y = _rms(y, gf_ref[...])
    o_ref[...] = y


def _ffn(x, layer, g, w_in, w_out, final_g=None, cast=()):
    n = x.shape[0]
    steps = n // FFN_TOKEN_TILE
    final = final_g is not None
    tile = pl.BlockSpec((FFN_TOKEN_TILE, D_MODEL), lambda i: (i, 0))
    in_specs = [tile, _layer_spec((1, D_MODEL), layer), _const_spec((D_MODEL, 2 * D_FF)),
                _const_spec((D_FF, D_MODEL))]
    args = [x, g, w_in, w_out]
    if final:
        in_specs.append(_const_spec((1, D_MODEL)))
        args.append(final_g.reshape(1, D_MODEL))
    cast_in, cast_out, cast_shapes = _cast_specs(cast, steps)
    y, *copies = pl.pallas_call(
        functools.partial(_ffn_kernel, final=final, n_cast=len(cast)),
        grid=(steps,),
        in_specs=in_specs + cast_in,
        out_specs=[tile] + cast_out,
        out_shape=[jax.ShapeDtypeStruct((n, D_MODEL), F32)] + cast_shapes,
        scratch_shapes=[pltpu.VMEM((FFN_TOKEN_TILE, D_FF), BF16)],
        compiler_params=_params(58),
        name="ffn",
    )(*args, *[arr for arr, _ in cast])
    return y, copies


def _proj_kernel(x_ref, g_ref, w_ref, q_ref, k_ref, v_ref, u_ref):
    s = NA_WIDTH
    for t in range(TOKEN_TILE // SUB_TILE):
        rows = slice(t * SUB_TILE, (t + 1) * SUB_TILE)
        h = _rms(x_ref[rows, :], g_ref[...]).astype(BF16)
        z = _dot(h, w_ref[...])
        q_ref[rows, :] = (z[:, :s] * (NA_HEAD_DIM ** -0.5)).astype(BF16)
        k_ref[rows, :] = z[:, s:2 * s].astype(BF16)
        v_ref[rows, :] = z[:, 2 * s:3 * s].astype(BF16)
        u_ref[rows, :] = z[:, 3 * s:3 * s + F_WIDTH]


def _proj(x, layer, g, w):
    n = x.shape[0]
    cols = 3 * NA_WIDTH + F_WIDTH
    assert w.shape[1] == 2 * cols
    out_tile = pl.BlockSpec((TOKEN_TILE, NA_WIDTH), lambda i: (i, 0))
    out_sds = jax.ShapeDtypeStruct((n, NA_WIDTH), BF16)
    return pl.pallas_call(
        _proj_kernel,
        grid=(n // TOKEN_TILE,),
        in_specs=[pl.BlockSpec((TOKEN_TILE, D_MODEL), lambda i: (i, 0)),
                  _layer_spec((1, D_MODEL), layer), _const_spec((D_MODEL, cols), 0)],
        out_specs=[out_tile] * 4,
        out_shape=[out_sds] * 3 + [jax.ShapeDtypeStruct((n, F_WIDTH), F32)],
        compiler_params=_params(40),
        name="mix_proj",
    )(x, g, w)


def _na_bias_tables(rpb):
    ncol = 2 * NA_COLS - 1
    qc = jnp.arange(GRID_W)
    lane = jnp.arange(2 * GRID_W)
    kc, side = lane % GRID_W, lane // GRID_W
    ws = jnp.clip(qc - NA_COLS // 2, 0, GRID_W - NA_COLS)
    ok = (kc[None, :] >= ws[:, None]) & (kc[None, :] < ws[:, None] + NA_COLS)
    dc = kc[None, :] - qc[:, None] + (NA_COLS - 1)
    src = jnp.where(ok, side[None, :] * ncol + dc, -1)
    pick = (src[:, :, None] == jnp.arange(2 * ncol)[None, None, :]).astype(F32)
    both = jnp.concatenate([rpb[:, :, :-1], rpb[:, :, 1:]], axis=-1).astype(F32)
    table = jnp.einsum('lhdm,qkm->lhdqk', both, pick, precision=lax.Precision.HIGHEST)
    return jnp.where(ok[None, None, None], table, -jnp.inf)


def _na_kernel(q_ref, k_ref, v_ref, bias_ref, hm_ref, o_ref, *, rows):
    blk = pl.program_id(1)
    kh = NA_ROWS
    half = kh // 2
    lane_lo = lax.broadcasted_iota(jnp.int32, (GRID_W, 128), 1) < NA_HEAD_DIM

    def window(i):
        r = blk * NA_ROW_BLOCK + i
        rs = jnp.clip(r - half, 0, rows - kh)
        dr0 = rs - r + (NA_ROWS - 1)
        return pl.multiple_of(rs * GRID_W, GRID_W), dr0

    def bias(h, dr0):
        return jnp.concatenate([bias_ref[h, dr0 + 2 * j] for j in range(kh // 2)], axis=-1)

    def score_stage(i):
        koff, _ = window(i)
        ss = []
        for p in range(NA_HEADS // 2):
            lanes = slice(p * 128, (p + 1) * 128)
            qp = q_ref[i * GRID_W:(i + 1) * GRID_W, lanes]
            ql = jnp.concatenate([qp * hm_ref[0], qp * hm_ref[1]], axis=0)
            kw = k_ref[0, pl.ds(koff, kh * GRID_W), lanes]
            ss.append(lax.dot_general(ql, kw, (((1,), (1,)), ((), ())),
                                      preferred_element_type=F32))
        return ss

    def output_stage(i, ss):
        koff, dr0 = window(i)
        es, ls = [], []
        for p in range(NA_HEADS // 2):
            s = ss[p] + jnp.concatenate([bias(2 * p, dr0), bias(2 * p + 1, dr0)], axis=0)
            m = jnp.max(s, axis=-1, keepdims=True)
            e = jnp.exp(s - m)
            ls.append(jnp.sum(e, axis=-1, keepdims=True))
            es.append(e.astype(BF16))
        for p in range(NA_HEADS // 2):
            lanes = slice(p * 128, (p + 1) * 128)
            vw = v_ref[0, pl.ds(koff, kh * GRID_W), lanes]
            pv = _dot(es[p], vw) / ls[p]
            o = jnp.where(lane_lo, pv[:GRID_W], pv[GRID_W:])
            o_ref[i * GRID_W:(i + 1) * GRID_W, lanes] = o.astype(BF16)

    ss = score_stage(0)
    for i in range(NA_ROW_BLOCK):
        nxt = score_stage(i + 1) if i + 1 < NA_ROW_BLOCK else None
        output_stage(i, ss)
        ss = nxt


def _na(q, k, v, layer, bias, batch, seq):
    rows = seq // GRID_W
    assert rows >= NA_ROWS and NA_ROWS % 2 == 0
    blk_tokens = NA_ROW_BLOCK * GRID_W
    nblk = rows // NA_ROW_BLOCK
    head_mask = (jnp.arange(128)[None, None, :] // NA_HEAD_DIM
                 == jnp.arange(2)[:, None, None]).astype(BF16)
    head_mask = jnp.broadcast_to(head_mask, (2, GRID_W, 128))
    q_tile = pl.BlockSpec((blk_tokens, NA_WIDTH), lambda b, i: (b * nblk + i, 0))
    kv_spec = pl.BlockSpec((1, seq, NA_WIDTH), lambda b, i: (b, 0, 0))
    return pl.pallas_call(
        functools.partial(_na_kernel, rows=rows),
        grid=(batch, nblk),
        in_specs=[q_tile, kv_spec, kv_spec,
                  _layer_spec((NA_HEADS, 2 * NA_ROWS - 2, GRID_W, 2 * GRID_W), layer),
                  _const_spec((2, GRID_W, 128))],
        out_specs=q_tile,
        out_shape=jax.ShapeDtypeStruct((batch * seq, NA_WIDTH), BF16),
        compiler_params=_params(56),
        name="na",
    )(q, k.reshape(batch, seq, NA_WIDTH), v.reshape(batch, seq, NA_WIDTH), bias, head_mask)


def _dft_tables():
    two_pi = 2.0 * math.pi
    n = DFT_ROWS * DFT_COLS
    k1 = jnp.arange(DFT_ROWS)
    ang = ((k1[:, None] * k1[None, :]) % DFT_ROWS).astype(F32) * (two_pi / DFT_ROWS)
    f = jnp.concatenate([jnp.cos(ang), -jnp.sin(ang)], axis=0)
    f_wide = jnp.repeat(f, F1_T2_BLOCK, axis=1)
    same_t2 = (jnp.arange(DFT_ROWS * F1_T2_BLOCK)[None, None, :] % F1_T2_BLOCK
               == jnp.arange(F1_T2_BLOCK)[None, :, None])
    f1 = jnp.where(same_t2, f_wide[:, None, :], 0.0).reshape(
        2 * DFT_ROWS * F1_T2_BLOCK, DFT_ROWS * F1_T2_BLOCK)
    k2 = jnp.arange(DFT_COLS)
    t2 = jnp.arange(DFT_COLS)
    a = ((k2[:, None] * t2[None, :]) % DFT_COLS).astype(F32) * (two_pi / DFT_COLS)
    b = ((k1[:, None] * t2[None, :]) % n).astype(F32) * (two_pi / n)
    ca, sa = jnp.cos(a)[None], jnp.sin(a)[None]
    cb, sb = jnp.cos(b)[:, None, :], jnp.sin(b)[:, None, :]
    gr = ca * cb - sa * sb
    gi = -(sa * cb + ca * sb)
    g = jnp.concatenate([jnp.concatenate([gr, -gi], axis=2),
                         jnp.concatenate([gi, gr], axis=2)], axis=1)
    c = jnp.arange(F_GROUP_DIM)
    ang = ((c[:, None] * c[None, :]) % F_GROUP_DIM).astype(F32) * (two_pi / F_GROUP_DIM)
    eye = jnp.eye(MXU_COLS_V7X // F_GROUP_DIM, dtype=F32)
    cc = jnp.kron(eye, jnp.cos(ang))
    ss = jnp.kron(eye, jnp.sin(ang))
    return f1.astype(BF16), g.astype(BF16), cc.astype(BF16), ss.astype(BF16)


def _fnet_kernel(f_ref, x_ref, g_ref, cc_ref, ss_ref, o_ref, a_ref, zr_ref, zi_ref):
    step = pl.program_id(1)

    @pl.when(step < F1_STEPS)
    def _stage1():
        per_store = BF16_SUBLANES // F1_T2_BLOCK
        base = step * F1_T2_STEP
        for s in range(F1_T2_STEP // BF16_SUBLANES):
            zs = []
            for j in range(per_store):
                lo = (s * per_store + j) * F1_T2_BLOCK
                x = x_ref[0, :, lo:lo + F1_T2_BLOCK, :]
                x = x.reshape(DFT_ROWS * F1_T2_BLOCK, F_WIDTH).astype(BF16)
                z = _dot(f_ref[...], x)
                zs.append(z.reshape(2, DFT_ROWS, F1_T2_BLOCK, F_WIDTH))
            t2 = pl.multiple_of(base + s * BF16_SUBLANES, BF16_SUBLANES)
            a_ref[:, :, pl.ds(t2, BF16_SUBLANES), :] = jnp.concatenate(zs, axis=2).astype(BF16)

    @pl.when(step >= F1_STEPS)
    def _stage2():
        k1 = (step - F1_STEPS) * F3_K1_BLOCK
        for j in range(F3_K1_BLOCK):
            a = jnp.concatenate([a_ref[0, k1 + j], a_ref[1, k1 + j]], axis=0)
            z = _dot(g_ref[j], a)
            zr_ref[j * DFT_COLS:(j + 1) * DFT_COLS] = z[:DFT_COLS].astype(BF16)
            zi_ref[j * DFT_COLS:(j + 1) * DFT_COLS] = z[DFT_COLS:].astype(BF16)
        scale = 1.0 / math.sqrt(DFT_ROWS * DFT_COLS * F_GROUP_DIM)
        for p in range(F_WIDTH // MXU_COLS_V7X):
            cols = slice(p * MXU_COLS_V7X, (p + 1) * MXU_COLS_V7X)
            y = _dot(zr_ref[:, cols], cc_ref[...]) + _dot(zi_ref[:, cols], ss_ref[...])
            y = (y * scale).astype(BF16)
            for j in range(F3_K1_BLOCK):
                o_ref[0, j, :, cols] = y[j * DFT_COLS:(j + 1) * DFT_COLS]


def _fourier(u, tables, batch, seq):
    f1, g, cc, ss = tables
    f1_rows = 2 * DFT_ROWS * F1_T2_BLOCK
    y = pl.pallas_call(
        _fnet_kernel,
        grid=(batch, F1_STEPS + F3_STEPS),
        in_specs=[_const_spec((f1_rows, f1_rows // 2)),
                  pl.BlockSpec((1, DFT_ROWS, F1_T2_STEP, F_WIDTH),
                               lambda b, i: (b, 0, jnp.minimum(i, F1_STEPS - 1), 0)),
                  pl.BlockSpec((F3_K1_BLOCK, 2 * DFT_COLS, 2 * DFT_COLS),
                               lambda b, i: (jnp.maximum(i - F1_STEPS, 0), 0, 0)),
                  _const_spec((MXU_COLS_V7X, MXU_COLS_V7X)), _const_spec((MXU_COLS_V7X, MXU_COLS_V7X))],
        out_specs=pl.BlockSpec((1, F3_K1_BLOCK, DFT_COLS, F_WIDTH),
                               lambda b, i: (b, jnp.maximum(i - F1_STEPS, 0), 0, 0)),
        out_shape=jax.ShapeDtypeStruct((batch, DFT_ROWS, DFT_COLS, F_WIDTH), BF16),
        scratch_shapes=[pltpu.VMEM((2, DFT_ROWS, DFT_COLS, F_WIDTH), BF16),
                        pltpu.VMEM((F3_K1_BLOCK * DFT_COLS, F_WIDTH), BF16),
                        pltpu.VMEM((F3_K1_BLOCK * DFT_COLS, F_WIDTH), BF16)],
        compiler_params=_params(52),
        name="fnet",
    )(f1, u.reshape(batch, DFT_ROWS, DFT_COLS, F_WIDTH), g, cc, ss)
    return jnp.transpose(y, (0, 2, 1, 3)).reshape(batch * seq, F_WIDTH)


def _mixout_kernel(x_ref, g_ref, att_ref, fou_ref, wg_ref, gb_ref, wna_ref, wf_ref, wo_ref,
                   *rest, n_cast):
    cast_src, o_ref, cast_dst = rest[:n_cast], rest[n_cast], rest[n_cast + 1:]
    _cast_blocks(cast_src, cast_dst)
    for t in range(TOKEN_TILE // SUB_TILE):
        rows = slice(t * SUB_TILE, (t + 1) * SUB_TILE)
        x = x_ref[rows, :]
        h = _rms(x, g_ref[...]).astype(BF16)
        gates = jax.nn.sigmoid(_dot(h, wg_ref[...]) + gb_ref[...])
        y_na = _dot(att_ref[rows, :], wna_ref[...])
        y_f = _dot(fou_ref[rows, :], wf_ref[...])
        m = gates[:, :D_MODEL] * y_na + gates[:, D_MODEL:] * y_f
        o_ref[rows, :] = x + _dot(m.astype(BF16), wo_ref[...])


def _mixout(x, layer, g, att, fou, w_in, gate_bias, w_na, w_f, w_o, cast=()):
    n = x.shape[0]
    steps = n // TOKEN_TILE
    tile = pl.BlockSpec((TOKEN_TILE, D_MODEL), lambda i: (i, 0))
    half = pl.BlockSpec((TOKEN_TILE, NA_WIDTH), lambda i: (i, 0))
    assert w_in.shape[1] == 4 * D_MODEL
    cast_in, cast_out, cast_shapes = _cast_specs(cast, steps)
    y, *copies = pl.pallas_call(
        functools.partial(_mixout_kernel, n_cast=len(cast)),
        grid=(steps,),
        in_specs=[tile, _layer_spec((1, D_MODEL), layer), half, half,
                  _const_spec((D_MODEL, 2 * D_MODEL), 1),
                  _layer_spec((1, 2 * D_MODEL), layer),
                  _const_spec((NA_WIDTH, D_MODEL)), _const_spec((F_WIDTH, D_MODEL)),
                  _const_spec((D_MODEL, D_MODEL))] + cast_in,
        out_specs=[tile] + cast_out,
        out_shape=[jax.ShapeDtypeStruct((n, D_MODEL), F32)] + cast_shapes,
        compiler_params=_params(48),
        name="mix_out",
    )(x, g, att, fou, w_in, gate_bias, w_na, w_f, w_o, *[arr for arr, _ in cast])
    return y, copies


def kernel(x, ffn1_norm, ffn1_w_in, ffn1_w_out, mix_norm, mix_w_in, mix_gate_bias, na_rpb,
           na_w_out, f_w_out, mix_w_o, ffn2_norm, ffn2_w_in, ffn2_w_out, final_norm):
    batch, seq, d = x.shape
    depth = ffn1_norm.shape[0]
    assert d == D_MODEL and seq == DFT_ROWS * DFT_COLS and seq % (GRID_W * NA_ROW_BLOCK) == 0
    assert (batch * seq) % TOKEN_TILE == 0 and (batch * seq) % FFN_TOKEN_TILE == 0
    tables = _dft_tables()
    na_bias = _na_bias_tables(na_rpb)
    gain = lambda g: g.reshape(depth, 1, D_MODEL)
    g1, g2, mix_g = gain(ffn1_norm), gain(ffn2_norm), gain(mix_norm)
    gate_bias = mix_gate_bias.reshape(depth, 1, 2 * D_MODEL)
    xs = x.reshape(batch * seq, d)
    ffn1_w = [ffn1_w_in[0].astype(BF16), ffn1_w_out[0].astype(BF16)]
    for l in range(depth):
        xs, (w_in, w_na, w_f, w_o) = _ffn(
            xs, l, g1, *ffn1_w,
            cast=[(mix_w_in, l), (na_w_out, l), (f_w_out, l), (mix_w_o, l)])
        q, k, v, u = _proj(xs, l, mix_g, w_in)
        att = _na(q, k, v, l, na_bias, batch, seq)
        fou = _fourier(u, tables, batch, seq)
        xs, ffn2_w = _mixout(xs, l, mix_g, att, fou, w_in, gate_bias, w_na, w_f, w_o,
                             cast=[(ffn2_w_in, l), (ffn2_w_out, l)])
        last = l == depth - 1
        xs, ffn1_w = _ffn(xs, l, g2, *ffn2_w, final_g=final_norm if last else None,
                          cast=[] if last else [(ffn1_w_in, l + 1), (ffn1_w_out, l + 1)])
    return xs.reshape(batch, seq, d)
```

```python
import functools
import math

import jax
import jax.numpy as jnp
from jax import lax
from jax.experimental import pallas as pl
from jax.experimental.pallas import tpu as pltpu

D_MODEL = 1024
GRID_W = 64
NA_HEADS = 8
NA_HEAD_DIM = 64
NA_WIDTH = NA_HEADS * NA_HEAD_DIM
NA_ROWS = 8
NA_COLS = 16
F_GROUPS = 4
F_GROUP_DIM = 128
F_WIDTH = F_GROUPS * F_GROUP_DIM
D_FF = 2816
RMS_EPS = 1e-6

BF16 = jnp.bfloat16
F32 = jnp.float32

MXU_COLS_V7X = 256
VMEM_BYTES_V7X = 64 * 1024 * 1024

TOKEN_TILE = 1024
SUB_TILE = 512
FFN_TOKEN_TILE = 1024
FFN_TOKEN_TILE_F32W = 512
FF_CHUNK = MXU_COLS_V7X
NA_ROW_BLOCK = 16
DFT_ROWS = 64
DFT_COLS = 128
SUBLANES_F32 = 8
BF16_SUBLANES = 16
F1_T2_BLOCK = SUBLANES_F32
F1_T2_STEP = 32
F3_K1_BLOCK = 16
F1_STEPS = DFT_COLS // F1_T2_STEP
F3_STEPS = DFT_ROWS // F3_K1_BLOCK


def _params(vmem_mib):
    assert vmem_mib * 1024 * 1024 <= VMEM_BYTES_V7X
    return pltpu.CompilerParams(
        dimension_semantics=None,
        vmem_limit_bytes=vmem_mib * 1024 * 1024,
    )


def _const_spec(shape, col_block=0):
    index = (0,) * (len(shape) - 1) + (col_block,)
    return pl.BlockSpec(shape, lambda *_: index, pipeline_mode=pl.Buffered(1))


def _cast_specs(params, steps):
    in_specs, out_specs, out_shapes = [], [], []
    for arr, layer in params:
        _, rows, cols = arr.shape
        nblk = steps
        while rows % nblk or (rows // nblk) % BF16_SUBLANES:
            nblk //= 2
        blk = rows // nblk
        in_specs.append(pl.BlockSpec(
            (None, blk, cols), lambda i, layer=layer, nblk=nblk: (layer, jnp.minimum(i, nblk - 1), 0)))
        out_specs.append(pl.BlockSpec(
            (blk, cols), lambda i, nblk=nblk: (jnp.minimum(i, nblk - 1), 0)))
        out_shapes.append(jax.ShapeDtypeStruct((rows, cols), BF16))
    return in_specs, out_specs, out_shapes


def _cast_blocks(src_refs, dst_refs):
    for src, dst in zip(src_refs, dst_refs):
        dst[...] = src[...].astype(BF16)


def _layer_spec(shape, layer):
    index = (layer,) + (0,) * len(shape)
    return pl.BlockSpec((None,) + tuple(shape), lambda *_: index, pipeline_mode=pl.Buffered(1))


def _rms(x, g):
    ms = jnp.mean(x * x, axis=-1, keepdims=True)
    return x * lax.rsqrt(ms + RMS_EPS) * g


def _dot(a, b):
    return jnp.dot(a, b, preferred_element_type=F32)


def _ffn_kernel(x_ref, g_ref, win_ref, wout_ref, *rest, final, n_cast):
    rest = list(rest)
    gf_ref = rest.pop(0) if final else None
    cast_src, o_ref, cast_dst, act_ref = (rest[:n_cast], rest[n_cast],
                                          rest[n_cast + 1:2 * n_cast + 1], rest[-1])
    _cast_blocks(cast_src, cast_dst)
    x = x_ref[...]
    h = _rms(x, g_ref[...]).astype(BF16)
    for c in range(D_FF // FF_CHUNK):
        lo = c * FF_CHUNK
        g = _dot(h, win_ref[:, lo:lo + FF_CHUNK].astype(BF16))
        u = _dot(h, win_ref[:, D_FF + lo:D_FF + lo + FF_CHUNK].astype(BF16))
        act_ref[:, lo:lo + FF_CHUNK] = (g * jax.nn.sigmoid(g) * u).astype(BF16)
    y = x + 0.5 * _dot(act_ref[...], wout_ref[...].astype(BF16))
    if final:
        y = _rms(y, gf_ref[...])
    o_ref[...] = y


def _ffn(x, layer, g, w_in, w_out, final_g=None, cast=()):
    n = x.shape[0]
    f32_weights = w_in.ndim == 3
    rows = FFN_TOKEN_TILE_F32W if f32_weights else FFN_TOKEN_TILE
    steps = n // rows
    final = final_g is not None
    tile = pl.BlockSpec((rows, D_MODEL), lambda i: (i, 0))
    w_spec = functools.partial(_layer_spec, layer=layer) if f32_weights else _const_spec
    in_specs = [tile, _layer_spec((1, D_MODEL), layer), w_spec((D_MODEL, 2 * D_FF)),
                w_spec((D_FF, D_MODEL))]
    args = [x, g, w_in, w_out]
    if final:
        in_specs.append(_const_spec((1, D_MODEL)))
        args.append(final_g.reshape(1, D_MODEL))
    cast_in, cast_out, cast_shapes = _cast_specs(cast, steps)
    y, *copies = pl.pallas_call(
        functools.partial(_ffn_kernel, final=final, n_cast=len(cast)),
        grid=(steps,),
        in_specs=in_specs + cast_in,
        out_specs=[tile] + cast_out,
        out_shape=[jax.ShapeDtypeStruct((n, D_MODEL), F32)] + cast_shapes,
        scratch_shapes=[pltpu.VMEM((rows, D_FF), BF16)],
        compiler_params=_params(58),
        name="ffn",
    )(*args, *[arr for arr, _ in cast])
    return y, copies


def _proj_kernel(x_ref, g_ref, w_ref, q_ref, k_ref, v_ref, u_ref):
    s = NA_WIDTH
    for t in range(TOKEN_TILE // SUB_TILE):
        rows = slice(t * SUB_TILE, (t + 1) * SUB_TILE)
        h = _rms(x_ref[rows, :], g_ref[...]).astype(BF16)
        z = _dot(h, w_ref[...])
        q_ref[rows, :] = (z[:, :s] * (NA_HEAD_DIM ** -0.5)).astype(BF16)
        k_ref[rows, :] = z[:, s:2 * s].astype(BF16)
        v_ref[rows, :] = z[:, 2 * s:3 * s].astype(BF16)
        u_ref[rows, :] = z[:, 3 * s:3 * s + F_WIDTH]


def _proj(x, layer, g, w):
    n = x.shape[0]
    cols = 3 * NA_WIDTH + F_WIDTH
    assert w.shape[1] == 2 * cols
    out_tile = pl.BlockSpec((TOKEN_TILE, NA_WIDTH), lambda i: (i, 0))
    out_sds = jax.ShapeDtypeStruct((n, NA_WIDTH), BF16)
    return pl.pallas_call(
        _proj_kernel,
        grid=(n // TOKEN_TILE,),
        in_specs=[pl.BlockSpec((TOKEN_TILE, D_MODEL), lambda i: (i, 0)),
                  _layer_spec((1, D_MODEL), layer), _const_spec((D_MODEL, cols), 0)],
        out_specs=[out_tile] * 4,
        out_shape=[out_sds] * 3 + [jax.ShapeDtypeStruct((n, F_WIDTH), F32)],
        compiler_params=_params(40),
        name="mix_proj",
    )(x, g, w)


def _na_bias_tables(rpb):
    ncol = 2 * NA_COLS - 1
    qc = jnp.arange(GRID_W)
    lane = jnp.arange(2 * GRID_W)
    kc, side = lane % GRID_W, lane // GRID_W
    ws = jnp.clip(qc - NA_COLS // 2, 0, GRID_W - NA_COLS)
    ok = (kc[None, :] >= ws[:, None]) & (kc[None, :] < ws[:, None] + NA_COLS)
    dc = kc[None, :] - qc[:, None] + (NA_COLS - 1)
    src = jnp.where(ok, side[None, :] * ncol + dc, -1)
    pick = (src[:, :, None] == jnp.arange(2 * ncol)[None, None, :]).astype(F32)
    both = jnp.concatenate([rpb[:, :, :-1], rpb[:, :, 1:]], axis=-1).astype(F32)
    table = jnp.einsum('lhdm,qkm->lhdqk', both, pick, precision=lax.Precision.HIGHEST)
    return jnp.where(ok[None, None, None], table, -jnp.inf)


def _na_kernel(q_ref, k_ref, v_ref, bias_ref, hm_ref, o_ref, *, rows):
    blk = pl.program_id(1)
    kh = NA_ROWS
    half = kh // 2
    lane_lo = lax.broadcasted_iota(jnp.int32, (GRID_W, 128), 1) < NA_HEAD_DIM

    def window(i):
        r = blk * NA_ROW_BLOCK + i
        rs = jnp.clip(r - half, 0, rows - kh)
        dr0 = rs - r + (NA_ROWS - 1)
        return pl.multiple_of(rs * GRID_W, GRID_W), dr0

    def bias(h, dr0):
        return jnp.concatenate([bias_ref[h, dr0 + 2 * j] for j in range(kh // 2)], axis=-1)

    def score_stage(i):
        koff, _ = window(i)
        ss = []
        for p in range(NA_HEADS // 2):
            lanes = slice(p * 128, (p + 1) * 128)
            qp = q_ref[i * GRID_W:(i + 1) * GRID_W, lanes]
            ql = jnp.concatenate([qp * hm_ref[0], qp * hm_ref[1]], axis=0)
            kw = k_ref[0, pl.ds(koff, kh * GRID_W), lanes]
            ss.append(lax.dot_general(ql, kw, (((1,), (1,)), ((), ())),
                                      preferred_element_type=F32))
        return ss

    def output_stage(i, ss):
        koff, dr0 = window(i)
        es, ls = [], []
        for p in range(NA_HEADS // 2):
            s = ss[p] + jnp.concatenate([bias(2 * p, dr0), bias(2 * p + 1, dr0)], axis=0)
            m = jnp.max(s, axis=-1, keepdims=True)
            e = jnp.exp(s - m)
            ls.append(jnp.sum(e, axis=-1, keepdims=True))
            es.append(e.astype(BF16))
        for p in range(NA_HEADS // 2):
            lanes = slice(p * 128, (p + 1) * 128)
            vw = v_ref[0, pl.ds(koff, kh * GRID_W), lanes]
            pv = _dot(es[p], vw) / ls[p]
            o = jnp.where(lane_lo, pv[:GRID_W], pv[GRID_W:])
            o_ref[i * GRID_W:(i + 1) * GRID_W, lanes] = o.astype(BF16)

    ss = score_stage(0)
    for i in range(NA_ROW_BLOCK):
        nxt = score_stage(i + 1) if i + 1 < NA_ROW_BLOCK else None
        output_stage(i, ss)
        ss = nxt


def _na(q, k, v, layer, bias, batch, seq):
    rows = seq // GRID_W
    assert rows >= NA_ROWS and NA_ROWS % 2 == 0
    blk_tokens = NA_ROW_BLOCK * GRID_W
    nblk = rows // NA_ROW_BLOCK
    head_mask = (jnp.arange(128)[None, None, :] // NA_HEAD_DIM
                 == jnp.arange(2)[:, None, None]).astype(BF16)
    head_mask = jnp.broadcast_to(head_mask, (2, GRID_W, 128))
    q_tile = pl.BlockSpec((blk_tokens, NA_WIDTH), lambda b, i: (b * nblk + i, 0))
    kv_spec = pl.BlockSpec((1, seq, NA_WIDTH), lambda b, i: (b, 0, 0))
    return pl.pallas_call(
        functools.partial(_na_kernel, rows=rows),
        grid=(batch, nblk),
        in_specs=[q_tile, kv_spec, kv_spec,
                  _layer_spec((NA_HEADS, 2 * NA_ROWS - 2, GRID_W, 2 * GRID_W), layer),
                  _const_spec((2, GRID_W, 128))],
        out_specs=q_tile,
        out_shape=jax.ShapeDtypeStruct((batch * seq, NA_WIDTH), BF16),
        compiler_params=_params(56),
        name="na",
    )(q, k.reshape(batch, seq, NA_WIDTH), v.reshape(batch, seq, NA_WIDTH), bias, head_mask)


def _dft_tables():
    two_pi = 2.0 * math.pi
    n = DFT_ROWS * DFT_COLS
    k1 = jnp.arange(DFT_ROWS)
    ang = ((k1[:, None] * k1[None, :]) % DFT_ROWS).astype(F32) * (two_pi / DFT_ROWS)
    f = jnp.concatenate([jnp.cos(ang), -jnp.sin(ang)], axis=0)
    f_wide = jnp.repeat(f, F1_T2_BLOCK, axis=1)
    same_t2 = (jnp.arange(DFT_ROWS * F1_T2_BLOCK)[None, None, :] % F1_T2_BLOCK
               == jnp.arange(F1_T2_BLOCK)[None, :, None])
    f1 = jnp.where(same_t2, f_wide[:, None, :], 0.0).reshape(
        2 * DFT_ROWS * F1_T2_BLOCK, DFT_ROWS * F1_T2_BLOCK)
    k2 = jnp.arange(DFT_COLS)
    t2 = jnp.arange(DFT_COLS)
    a = ((k2[:, None] * t2[None, :]) % DFT_COLS).astype(F32) * (two_pi / DFT_COLS)
    b = ((k1[:, None] * t2[None, :]) % n).astype(F32) * (two_pi / n)
    ca, sa = jnp.cos(a)[None], jnp.sin(a)[None]
    cb, sb = jnp.cos(b)[:, None, :], jnp.sin(b)[:, None, :]
    gr = ca * cb - sa * sb
    gi = -(sa * cb + ca * sb)
    g = jnp.concatenate([jnp.concatenate([gr, -gi], axis=2),
                         jnp.concatenate([gi, gr], axis=2)], axis=1)
    c = jnp.arange(F_GROUP_DIM)
    ang = ((c[:, None] * c[None, :]) % F_GROUP_DIM).astype(F32) * (two_pi / F_GROUP_DIM)
    eye = jnp.eye(MXU_COLS_V7X // F_GROUP_DIM, dtype=F32)
    cc = jnp.kron(eye, jnp.cos(ang))
    ss = jnp.kron(eye, jnp.sin(ang))
    return f1.astype(BF16), g.astype(BF16), cc.astype(BF16), ss.astype(BF16)


def _fnet_kernel(f_ref, x_ref, g_ref, cc_ref, ss_ref, o_ref, a_ref, zr_ref, zi_ref):
    step = pl.program_id(1)

    @pl.when(step < F1_STEPS)
    def _stage1():
        per_store = BF16_SUBLANES // F1_T2_BLOCK
        base = step * F1_T2_STEP
        for s in range(F1_T2_STEP // BF16_SUBLANES):
            zs = []
            for j in range(per_store):
                lo = (s * per_store + j) * F1_T2_BLOCK
                x = x_ref[0, :, lo:lo + F1_T2_BLOCK, :]
                x = x.reshape(DFT_ROWS * F1_T2_BLOCK, F_WIDTH).astype(BF16)
                z = _dot(f_ref[...], x)
                zs.append(z.reshape(2, DFT_ROWS, F1_T2_BLOCK, F_WIDTH))
            t2 = pl.multiple_of(base + s * BF16_SUBLANES, BF16_SUBLANES)
            a_ref[:, :, pl.ds(t2, BF16_SUBLANES), :] = jnp.concatenate(zs, axis=2).astype(BF16)

    @pl.when(step >= F1_STEPS)
    def _stage2():
        k1 = (step - F1_STEPS) * F3_K1_BLOCK
        for j in range(F3_K1_BLOCK):
            a = jnp.concatenate([a_ref[0, k1 + j], a_ref[1, k1 + j]], axis=0)
            z = _dot(g_ref[j], a)
            zr_ref[j * DFT_COLS:(j + 1) * DFT_COLS] = z[:DFT_COLS].astype(BF16)
            zi_ref[j * DFT_COLS:(j + 1) * DFT_COLS] = z[DFT_COLS:].astype(BF16)
        scale = 1.0 / math.sqrt(DFT_ROWS * DFT_COLS * F_GROUP_DIM)
        for p in range(F_WIDTH // MXU_COLS_V7X):
            cols = slice(p * MXU_COLS_V7X, (p + 1) * MXU_COLS_V7X)
            y = _dot(zr_ref[:, cols], cc_ref[...]) + _dot(zi_ref[:, cols], ss_ref[...])
            y = (y * scale).astype(BF16)
            for j in range(F3_K1_BLOCK):
                o_ref[0, j, :, cols] = y[j * DFT_COLS:(j + 1) * DFT_COLS]


def _fourier(u, tables, batch, seq):
    f1, g, cc, ss = tables
    f1_rows = 2 * DFT_ROWS * F1_T2_BLOCK
    y = pl.pallas_call(
        _fnet_kernel,
        grid=(batch, F1_STEPS + F3_STEPS),
        in_specs=[_const_spec((f1_rows, f1_rows // 2)),
                  pl.BlockSpec((1, DFT_ROWS, F1_T2_STEP, F_WIDTH),
                               lambda b, i: (b, 0, jnp.minimum(i, F1_STEPS - 1), 0)),
                  pl.BlockSpec((F3_K1_BLOCK, 2 * DFT_COLS, 2 * DFT_COLS),
                               lambda b, i: (jnp.maximum(i - F1_STEPS, 0), 0, 0)),
                  _const_spec((MXU_COLS_V7X, MXU_COLS_V7X)), _const_spec((MXU_COLS_V7X, MXU_COLS_V7X))],
        out_specs=pl.BlockSpec((1, F3_K1_BLOCK, DFT_COLS, F_WIDTH),
                               lambda b, i: (b, jnp.maximum(i - F1_STEPS, 0), 0, 0)),
        out_shape=jax.ShapeDtypeStruct((batch, DFT_ROWS, DFT_COLS, F_WIDTH), BF16),
        scratch_shapes=[pltpu.VMEM((2, DFT_ROWS, DFT_COLS, F_WIDTH), BF16),
                        pltpu.VMEM((F3_K1_BLOCK * DFT_COLS, F_WIDTH), BF16),
                        pltpu.VMEM((F3_K1_BLOCK * DFT_COLS, F_WIDTH), BF16)],
        compiler_params=_params(52),
        name="fnet",
    )(f1, u.reshape(batch, DFT_ROWS, DFT_COLS, F_WIDTH), g, cc, ss)
    return jnp.transpose(y, (0, 2, 1, 3)).reshape(batch * seq, F_WIDTH)


def _mixout_kernel(x_ref, g_ref, att_ref, fou_ref, wg_ref, gb_ref, wna_ref, wf_ref, wo_ref,
                   *rest, n_cast):
    cast_src, o_ref, cast_dst = rest[:n_cast], rest[n_cast], rest[n_cast + 1:]
    _cast_blocks(cast_src, cast_dst)
    for t in range(TOKEN_TILE // SUB_TILE):
        rows = slice(t * SUB_TILE, (t + 1) * SUB_TILE)
        x = x_ref[rows, :]
        h = _rms(x, g_ref[...]).astype(BF16)
        gates = jax.nn.sigmoid(_dot(h, wg_ref[...]) + gb_ref[...])
        y_na = _dot(att_ref[rows, :], wna_ref[...])
        y_f = _dot(fou_ref[rows, :], wf_ref[...])
        m = gates[:, :D_MODEL] * y_na + gates[:, D_MODEL:] * y_f
        o_ref[rows, :] = x + _dot(m.astype(BF16), wo_ref[...])


def _mixout(x, layer, g, att, fou, w_in, gate_bias, w_na, w_f, w_o, cast=()):
    n = x.shape[0]
    steps = n // TOKEN_TILE
    tile = pl.BlockSpec((TOKEN_TILE, D_MODEL), lambda i: (i, 0))
    half = pl.BlockSpec((TOKEN_TILE, NA_WIDTH), lambda i: (i, 0))
    assert w_in.shape[1] == 4 * D_MODEL
    cast_in, cast_out, cast_shapes = _cast_specs(cast, steps)
    y, *copies = pl.pallas_call(
        functools.partial(_mixout_kernel, n_cast=len(cast)),
        grid=(steps,),
        in_specs=[tile, _layer_spec((1, D_MODEL), layer), half, half,
                  _const_spec((D_MODEL, 2 * D_MODEL), 1),
                  _layer_spec((1, 2 * D_MODEL), layer),
                  _const_spec((NA_WIDTH, D_MODEL)), _const_spec((F_WIDTH, D_MODEL)),
                  _const_spec((D_MODEL, D_MODEL))] + cast_in,
        out_specs=[tile] + cast_out,
        out_shape=[jax.ShapeDtypeStruct((n, D_MODEL), F32)] + cast_shapes,
        compiler_params=_params(48),
        name="mix_out",
    )(x, g, att, fou, w_in, gate_bias, w_na, w_f, w_o, *[arr for arr, _ in cast])
    return y, copies


def kernel(x, ffn1_norm, ffn1_w_in, ffn1_w_out, mix_norm, mix_w_in, mix_gate_bias, na_rpb,
           na_w_out, f_w_out, mix_w_o, ffn2_norm, ffn2_w_in, ffn2_w_out, final_norm):
    batch, seq, d = x.shape
    depth = ffn1_norm.shape[0]
    assert d == D_MODEL and seq == DFT_ROWS * DFT_COLS and seq % (GRID_W * NA_ROW_BLOCK) == 0
    assert all((batch * seq) % t == 0 for t in (TOKEN_TILE, FFN_TOKEN_TILE, FFN_TOKEN_TILE_F32W))
    tables = _dft_tables()
    na_bias = _na_bias_tables(na_rpb)
    gain = lambda g: g.reshape(depth, 1, D_MODEL)
    g1, g2, mix_g = gain(ffn1_norm), gain(ffn2_norm), gain(mix_norm)
    gate_bias = mix_gate_bias.reshape(depth, 1, 2 * D_MODEL)
    xs = x.reshape(batch * seq, d)
    ffn1_w = [ffn1_w_in, ffn1_w_out]
    for l in range(depth):
        xs, (w_in, w_na, w_f, w_o) = _ffn(
            xs, l, g1, *ffn1_w,
            cast=[(mix_w_in, l), (na_w_out, l), (f_w_out, l), (mix_w_o, l)])
        q, k, v, u = _proj(xs, l, mix_g, w_in)
        att = _na(q, k, v, l, na_bias, batch, seq)
        fou = _fourier(u, tables, batch, seq)
        xs, ffn2_w = _mixout(xs, l, mix_g, att, fou, w_in, gate_bias, w_na, w_f, w_o,
                             cast=[(ffn2_w_in, l), (ffn2_w_out, l)])
        last = l == depth - 1
        xs, ffn1_w = _ffn(xs, l, g2, *ffn2_w, final_g=final_norm if last else None,
                          cast=[] if last else [(ffn1_w_in, l + 1), (ffn1_w_out, l + 1)])
    return xs.reshape(batch, seq, d)
```

```python
import functools
import math

import jax
import jax.numpy as jnp
import numpy as np
from jax import lax
from jax.experimental import pallas as pl
from jax.experimental.pallas import tpu as pltpu

D_MODEL = 1024
GRID_W = 64
NA_HEADS = 8
NA_HEAD_DIM = 64
NA_WIDTH = NA_HEADS * NA_HEAD_DIM
NA_ROWS = 8
NA_COLS = 16
F_GROUPS = 4
F_GROUP_DIM = 128
F_WIDTH = F_GROUPS * F_GROUP_DIM
D_FF = 2816
RMS_EPS = 1e-6

BF16 = jnp.bfloat16
F32 = jnp.float32

MXU_COLS_V7X = 256
VMEM_BYTES_V7X = 64 * 1024 * 1024

TOKEN_TILE = 1024
SUB_TILE = 512
FFN_TOKEN_TILE = 1024
FFN_TOKEN_TILE_F32W = 512
FF_CHUNK = MXU_COLS_V7X
NA_ROW_BLOCK = 16
DFT_ROWS = 64
DFT_COLS = 128
SUBLANES_F32 = 8
BF16_SUBLANES = 16
F1_T2_BLOCK = SUBLANES_F32
F1_T2_STEP = 32
F3_K1_BLOCK = 16
F1_STEPS = DFT_COLS // F1_T2_STEP
F3_STEPS = DFT_ROWS // F3_K1_BLOCK


def _params(vmem_mib):
    assert vmem_mib * 1024 * 1024 <= VMEM_BYTES_V7X
    return pltpu.CompilerParams(
        dimension_semantics=None,
        vmem_limit_bytes=vmem_mib * 1024 * 1024,
    )


def _const_spec(shape, col_block=0):
    index = (0,) * (len(shape) - 1) + (col_block,)
    return pl.BlockSpec(shape, lambda *_: index, pipeline_mode=pl.Buffered(1))


def _cast_specs(params, steps):
    in_specs, out_specs, out_shapes = [], [], []
    for arr, layer in params:
        _, rows, cols = arr.shape
        nblk = steps
        while rows % nblk or (rows // nblk) % BF16_SUBLANES:
            nblk //= 2
        blk = rows // nblk
        in_specs.append(pl.BlockSpec(
            (None, blk, cols), lambda i, layer=layer, nblk=nblk: (layer, jnp.minimum(i, nblk - 1), 0)))
        out_specs.append(pl.BlockSpec(
            (blk, cols), lambda i, nblk=nblk: (jnp.minimum(i, nblk - 1), 0)))
        out_shapes.append(jax.ShapeDtypeStruct((rows, cols), BF16))
    return in_specs, out_specs, out_shapes


def _cast_blocks(src_refs, dst_refs):
    for src, dst in zip(src_refs, dst_refs):
        dst[...] = src[...].astype(BF16)


def _layer_spec(shape, layer):
    index = (layer,) + (0,) * len(shape)
    return pl.BlockSpec((None,) + tuple(shape), lambda *_: index, pipeline_mode=pl.Buffered(1))


def _rms(x, g):
    ms = jnp.mean(x * x, axis=-1, keepdims=True)
    return x * lax.rsqrt(ms + RMS_EPS) * g


def _dot(a, b):
    return jnp.dot(a, b, preferred_element_type=F32)


def _ffn_kernel(x_ref, g_ref, win_ref, wout_ref, *rest, final, n_cast):
    rest = list(rest)
    gf_ref = rest.pop(0) if final else None
    cast_src, o_ref, cast_dst, act_ref = (rest[:n_cast], rest[n_cast],
                                          rest[n_cast + 1:2 * n_cast + 1], rest[-1])
    _cast_blocks(cast_src, cast_dst)
    x = x_ref[...]
    h = _rms(x, g_ref[...]).astype(BF16)
    for c in range(D_FF // FF_CHUNK):
        lo = c * FF_CHUNK
        g = _dot(h, win_ref[:, lo:lo + FF_CHUNK].astype(BF16))
        u = _dot(h, win_ref[:, D_FF + lo:D_FF + lo + FF_CHUNK].astype(BF16))
        act_ref[:, lo:lo + FF_CHUNK] = (g * jax.nn.sigmoid(g) * u).astype(BF16)
    y = x + 0.5 * _dot(act_ref[...], wout_ref[...].astype(BF16))
    if final:
        y = _rms(y, gf_ref[...])
    o_ref[...] = y


def _ffn(x, layer, g, w_in, w_out, final_g=None, cast=()):
    n = x.shape[0]
    f32_weights = w_in.ndim == 3
    rows = FFN_TOKEN_TILE_F32W if f32_weights else FFN_TOKEN_TILE
    steps = n // rows
    final = final_g is not None
    tile = pl.BlockSpec((rows, D_MODEL), lambda i: (i, 0))
    w_spec = functools.partial(_layer_spec, layer=layer) if f32_weights else _const_spec
    in_specs = [tile, _layer_spec((1, D_MODEL), layer), w_spec((D_MODEL, 2 * D_FF)),
                w_spec((D_FF, D_MODEL))]
    args = [x, g, w_in, w_out]
    if final:
        in_specs.append(_const_spec((1, D_MODEL)))
        args.append(final_g.reshape(1, D_MODEL))
    cast_in, cast_out, cast_shapes = _cast_specs(cast, steps)
    y, *copies = pl.pallas_call(
        functools.partial(_ffn_kernel, final=final, n_cast=len(cast)),
        grid=(steps,),
        in_specs=in_specs + cast_in,
        out_specs=[tile] + cast_out,
        out_shape=[jax.ShapeDtypeStruct((n, D_MODEL), F32)] + cast_shapes,
        scratch_shapes=[pltpu.VMEM((rows, D_FF), BF16)],
        compiler_params=_params(58),
        name="ffn",
    )(*args, *[arr for arr, _ in cast])
    return y, copies


def _proj_kernel(x_ref, g_ref, w_ref, q_ref, k_ref, v_ref, u_ref):
    s = NA_WIDTH
    for t in range(TOKEN_TILE // SUB_TILE):
        rows = slice(t * SUB_TILE, (t + 1) * SUB_TILE)
        h = _rms(x_ref[rows, :], g_ref[...]).astype(BF16)
        z = _dot(h, w_ref[...])
        q_ref[rows, :] = (z[:, :s] * (NA_HEAD_DIM ** -0.5)).astype(BF16)
        k_ref[rows, :] = z[:, s:2 * s].astype(BF16)
        v_ref[rows, :] = z[:, 2 * s:3 * s].astype(BF16)
        u_ref[rows, :] = z[:, 3 * s:3 * s + F_WIDTH]


def _proj(x, layer, g, w):
    n = x.shape[0]
    cols = 3 * NA_WIDTH + F_WIDTH
    assert w.shape[1] == 2 * cols
    out_tile = pl.BlockSpec((TOKEN_TILE, NA_WIDTH), lambda i: (i, 0))
    out_sds = jax.ShapeDtypeStruct((n, NA_WIDTH), BF16)
    return pl.pallas_call(
        _proj_kernel,
        grid=(n // TOKEN_TILE,),
        in_specs=[pl.BlockSpec((TOKEN_TILE, D_MODEL), lambda i: (i, 0)),
                  _layer_spec((1, D_MODEL), layer), _const_spec((D_MODEL, cols), 0)],
        out_specs=[out_tile] * 4,
        out_shape=[out_sds] * 3 + [jax.ShapeDtypeStruct((n, F_WIDTH), F32)],
        compiler_params=_params(40),
        name="mix_proj",
    )(x, g, w)


def _na_bias_tables(rpb):
    ncol = 2 * NA_COLS - 1
    qc = jnp.arange(GRID_W)
    lane = jnp.arange(2 * GRID_W)
    kc, side = lane % GRID_W, lane // GRID_W
    ws = jnp.clip(qc - NA_COLS // 2, 0, GRID_W - NA_COLS)
    ok = (kc[None, :] >= ws[:, None]) & (kc[None, :] < ws[:, None] + NA_COLS)
    dc = kc[None, :] - qc[:, None] + (NA_COLS - 1)
    src = jnp.where(ok, side[None, :] * ncol + dc, -1)
    pick = (src[:, :, None] == jnp.arange(2 * ncol)[None, None, :]).astype(F32)
    both = jnp.concatenate([rpb[:, :, :-1], rpb[:, :, 1:]], axis=-1).astype(F32)
    table = jnp.einsum('lhdm,qkm->lhdqk', both, pick, precision=lax.Precision.HIGHEST)
    return jnp.where(ok[None, None, None], table, -jnp.inf)


def _na_kernel(q_ref, k_ref, v_ref, bias_ref, hm_ref, o_ref, *, rows):
    blk = pl.program_id(1)
    kh = NA_ROWS
    half = kh // 2
    lane_lo = lax.broadcasted_iota(jnp.int32, (GRID_W, 128), 1) < NA_HEAD_DIM

    def window(i):
        r = blk * NA_ROW_BLOCK + i
        rs = jnp.clip(r - half, 0, rows - kh)
        dr0 = rs - r + (NA_ROWS - 1)
        return pl.multiple_of(rs * GRID_W, GRID_W), dr0

    def bias(h, dr0):
        return jnp.concatenate([bias_ref[h, dr0 + 2 * j] for j in range(kh // 2)], axis=-1)

    def score_stage(i):
        koff, _ = window(i)
        ss = []
        for p in range(NA_HEADS // 2):
            lanes = slice(p * 128, (p + 1) * 128)
            qp = q_ref[i * GRID_W:(i + 1) * GRID_W, lanes]
            ql = jnp.concatenate([qp * hm_ref[0], qp * hm_ref[1]], axis=0)
            kw = k_ref[0, pl.ds(koff, kh * GRID_W), lanes]
            ss.append(lax.dot_general(ql, kw, (((1,), (1,)), ((), ())),
                                      preferred_element_type=F32))
        return ss

    def output_stage(i, ss):
        koff, dr0 = window(i)
        es, ls = [], []
        for p in range(NA_HEADS // 2):
            s = ss[p] + jnp.concatenate([bias(2 * p, dr0), bias(2 * p + 1, dr0)], axis=0)
            m = jnp.max(s, axis=-1, keepdims=True)
            e = jnp.exp(s - m)
            ls.append(jnp.sum(e, axis=-1, keepdims=True))
            es.append(e.astype(BF16))
        for p in range(NA_HEADS // 2):
            lanes = slice(p * 128, (p + 1) * 128)
            vw = v_ref[0, pl.ds(koff, kh * GRID_W), lanes]
            pv = _dot(es[p], vw) / ls[p]
            o = jnp.where(lane_lo, pv[:GRID_W], pv[GRID_W:])
            o_ref[i * GRID_W:(i + 1) * GRID_W, lanes] = o.astype(BF16)

    ss = score_stage(0)
    for i in range(NA_ROW_BLOCK):
        nxt = score_stage(i + 1) if i + 1 < NA_ROW_BLOCK else None
        output_stage(i, ss)
        ss = nxt


def _na(q, k, v, layer, bias, batch, seq):
    rows = seq // GRID_W
    assert rows >= NA_ROWS and NA_ROWS % 2 == 0
    blk_tokens = NA_ROW_BLOCK * GRID_W
    nblk = rows // NA_ROW_BLOCK
    head_mask = (jnp.arange(128)[None, None, :] // NA_HEAD_DIM
                 == jnp.arange(2)[:, None, None]).astype(BF16)
    head_mask = jnp.broadcast_to(head_mask, (2, GRID_W, 128))
    q_tile = pl.BlockSpec((blk_tokens, NA_WIDTH), lambda b, i: (b * nblk + i, 0))
    kv_spec = pl.BlockSpec((1, seq, NA_WIDTH), lambda b, i: (b, 0, 0))
    return pl.pallas_call(
        functools.partial(_na_kernel, rows=rows),
        grid=(batch, nblk),
        in_specs=[q_tile, kv_spec, kv_spec,
                  _layer_spec((NA_HEADS, 2 * NA_ROWS - 2, GRID_W, 2 * GRID_W), layer),
                  _const_spec((2, GRID_W, 128))],
        out_specs=q_tile,
        out_shape=jax.ShapeDtypeStruct((batch * seq, NA_WIDTH), BF16),
        compiler_params=_params(56),
        name="na",
    )(q, k.reshape(batch, seq, NA_WIDTH), v.reshape(batch, seq, NA_WIDTH), bias, head_mask)


def _dft_tables():
    two_pi = 2.0 * math.pi
    n = DFT_ROWS * DFT_COLS
    k1 = np.arange(DFT_ROWS)
    ang = ((k1[:, None] * k1[None, :]) % DFT_ROWS) * (two_pi / DFT_ROWS)
    eye_t2 = np.eye(F1_T2_BLOCK)
    f1 = np.concatenate([np.kron(np.cos(ang), eye_t2), np.kron(-np.sin(ang), eye_t2)], axis=0)
    k2 = np.arange(DFT_COLS)
    t2 = np.arange(DFT_COLS)
    ang = ((t2[None, None, :] * (DFT_ROWS * k2[None, :, None] + k1[:, None, None])) % n) * (two_pi / n)
    gr, gi = np.cos(ang), -np.sin(ang)
    g = np.concatenate([np.concatenate([gr, -gi], axis=2),
                        np.concatenate([gi, gr], axis=2)], axis=1)
    c = np.arange(F_GROUP_DIM)
    ang = ((c[:, None] * c[None, :]) % F_GROUP_DIM) * (two_pi / F_GROUP_DIM)
    eye = np.eye(MXU_COLS_V7X // F_GROUP_DIM)
    cc = np.kron(eye, np.cos(ang))
    ss = np.kron(eye, np.sin(ang))
    return tuple(jnp.asarray(t.astype(np.float32)).astype(BF16) for t in (f1, g, cc, ss))


def _fnet_kernel(f_ref, x_ref, g_ref, cc_ref, ss_ref, o_ref, a_ref, zr_ref, zi_ref):
    step = pl.program_id(1)

    @pl.when(step < F1_STEPS)
    def _stage1():
        per_store = BF16_SUBLANES // F1_T2_BLOCK
        base = step * F1_T2_STEP
        for s in range(F1_T2_STEP // BF16_SUBLANES):
            zs = []
            for j in range(per_store):
                lo = (s * per_store + j) * F1_T2_BLOCK
                x = x_ref[0, :, lo:lo + F1_T2_BLOCK, :]
                x = x.reshape(DFT_ROWS * F1_T2_BLOCK, F_WIDTH).astype(BF16)
                z = _dot(f_ref[...], x)
                zs.append(z.reshape(2, DFT_ROWS, F1_T2_BLOCK, F_WIDTH))
            t2 = pl.multiple_of(base + s * BF16_SUBLANES, BF16_SUBLANES)
            a_ref[:, :, pl.ds(t2, BF16_SUBLANES), :] = jnp.concatenate(zs, axis=2).astype(BF16)

    @pl.when(step >= F1_STEPS)
    def _stage2():
        k1 = (step - F1_STEPS) * F3_K1_BLOCK
        for j in range(F3_K1_BLOCK):
            a = jnp.concatenate([a_ref[0, k1 + j], a_ref[1, k1 + j]], axis=0)
            z = _dot(g_ref[j], a)
            zr_ref[j * DFT_COLS:(j + 1) * DFT_COLS] = z[:DFT_COLS].astype(BF16)
            zi_ref[j * DFT_COLS:(j + 1) * DFT_COLS] = z[DFT_COLS:].astype(BF16)
        scale = 1.0 / math.sqrt(DFT_ROWS * DFT_COLS * F_GROUP_DIM)
        for p in range(F_WIDTH // MXU_COLS_V7X):
            cols = slice(p * MXU_COLS_V7X, (p + 1) * MXU_COLS_V7X)
            y = _dot(zr_ref[:, cols], cc_ref[...]) + _dot(zi_ref[:, cols], ss_ref[...])
            y = (y * scale).astype(BF16)
            for j in range(F3_K1_BLOCK):
                o_ref[0, j, :, cols] = y[j * DFT_COLS:(j + 1) * DFT_COLS]


def _fourier(u, tables, batch, seq):
    f1, g, cc, ss = tables
    f1_rows = 2 * DFT_ROWS * F1_T2_BLOCK
    y = pl.pallas_call(
        _fnet_kernel,
        grid=(batch, F1_STEPS + F3_STEPS),
        in_specs=[_const_spec((f1_rows, f1_rows // 2)),
                  pl.BlockSpec((1, DFT_ROWS, F1_T2_STEP, F_WIDTH),
                               lambda b, i: (b, 0, jnp.minimum(i, F1_STEPS - 1), 0)),
                  pl.BlockSpec((F3_K1_BLOCK, 2 * DFT_COLS, 2 * DFT_COLS),
                               lambda b, i: (jnp.maximum(i - F1_STEPS, 0), 0, 0)),
                  _const_spec((MXU_COLS_V7X, MXU_COLS_V7X)), _const_spec((MXU_COLS_V7X, MXU_COLS_V7X))],
        out_specs=pl.BlockSpec((1, F3_K1_BLOCK, DFT_COLS, F_WIDTH),
                               lambda b, i: (b, jnp.maximum(i - F1_STEPS, 0), 0, 0)),
        out_shape=jax.ShapeDtypeStruct((batch, DFT_ROWS, DFT_COLS, F_WIDTH), BF16),
        scratch_shapes=[pltpu.VMEM((2, DFT_ROWS, DFT_COLS, F_WIDTH), BF16),
                        pltpu.VMEM((F3_K1_BLOCK * DFT_COLS, F_WIDTH), BF16),
                        pltpu.VMEM((F3_K1_BLOCK * DFT_COLS, F_WIDTH), BF16)],
        compiler_params=_params(52),
        name="fnet",
    )(f1, u.reshape(batch, DFT_ROWS, DFT_COLS, F_WIDTH), g, cc, ss)
    return jnp.transpose(y, (0, 2, 1, 3)).reshape(batch * seq, F_WIDTH)


def _mixout_kernel(x_ref, g_ref, att_ref, fou_ref, wg_ref, gb_ref, wna_ref, wf_ref, wo_ref,
                   *rest, n_cast):
    cast_src, o_ref, cast_dst = rest[:n_cast], rest[n_cast], rest[n_cast + 1:]
    _cast_blocks(cast_src, cast_dst)
    for t in range(TOKEN_TILE // SUB_TILE):
        rows = slice(t * SUB_TILE, (t + 1) * SUB_TILE)
        x = x_ref[rows, :]
        h = _rms(x, g_ref[...]).astype(BF16)
        gates = jax.nn.sigmoid(_dot(h, wg_ref[...]) + gb_ref[...])
        y_na = _dot(att_ref[rows, :], wna_ref[...])
        y_f = _dot(fou_ref[rows, :], wf_ref[...])
        m = gates[:, :D_MODEL] * y_na + gates[:, D_MODEL:] * y_f
        o_ref[rows, :] = x + _dot(m.astype(BF16), wo_ref[...])


def _mixout(x, layer, g, att, fou, w_in, gate_bias, w_na, w_f, w_o, cast=()):
    n = x.shape[0]
    steps = n // TOKEN_TILE
    tile = pl.BlockSpec((TOKEN_TILE, D_MODEL), lambda i: (i, 0))
    half = pl.BlockSpec((TOKEN_TILE, NA_WIDTH), lambda i: (i, 0))
    assert w_in.shape[1] == 4 * D_MODEL
    cast_in, cast_out, cast_shapes = _cast_specs(cast, steps)
    y, *copies = pl.pallas_call(
        functools.partial(_mixout_kernel, n_cast=len(cast)),
        grid=(steps,),
        in_specs=[tile, _layer_spec((1, D_MODEL), layer), half, half,
                  _const_spec((D_MODEL, 2 * D_MODEL), 1),
                  _layer_spec((1, 2 * D_MODEL), layer),
                  _const_spec((NA_WIDTH, D_MODEL)), _const_spec((F_WIDTH, D_MODEL)),
                  _const_spec((D_MODEL, D_MODEL))] + cast_in,
        out_specs=[tile] + cast_out,
        out_shape=[jax.ShapeDtypeStruct((n, D_MODEL), F32)] + cast_shapes,
        compiler_params=_params(48),
        name="mix_out",
    )(x, g, att, fou, w_in, gate_bias, w_na, w_f, w_o, *[arr for arr, _ in cast])
    return y, copies


def kernel(x, ffn1_norm, ffn1_w_in, ffn1_w_out, mix_norm, mix_w_in, mix_gate_bias, na_rpb,
           na_w_out, f_w_out, mix_w_o, ffn2_norm, ffn2_w_in, ffn2_w_out, final_norm):
    batch, seq, d = x.shape
    depth = ffn1_norm.shape[0]
    assert d == D_MODEL and seq == DFT_ROWS * DFT_COLS and seq % (GRID_W * NA_ROW_BLOCK) == 0
    assert all((batch * seq) % t == 0 for t in (TOKEN_TILE, FFN_TOKEN_TILE, FFN_TOKEN_TILE_F32W))
    tables = _dft_tables()
    na_bias = _na_bias_tables(na_rpb)
    gain = lambda g: g.reshape(depth, 1, D_MODEL)
    g1, g2, mix_g = gain(ffn1_norm), gain(ffn2_norm), gain(mix_norm)
    gate_bias = mix_gate_bias.reshape(depth, 1, 2 * D_MODEL)
    xs = x.reshape(batch * seq, d)
    ffn1_w = [ffn1_w_in, ffn1_w_out]
    for l in range(depth):
        xs, (w_in, w_na, w_f, w_o) = _ffn(
            xs, l, g1, *ffn1_w,
            cast=[(mix_w_in, l), (na_w_out, l), (f_w_out, l), (mix_w_o, l)])
        q, k, v, u = _proj(xs, l, mix_g, w_in)
        att = _na(q, k, v, l, na_bias, batch, seq)
        fou = _fourier(u, tables, batch, seq)
        xs, ffn2_w = _mixout(xs, l, mix_g, att, fou, w_in, gate_bias, w_na, w_f, w_o,
                             cast=[(ffn2_w_in, l), (ffn2_w_out, l)])
        last = l == depth - 1
        xs, ffn1_w = _ffn(xs, l, g2, *ffn2_w, final_g=final_norm if last else None,
                          cast=[] if last else [(ffn1_w_in, l + 1), (ffn1_w_out, l + 1)])
    return xs.reshape(batch, seq, d)
```

```python
import functools
import math

import jax
import jax.numpy as jnp
import numpy as np
from jax import lax
from jax.experimental import pallas as pl
from jax.experimental.pallas import tpu as pltpu

D_MODEL = 1024
GRID_W = 64
NA_HEADS = 8
NA_HEAD_DIM = 64
NA_WIDTH = NA_HEADS * NA_HEAD_DIM
NA_ROWS = 8
NA_COLS = 16
F_GROUPS = 4
F_GROUP_DIM = 128
F_WIDTH = F_GROUPS * F_GROUP_DIM
D_FF = 2816
RMS_EPS = 1e-6

BF16 = jnp.bfloat16
F32 = jnp.float32

MXU_COLS_V7X = 256
VMEM_BYTES_V7X = 64 * 1024 * 1024

PROJ_TOKEN_TILE = 2048
TOKEN_TILE = 1024
SUB_TILE = 512
FFN_TOKEN_TILE = 1024
FFN_TOKEN_TILE_F32W = 512
FF_CHUNK = MXU_COLS_V7X
NA_ROW_BLOCK = 16
DFT_ROWS = 64
DFT_COLS = 128
SUBLANES_F32 = 8
BF16_SUBLANES = 16
F1_T2_BLOCK = SUBLANES_F32
F1_T2_STEP = 32
F3_K1_BLOCK = 16
F1_STEPS = DFT_COLS // F1_T2_STEP
F3_STEPS = DFT_ROWS // F3_K1_BLOCK


def _params(vmem_mib):
    assert vmem_mib * 1024 * 1024 <= VMEM_BYTES_V7X
    return pltpu.CompilerParams(
        dimension_semantics=None,
        vmem_limit_bytes=vmem_mib * 1024 * 1024,
    )


def _const_spec(shape, col_block=0):
    index = (0,) * (len(shape) - 1) + (col_block,)
    return pl.BlockSpec(shape, lambda *_: index, pipeline_mode=pl.Buffered(1))


def _cast_specs(params, steps):
    in_specs, out_specs, out_shapes = [], [], []
    for arr, layer in params:
        _, rows, cols = arr.shape
        nblk = steps
        while rows % nblk or (rows // nblk) % BF16_SUBLANES:
            nblk //= 2
        blk = rows // nblk
        in_specs.append(pl.BlockSpec(
            (None, blk, cols), lambda i, layer=layer, nblk=nblk: (layer, jnp.minimum(i, nblk - 1), 0)))
        out_specs.append(pl.BlockSpec(
            (blk, cols), lambda i, nblk=nblk: (jnp.minimum(i, nblk - 1), 0)))
        out_shapes.append(jax.ShapeDtypeStruct((rows, cols), BF16))
    return in_specs, out_specs, out_shapes


def _cast_blocks(src_refs, dst_refs):
    for src, dst in zip(src_refs, dst_refs):
        dst[...] = src[...].astype(BF16)


def _layer_spec(shape, layer):
    index = (layer,) + (0,) * len(shape)
    return pl.BlockSpec((None,) + tuple(shape), lambda *_: index, pipeline_mode=pl.Buffered(1))


def _rms(x, g):
    ms = jnp.mean(x * x, axis=-1, keepdims=True)
    return x * lax.rsqrt(ms + RMS_EPS) * g


def _dot(a, b):
    return jnp.dot(a, b, preferred_element_type=F32)


def _ffn_kernel(x_ref, g_ref, win_ref, wout_ref, *rest, final, n_cast):
    rest = list(rest)
    gf_ref = rest.pop(0) if final else None
    cast_src, o_ref, cast_dst, act_ref = (rest[:n_cast], rest[n_cast],
                                          rest[n_cast + 1:2 * n_cast + 1], rest[-1])
    _cast_blocks(cast_src, cast_dst)
    x = x_ref[...]
    h = _rms(x, g_ref[...]).astype(BF16)
    for c in range(D_FF // FF_CHUNK):
        lo = c * FF_CHUNK
        g = _dot(h, win_ref[:, lo:lo + FF_CHUNK].astype(BF16))
        u = _dot(h, win_ref[:, D_FF + lo:D_FF + lo + FF_CHUNK].astype(BF16))
        act_ref[:, lo:lo + FF_CHUNK] = (g * jax.nn.sigmoid(g) * u).astype(BF16)
    y = x + 0.5 * _dot(act_ref[...], wout_ref[...].astype(BF16))
    if final:
        y = _rms(y, gf_ref[...])
    o_ref[...] = y


def _ffn(x, layer, g, w_in, w_out, final_g=None, cast=()):
    n = x.shape[0]
    f32_weights = w_in.ndim == 3
    rows = FFN_TOKEN_TILE_F32W if f32_weights else FFN_TOKEN_TILE
    steps = n // rows
    final = final_g is not None
    tile = pl.BlockSpec((rows, D_MODEL), lambda i: (i, 0))
    w_spec = functools.partial(_layer_spec, layer=layer) if f32_weights else _const_spec
    in_specs = [tile, _layer_spec((1, D_MODEL), layer), w_spec((D_MODEL, 2 * D_FF)),
                w_spec((D_FF, D_MODEL))]
    args = [x, g, w_in, w_out]
    if final:
        in_specs.append(_const_spec((1, D_MODEL)))
        args.append(final_g.reshape(1, D_MODEL))
    cast_in, cast_out, cast_shapes = _cast_specs(cast, steps)
    y, *copies = pl.pallas_call(
        functools.partial(_ffn_kernel, final=final, n_cast=len(cast)),
        grid=(steps,),
        in_specs=in_specs + cast_in,
        out_specs=[tile] + cast_out,
        out_shape=[jax.ShapeDtypeStruct((n, D_MODEL), F32)] + cast_shapes,
        scratch_shapes=[pltpu.VMEM((rows, D_FF), BF16)],
        compiler_params=_params(58),
        name="ffn",
    )(*args, *[arr for arr, _ in cast])
    return y, copies


def _proj_kernel(x_ref, g_ref, w_ref, q_ref, k_ref, v_ref, u_ref):
    s = NA_WIDTH
    for t in range(PROJ_TOKEN_TILE // SUB_TILE):
        rows = slice(t * SUB_TILE, (t + 1) * SUB_TILE)
        h = _rms(x_ref[rows, :], g_ref[...]).astype(BF16)
        z = _dot(h, w_ref[...])
        q_ref[rows, :] = (z[:, :s] * (NA_HEAD_DIM ** -0.5)).astype(BF16)
        k_ref[rows, :] = z[:, s:2 * s].astype(BF16)
        v_ref[rows, :] = z[:, 2 * s:3 * s].astype(BF16)
        u_ref[rows, :] = z[:, 3 * s:3 * s + F_WIDTH]


def _proj(x, layer, g, w):
    n = x.shape[0]
    cols = 3 * NA_WIDTH + F_WIDTH
    assert w.shape[1] == 2 * cols
    out_tile = pl.BlockSpec((PROJ_TOKEN_TILE, NA_WIDTH), lambda i: (i, 0))
    out_sds = jax.ShapeDtypeStruct((n, NA_WIDTH), BF16)
    return pl.pallas_call(
        _proj_kernel,
        grid=(n // PROJ_TOKEN_TILE,),
        in_specs=[pl.BlockSpec((PROJ_TOKEN_TILE, D_MODEL), lambda i: (i, 0)),
                  _layer_spec((1, D_MODEL), layer), _const_spec((D_MODEL, cols), 0)],
        out_specs=[out_tile] * 4,
        out_shape=[out_sds] * 3 + [jax.ShapeDtypeStruct((n, F_WIDTH), F32)],
        compiler_params=_params(48),
        name="mix_proj",
    )(x, g, w)


def _na_bias_tables(rpb):
    ncol = 2 * NA_COLS - 1
    qc = jnp.arange(GRID_W)
    lane = jnp.arange(2 * GRID_W)
    kc, side = lane % GRID_W, lane // GRID_W
    ws = jnp.clip(qc - NA_COLS // 2, 0, GRID_W - NA_COLS)
    ok = (kc[None, :] >= ws[:, None]) & (kc[None, :] < ws[:, None] + NA_COLS)
    dc = kc[None, :] - qc[:, None] + (NA_COLS - 1)
    src = jnp.where(ok, side[None, :] * ncol + dc, -1)
    pick = (src[:, :, None] == jnp.arange(2 * ncol)[None, None, :]).astype(F32)
    both = jnp.concatenate([rpb[:, :, :-1], rpb[:, :, 1:]], axis=-1).astype(F32)
    table = jnp.einsum('lhdm,qkm->lhdqk', both, pick, precision=lax.Precision.HIGHEST)
    return jnp.where(ok[None, None, None], table, -jnp.inf)


def _na_kernel(q_ref, k_ref, v_ref, bias_ref, hm_ref, o_ref, *, rows):
    blk = pl.program_id(1)
    kh = NA_ROWS
    half = kh // 2
    lane_lo = lax.broadcasted_iota(jnp.int32, (GRID_W, 128), 1) < NA_HEAD_DIM

    def window(i):
        r = blk * NA_ROW_BLOCK + i
        rs = jnp.clip(r - half, 0, rows - kh)
        dr0 = rs - r + (NA_ROWS - 1)
        return pl.multiple_of(rs * GRID_W, GRID_W), dr0

    def bias(h, dr0):
        return jnp.concatenate([bias_ref[h, dr0 + 2 * j] for j in range(kh // 2)], axis=-1)

    def score_stage(i):
        koff, _ = window(i)
        ss = []
        for p in range(NA_HEADS // 2):
            lanes = slice(p * 128, (p + 1) * 128)
            qp = q_ref[i * GRID_W:(i + 1) * GRID_W, lanes]
            ql = jnp.concatenate([qp * hm_ref[0], qp * hm_ref[1]], axis=0)
            kw = k_ref[0, pl.ds(koff, kh * GRID_W), lanes]
            ss.append(lax.dot_general(ql, kw, (((1,), (1,)), ((), ())),
                                      preferred_element_type=F32))
        return ss

    def output_stage(i, ss):
        koff, dr0 = window(i)
        es, ls = [], []
        for p in range(NA_HEADS // 2):
            s = ss[p] + jnp.concatenate([bias(2 * p, dr0), bias(2 * p + 1, dr0)], axis=0)
            m = jnp.max(s, axis=-1, keepdims=True)
            e = jnp.exp(s - m)
            ls.append(jnp.sum(e, axis=-1, keepdims=True))
            es.append(e.astype(BF16))
        for p in range(NA_HEADS // 2):
            lanes = slice(p * 128, (p + 1) * 128)
            vw = v_ref[0, pl.ds(koff, kh * GRID_W), lanes]
            pv = _dot(es[p], vw) / ls[p]
            o = jnp.where(lane_lo, pv[:GRID_W], pv[GRID_W:])
            o_ref[i * GRID_W:(i + 1) * GRID_W, lanes] = o.astype(BF16)

    ss = score_stage(0)
    for i in range(NA_ROW_BLOCK):
        nxt = score_stage(i + 1) if i + 1 < NA_ROW_BLOCK else None
        output_stage(i, ss)
        ss = nxt


def _na(q, k, v, layer, bias, batch, seq):
    rows = seq // GRID_W
    assert rows >= NA_ROWS and NA_ROWS % 2 == 0
    blk_tokens = NA_ROW_BLOCK * GRID_W
    nblk = rows // NA_ROW_BLOCK
    head_mask = (jnp.arange(128)[None, None, :] // NA_HEAD_DIM
                 == jnp.arange(2)[:, None, None]).astype(BF16)
    head_mask = jnp.broadcast_to(head_mask, (2, GRID_W, 128))
    q_tile = pl.BlockSpec((blk_tokens, NA_WIDTH), lambda b, i: (b * nblk + i, 0))
    kv_spec = pl.BlockSpec((1, seq, NA_WIDTH), lambda b, i: (b, 0, 0))
    return pl.pallas_call(
        functools.partial(_na_kernel, rows=rows),
        grid=(batch, nblk),
        in_specs=[q_tile, kv_spec, kv_spec,
                  _layer_spec((NA_HEADS, 2 * NA_ROWS - 2, GRID_W, 2 * GRID_W), layer),
                  _const_spec((2, GRID_W, 128))],
        out_specs=q_tile,
        out_shape=jax.ShapeDtypeStruct((batch * seq, NA_WIDTH), BF16),
        compiler_params=_params(56),
        name="na",
    )(q, k.reshape(batch, seq, NA_WIDTH), v.reshape(batch, seq, NA_WIDTH), bias, head_mask)


def _dft_tables():
    two_pi = 2.0 * math.pi
    n = DFT_ROWS * DFT_COLS
    k1 = np.arange(DFT_ROWS)
    ang = ((k1[:, None] * k1[None, :]) % DFT_ROWS) * (two_pi / DFT_ROWS)
    eye_t2 = np.eye(F1_T2_BLOCK)
    f1 = np.concatenate([np.kron(np.cos(ang), eye_t2), np.kron(-np.sin(ang), eye_t2)], axis=0)
    k2 = np.arange(DFT_COLS)
    t2 = np.arange(DFT_COLS)
    ang = ((t2[None, None, :] * (DFT_ROWS * k2[None, :, None] + k1[:, None, None])) % n) * (two_pi / n)
    gr, gi = np.cos(ang), -np.sin(ang)
    g = np.concatenate([np.concatenate([gr, -gi], axis=2),
                        np.concatenate([gi, gr], axis=2)], axis=1)
    c = np.arange(F_GROUP_DIM)
    ang = ((c[:, None] * c[None, :]) % F_GROUP_DIM) * (two_pi / F_GROUP_DIM)
    eye = np.eye(MXU_COLS_V7X // F_GROUP_DIM)
    cc = np.kron(eye, np.cos(ang))
    ss = np.kron(eye, np.sin(ang))
    return tuple(jnp.asarray(t.astype(np.float32)).astype(BF16) for t in (f1, g, cc, ss))


def _fnet_kernel(f_ref, x_ref, g_ref, cc_ref, ss_ref, o_ref, a_ref, zr_ref, zi_ref):
    step = pl.program_id(1)

    @pl.when(step < F1_STEPS)
    def _stage1():
        per_store = BF16_SUBLANES // F1_T2_BLOCK
        base = step * F1_T2_STEP
        for s in range(F1_T2_STEP // BF16_SUBLANES):
            zs = []
            for j in range(per_store):
                lo = (s * per_store + j) * F1_T2_BLOCK
                x = x_ref[0, :, lo:lo + F1_T2_BLOCK, :]
                x = x.reshape(DFT_ROWS * F1_T2_BLOCK, F_WIDTH).astype(BF16)
                z = _dot(f_ref[...], x)
                zs.append(z.reshape(2, DFT_ROWS, F1_T2_BLOCK, F_WIDTH))
            t2 = pl.multiple_of(base + s * BF16_SUBLANES, BF16_SUBLANES)
            a_ref[:, :, pl.ds(t2, BF16_SUBLANES), :] = jnp.concatenate(zs, axis=2).astype(BF16)

    @pl.when(step >= F1_STEPS)
    def _stage2():
        k1 = (step - F1_STEPS) * F3_K1_BLOCK
        for j in range(F3_K1_BLOCK):
            a = jnp.concatenate([a_ref[0, k1 + j], a_ref[1, k1 + j]], axis=0)
            z = _dot(g_ref[j], a)
            zr_ref[j * DFT_COLS:(j + 1) * DFT_COLS] = z[:DFT_COLS].astype(BF16)
            zi_ref[j * DFT_COLS:(j + 1) * DFT_COLS] = z[DFT_COLS:].astype(BF16)
        scale = 1.0 / math.sqrt(DFT_ROWS * DFT_COLS * F_GROUP_DIM)
        for p in range(F_WIDTH // MXU_COLS_V7X):
            cols = slice(p * MXU_COLS_V7X, (p + 1) * MXU_COLS_V7X)
            y = _dot(zr_ref[:, cols], cc_ref[...]) + _dot(zi_ref[:, cols], ss_ref[...])
            y = (y * scale).astype(BF16)
            for j in range(F3_K1_BLOCK):
                o_ref[0, j, :, cols] = y[j * DFT_COLS:(j + 1) * DFT_COLS]


def _fourier(u, tables, batch, seq):
    f1, g, cc, ss = tables
    f1_rows = 2 * DFT_ROWS * F1_T2_BLOCK
    y = pl.pallas_call(
        _fnet_kernel,
        grid=(batch, F1_STEPS + F3_STEPS),
        in_specs=[_const_spec((f1_rows, f1_rows // 2)),
                  pl.BlockSpec((1, DFT_ROWS, F1_T2_STEP, F_WIDTH),
                               lambda b, i: (b, 0, jnp.minimum(i, F1_STEPS - 1), 0)),
                  pl.BlockSpec((F3_K1_BLOCK, 2 * DFT_COLS, 2 * DFT_COLS),
                               lambda b, i: (jnp.maximum(i - F1_STEPS, 0), 0, 0)),
                  _const_spec((MXU_COLS_V7X, MXU_COLS_V7X)), _const_spec((MXU_COLS_V7X, MXU_COLS_V7X))],
        out_specs=pl.BlockSpec((1, F3_K1_BLOCK, DFT_COLS, F_WIDTH),
                               lambda b, i: (b, jnp.maximum(i - F1_STEPS, 0), 0, 0)),
        out_shape=jax.ShapeDtypeStruct((batch, DFT_ROWS, DFT_COLS, F_WIDTH), BF16),
        scratch_shapes=[pltpu.VMEM((2, DFT_ROWS, DFT_COLS, F_WIDTH), BF16),
                        pltpu.VMEM((F3_K1_BLOCK * DFT_COLS, F_WIDTH), BF16),
                        pltpu.VMEM((F3_K1_BLOCK * DFT_COLS, F_WIDTH), BF16)],
        compiler_params=_params(52),
        name="fnet",
    )(f1, u.reshape(batch, DFT_ROWS, DFT_COLS, F_WIDTH), g, cc, ss)
    return jnp.transpose(y, (0, 2, 1, 3)).reshape(batch * seq, F_WIDTH)


def _mixout_kernel(x_ref, g_ref, att_ref, fou_ref, wg_ref, gb_ref, wna_ref, wf_ref, wo_ref,
                   *rest, n_cast):
    cast_src, o_ref, cast_dst = rest[:n_cast], rest[n_cast], rest[n_cast + 1:]
    _cast_blocks(cast_src, cast_dst)
    for t in range(TOKEN_TILE // SUB_TILE):
        rows = slice(t * SUB_TILE, (t + 1) * SUB_TILE)
        x = x_ref[rows, :]
        h = _rms(x, g_ref[...]).astype(BF16)
        gates = jax.nn.sigmoid(_dot(h, wg_ref[...]) + gb_ref[...])
        y_na = _dot(att_ref[rows, :], wna_ref[...])
        y_f = _dot(fou_ref[rows, :], wf_ref[...])
        m = gates[:, :D_MODEL] * y_na + gates[:, D_MODEL:] * y_f
        o_ref[rows, :] = x + _dot(m.astype(BF16), wo_ref[...])


def _mixout(x, layer, g, att, fou, w_in, gate_bias, w_na, w_f, w_o, cast=()):
    n = x.shape[0]
    steps = n // TOKEN_TILE
    tile = pl.BlockSpec((TOKEN_TILE, D_MODEL), lambda i: (i, 0))
    half = pl.BlockSpec((TOKEN_TILE, NA_WIDTH), lambda i: (i, 0))
    assert w_in.shape[1] == 4 * D_MODEL
    cast_in, cast_out, cast_shapes = _cast_specs(cast, steps)
    y, *copies = pl.pallas_call(
        functools.partial(_mixout_kernel, n_cast=len(cast)),
        grid=(steps,),
        in_specs=[tile, _layer_spec((1, D_MODEL), layer), half, half,
                  _const_spec((D_MODEL, 2 * D_MODEL), 1),
                  _layer_spec((1, 2 * D_MODEL), layer),
                  _const_spec((NA_WIDTH, D_MODEL)), _const_spec((F_WIDTH, D_MODEL)),
                  _const_spec((D_MODEL, D_MODEL))] + cast_in,
        out_specs=[tile] + cast_out,
        out_shape=[jax.ShapeDtypeStruct((n, D_MODEL), F32)] + cast_shapes,
        compiler_params=_params(48),
        name="mix_out",
    )(x, g, att, fou, w_in, gate_bias, w_na, w_f, w_o, *[arr for arr, _ in cast])
    return y, copies


def kernel(x, ffn1_norm, ffn1_w_in, ffn1_w_out, mix_norm, mix_w_in, mix_gate_bias, na_rpb,
           na_w_out, f_w_out, mix_w_o, ffn2_norm, ffn2_w_in, ffn2_w_out, final_norm):
    batch, seq, d = x.shape
    depth = ffn1_norm.shape[0]
    assert d == D_MODEL and seq == DFT_ROWS * DFT_COLS and seq % (GRID_W * NA_ROW_BLOCK) == 0
    assert all((batch * seq) % t == 0
               for t in (PROJ_TOKEN_TILE, TOKEN_TILE, FFN_TOKEN_TILE, FFN_TOKEN_TILE_F32W))
    tables = _dft_tables()
    na_bias = _na_bias_tables(na_rpb)
    gain = lambda g: g.reshape(depth, 1, D_MODEL)
    g1, g2, mix_g = gain(ffn1_norm), gain(ffn2_norm), gain(mix_norm)
    gate_bias = mix_gate_bias.reshape(depth, 1, 2 * D_MODEL)
    xs = x.reshape(batch * seq, d)
    ffn1_w = [ffn1_w_in, ffn1_w_out]
    for l in range(depth):
        xs, (w_in, w_na, w_f, w_o) = _ffn(
            xs, l, g1, *ffn1_w,
            cast=[(mix_w_in, l), (na_w_out, l), (f_w_out, l), (mix_w_o, l)])
        q, k, v, u = _proj(xs, l, mix_g, w_in)
        att = _na(q, k, v, l, na_bias, batch, seq)
        fou = _fourier(u, tables, batch, seq)
        xs, ffn2_w = _mixout(xs, l, mix_g, att, fou, w_in, gate_bias, w_na, w_f, w_o,
                             cast=[(ffn2_w_in, l), (ffn2_w_out, l)])
        last = l == depth - 1
        xs, ffn1_w = _ffn(xs, l, g2, *ffn2_w, final_g=final_norm if last else None,
                          cast=[] if last else [(ffn1_w_in, l + 1), (ffn1_w_out, l + 1)])
    return xs.reshape(batch, seq, d)
```

```python
import functools
import math

import jax
import jax.numpy as jnp
import numpy as np
from jax import lax
from jax.experimental import pallas as pl
from jax.experimental.pallas import tpu as pltpu

D_MODEL = 1024
GRID_W = 64
NA_HEADS = 8
NA_HEAD_DIM = 64
NA_WIDTH = NA_HEADS * NA_HEAD_DIM
NA_ROWS = 8
NA_COLS = 16
F_GROUPS = 4
F_GROUP_DIM = 128
F_WIDTH = F_GROUPS * F_GROUP_DIM
D_FF = 2816
RMS_EPS = 1e-6

BF16 = jnp.bfloat16
F32 = jnp.float32

MXU_COLS_V7X = 256
VMEM_BYTES_V7X = 64 * 1024 * 1024

PROJ_TOKEN_TILE = 2048
TOKEN_TILE = 1024
SUB_TILE = 512
FFN_TOKEN_TILE = 1024
FFN_TOKEN_TILE_F32W = 512
FF_CHUNK = MXU_COLS_V7X
NA_ROW_BLOCK = 16
DFT_ROWS = 64
DFT_COLS = 128
SUBLANES_F32 = 8
BF16_SUBLANES = 16
F1_T2_BLOCK = SUBLANES_F32
F1_T2_STEP = 32
F3_K1_BLOCK = 16
F1_STEPS = DFT_COLS // F1_T2_STEP
F3_STEPS = DFT_ROWS // F3_K1_BLOCK


def _params(vmem_mib):
    assert vmem_mib * 1024 * 1024 <= VMEM_BYTES_V7X
    return pltpu.CompilerParams(
        dimension_semantics=None,
        vmem_limit_bytes=vmem_mib * 1024 * 1024,
    )


def _const_spec(shape, col_block=0):
    index = (0,) * (len(shape) - 1) + (col_block,)
    return pl.BlockSpec(shape, lambda *_: index, pipeline_mode=pl.Buffered(1))


def _cast_specs(params, steps):
    in_specs, out_specs, out_shapes = [], [], []
    for arr, layer in params:
        _, rows, cols = arr.shape
        nblk = steps
        while rows % nblk or (rows // nblk) % BF16_SUBLANES:
            nblk //= 2
        blk = rows // nblk
        in_specs.append(pl.BlockSpec(
            (None, blk, cols), lambda i, layer=layer, nblk=nblk: (layer, jnp.minimum(i, nblk - 1), 0)))
        out_specs.append(pl.BlockSpec(
            (blk, cols), lambda i, nblk=nblk: (jnp.minimum(i, nblk - 1), 0)))
        out_shapes.append(jax.ShapeDtypeStruct((rows, cols), BF16))
    return in_specs, out_specs, out_shapes


def _cast_blocks(src_refs, dst_refs):
    for src, dst in zip(src_refs, dst_refs):
        dst[...] = src[...].astype(BF16)


def _layer_spec(shape, layer):
    index = (layer,) + (0,) * len(shape)
    return pl.BlockSpec((None,) + tuple(shape), lambda *_: index, pipeline_mode=pl.Buffered(1))


def _rms(x, g):
    ms = jnp.mean(x * x, axis=-1, keepdims=True)
    return x * lax.rsqrt(ms + RMS_EPS) * g


def _dot(a, b):
    return jnp.dot(a, b, preferred_element_type=F32)


def _ffn_kernel(x_ref, g_ref, win_ref, wout_ref, *rest, final, n_cast):
    rest = list(rest)
    gf_ref = rest.pop(0) if final else None
    cast_src, o_ref, cast_dst, act_ref = (rest[:n_cast], rest[n_cast],
                                          rest[n_cast + 1:2 * n_cast + 1], rest[-1])
    _cast_blocks(cast_src, cast_dst)
    x = x_ref[...]
    h = _rms(x, g_ref[...]).astype(BF16)
    for c in range(D_FF // FF_CHUNK):
        lo = c * FF_CHUNK
        g = _dot(h, win_ref[:, lo:lo + FF_CHUNK].astype(BF16))
        u = _dot(h, win_ref[:, D_FF + lo:D_FF + lo + FF_CHUNK].astype(BF16))
        act_ref[:, lo:lo + FF_CHUNK] = (g * jax.nn.sigmoid(g) * u).astype(BF16)
    y = x + 0.5 * _dot(act_ref[...], wout_ref[...].astype(BF16))
    if final:
        y = _rms(y, gf_ref[...])
    o_ref[...] = y


def _ffn(x, layer, g, w_in, w_out, final_g=None, cast=()):
    n = x.shape[0]
    f32_weights = w_in.ndim == 3
    rows = FFN_TOKEN_TILE_F32W if f32_weights else FFN_TOKEN_TILE
    steps = n // rows
    final = final_g is not None
    tile = pl.BlockSpec((rows, D_MODEL), lambda i: (i, 0))
    w_spec = functools.partial(_layer_spec, layer=layer) if f32_weights else _const_spec
    in_specs = [tile, _layer_spec((1, D_MODEL), layer), w_spec((D_MODEL, 2 * D_FF)),
                w_spec((D_FF, D_MODEL))]
    args = [x, g, w_in, w_out]
    if final:
        in_specs.append(_const_spec((1, D_MODEL)))
        args.append(final_g.reshape(1, D_MODEL))
    cast_in, cast_out, cast_shapes = _cast_specs(cast, steps)
    y, *copies = pl.pallas_call(
        functools.partial(_ffn_kernel, final=final, n_cast=len(cast)),
        grid=(steps,),
        in_specs=in_specs + cast_in,
        out_specs=[tile] + cast_out,
        out_shape=[jax.ShapeDtypeStruct((n, D_MODEL), F32)] + cast_shapes,
        scratch_shapes=[pltpu.VMEM((rows, D_FF), BF16)],
        compiler_params=_params(58),
        name="ffn",
    )(*args, *[arr for arr, _ in cast])
    return y, copies


def _proj_kernel(x_ref, g_ref, w_ref, q_ref, k_ref, v_ref, u_ref):
    s = NA_WIDTH
    for t in range(PROJ_TOKEN_TILE // SUB_TILE):
        rows = slice(t * SUB_TILE, (t + 1) * SUB_TILE)
        h = _rms(x_ref[rows, :], g_ref[...]).astype(BF16)
        z = _dot(h, w_ref[...])
        q_ref[rows, :] = (z[:, :s] * (NA_HEAD_DIM ** -0.5)).astype(BF16)
        k_ref[rows, :] = z[:, s:2 * s].astype(BF16)
        v_ref[rows, :] = z[:, 2 * s:3 * s].astype(BF16)
        u_ref[rows, :] = z[:, 3 * s:3 * s + F_WIDTH]


def _proj(x, layer, g, w):
    n = x.shape[0]
    cols = 3 * NA_WIDTH + F_WIDTH
    assert w.shape[1] == 2 * cols
    out_tile = pl.BlockSpec((PROJ_TOKEN_TILE, NA_WIDTH), lambda i: (i, 0))
    out_sds = jax.ShapeDtypeStruct((n, NA_WIDTH), BF16)
    return pl.pallas_call(
        _proj_kernel,
        grid=(n // PROJ_TOKEN_TILE,),
        in_specs=[pl.BlockSpec((PROJ_TOKEN_TILE, D_MODEL), lambda i: (i, 0)),
                  _layer_spec((1, D_MODEL), layer), _const_spec((D_MODEL, cols), 0)],
        out_specs=[out_tile] * 4,
        out_shape=[out_sds] * 3 + [jax.ShapeDtypeStruct((n, F_WIDTH), F32)],
        compiler_params=_params(48),
        name="mix_proj",
    )(x, g, w)


def _na_bias_tables(rpb):
    ncol = 2 * NA_COLS - 1
    qc = jnp.arange(GRID_W)
    lane = jnp.arange(2 * GRID_W)
    kc, side = lane % GRID_W, lane // GRID_W
    ws = jnp.clip(qc - NA_COLS // 2, 0, GRID_W - NA_COLS)
    ok = (kc[None, :] >= ws[:, None]) & (kc[None, :] < ws[:, None] + NA_COLS)
    dc = kc[None, :] - qc[:, None] + (NA_COLS - 1)
    src = jnp.where(ok, side[None, :] * ncol + dc, -1)
    pick = (src[:, :, None] == jnp.arange(2 * ncol)[None, None, :]).astype(F32)
    both = jnp.concatenate([rpb[:, :, :-1], rpb[:, :, 1:]], axis=-1).astype(F32)
    table = jnp.einsum('lhdm,qkm->lhdqk', both, pick, precision=lax.Precision.HIGHEST)
    return jnp.where(ok[None, None, None], table, -jnp.inf)


def _na_kernel(q_ref, k_ref, v_ref, bias_ref, hm_ref, o_ref, *, rows):
    blk = pl.program_id(1)
    kh = NA_ROWS
    half = kh // 2
    lane_lo = lax.broadcasted_iota(jnp.int32, (GRID_W, 128), 1) < NA_HEAD_DIM

    def window(i):
        r = blk * NA_ROW_BLOCK + i
        rs = jnp.clip(r - half, 0, rows - kh)
        dr0 = rs - r + (NA_ROWS - 1)
        return pl.multiple_of(rs * GRID_W, GRID_W), dr0

    def bias(h, dr0):
        return jnp.concatenate([bias_ref[h, dr0 + 2 * j] for j in range(kh // 2)], axis=-1)

    def score_stage(i):
        koff, _ = window(i)
        ss = []
        for p in range(NA_HEADS // 2):
            lanes = slice(p * 128, (p + 1) * 128)
            qp = q_ref[i * GRID_W:(i + 1) * GRID_W, lanes]
            ql = jnp.concatenate([qp * hm_ref[0], qp * hm_ref[1]], axis=0)
            kw = k_ref[0, pl.ds(koff, kh * GRID_W), lanes]
            ss.append(lax.dot_general(ql, kw, (((1,), (1,)), ((), ())),
                                      preferred_element_type=F32))
        return ss

    def output_stage(i, ss):
        koff, dr0 = window(i)
        es, ls = [], []
        for p in range(NA_HEADS // 2):
            s = ss[p] + jnp.concatenate([bias(2 * p, dr0), bias(2 * p + 1, dr0)], axis=0)
            m = jnp.max(s, axis=-1, keepdims=True)
            e = jnp.exp((s - m).astype(BF16))
            ls.append(jnp.sum(e.astype(F32), axis=-1, keepdims=True))
            es.append(e)
        for p in range(NA_HEADS // 2):
            lanes = slice(p * 128, (p + 1) * 128)
            vw = v_ref[0, pl.ds(koff, kh * GRID_W), lanes]
            pv = _dot(es[p], vw) / ls[p]
            o = jnp.where(lane_lo, pv[:GRID_W], pv[GRID_W:])
            o_ref[i * GRID_W:(i + 1) * GRID_W, lanes] = o.astype(BF16)

    ss = score_stage(0)
    for i in range(NA_ROW_BLOCK):
        nxt = score_stage(i + 1) if i + 1 < NA_ROW_BLOCK else None
        output_stage(i, ss)
        ss = nxt


def _na(q, k, v, layer, bias, batch, seq):
    rows = seq // GRID_W
    assert rows >= NA_ROWS and NA_ROWS % 2 == 0
    blk_tokens = NA_ROW_BLOCK * GRID_W
    nblk = rows // NA_ROW_BLOCK
    head_mask = (jnp.arange(128)[None, None, :] // NA_HEAD_DIM
                 == jnp.arange(2)[:, None, None]).astype(BF16)
    head_mask = jnp.broadcast_to(head_mask, (2, GRID_W, 128))
    q_tile = pl.BlockSpec((blk_tokens, NA_WIDTH), lambda b, i: (b * nblk + i, 0))
    kv_spec = pl.BlockSpec((1, seq, NA_WIDTH), lambda b, i: (b, 0, 0))
    return pl.pallas_call(
        functools.partial(_na_kernel, rows=rows),
        grid=(batch, nblk),
        in_specs=[q_tile, kv_spec, kv_spec,
                  _layer_spec((NA_HEADS, 2 * NA_ROWS - 2, GRID_W, 2 * GRID_W), layer),
                  _const_spec((2, GRID_W, 128))],
        out_specs=q_tile,
        out_shape=jax.ShapeDtypeStruct((batch * seq, NA_WIDTH), BF16),
        compiler_params=_params(56),
        name="na",
    )(q, k.reshape(batch, seq, NA_WIDTH), v.reshape(batch, seq, NA_WIDTH), bias, head_mask)


def _dft_tables():
    two_pi = 2.0 * math.pi
    n = DFT_ROWS * DFT_COLS
    k1 = np.arange(DFT_ROWS)
    ang = ((k1[:, None] * k1[None, :]) % DFT_ROWS) * (two_pi / DFT_ROWS)
    eye_t2 = np.eye(F1_T2_BLOCK)
    f1 = np.concatenate([np.kron(np.cos(ang), eye_t2), np.kron(-np.sin(ang), eye_t2)], axis=0)
    k2 = np.arange(DFT_COLS)
    t2 = np.arange(DFT_COLS)
    ang = ((t2[None, None, :] * (DFT_ROWS * k2[None, :, None] + k1[:, None, None])) % n) * (two_pi / n)
    gr, gi = np.cos(ang), -np.sin(ang)
    g = np.concatenate([np.concatenate([gr, -gi], axis=2),
                        np.concatenate([gi, gr], axis=2)], axis=1)
    c = np.arange(F_GROUP_DIM)
    ang = ((c[:, None] * c[None, :]) % F_GROUP_DIM) * (two_pi / F_GROUP_DIM)
    eye = np.eye(MXU_COLS_V7X // F_GROUP_DIM)
    cc = np.kron(eye, np.cos(ang))
    ss = np.kron(eye, np.sin(ang))
    return tuple(jnp.asarray(t.astype(np.float32)).astype(BF16) for t in (f1, g, cc, ss))


def _fnet_kernel(f_ref, x_ref, g_ref, cc_ref, ss_ref, o_ref, a_ref, zr_ref, zi_ref):
    step = pl.program_id(1)

    @pl.when(step < F1_STEPS)
    def _stage1():
        per_store = BF16_SUBLANES // F1_T2_BLOCK
        base = step * F1_T2_STEP
        for s in range(F1_T2_STEP // BF16_SUBLANES):
            zs = []
            for j in range(per_store):
                lo = (s * per_store + j) * F1_T2_BLOCK
                x = x_ref[0, :, lo:lo + F1_T2_BLOCK, :]
                x = x.reshape(DFT_ROWS * F1_T2_BLOCK, F_WIDTH).astype(BF16)
                z = _dot(f_ref[...], x)
                zs.append(z.reshape(2, DFT_ROWS, F1_T2_BLOCK, F_WIDTH))
            t2 = pl.multiple_of(base + s * BF16_SUBLANES, BF16_SUBLANES)
            a_ref[:, :, pl.ds(t2, BF16_SUBLANES), :] = jnp.concatenate(zs, axis=2).astype(BF16)

    @pl.when(step >= F1_STEPS)
    def _stage2():
        k1 = (step - F1_STEPS) * F3_K1_BLOCK
        for j in range(F3_K1_BLOCK):
            a = jnp.concatenate([a_ref[0, k1 + j], a_ref[1, k1 + j]], axis=0)
            z = _dot(g_ref[j], a)
            zr_ref[j * DFT_COLS:(j + 1) * DFT_COLS] = z[:DFT_COLS].astype(BF16)
            zi_ref[j * DFT_COLS:(j + 1) * DFT_COLS] = z[DFT_COLS:].astype(BF16)
        scale = 1.0 / math.sqrt(DFT_ROWS * DFT_COLS * F_GROUP_DIM)
        for p in range(F_WIDTH // MXU_COLS_V7X):
            cols = slice(p * MXU_COLS_V7X, (p + 1) * MXU_COLS_V7X)
            y = _dot(zr_ref[:, cols], cc_ref[...]) + _dot(zi_ref[:, cols], ss_ref[...])
            y = (y * scale).astype(BF16)
            for j in range(F3_K1_BLOCK):
                o_ref[0, j, :, cols] = y[j * DFT_COLS:(j + 1) * DFT_COLS]


def _fourier(u, tables, batch, seq):
    f1, g, cc, ss = tables
    f1_rows = 2 * DFT_ROWS * F1_T2_BLOCK
    y = pl.pallas_call(
        _fnet_kernel,
        grid=(batch, F1_STEPS + F3_STEPS),
        in_specs=[_const_spec((f1_rows, f1_rows // 2)),
                  pl.BlockSpec((1, DFT_ROWS, F1_T2_STEP, F_WIDTH),
                               lambda b, i: (b, 0, jnp.minimum(i, F1_STEPS - 1), 0)),
                  pl.BlockSpec((F3_K1_BLOCK, 2 * DFT_COLS, 2 * DFT_COLS),
                               lambda b, i: (jnp.maximum(i - F1_STEPS, 0), 0, 0)),
                  _const_spec((MXU_COLS_V7X, MXU_COLS_V7X)), _const_spec((MXU_COLS_V7X, MXU_COLS_V7X))],
        out_specs=pl.BlockSpec((1, F3_K1_BLOCK, DFT_COLS, F_WIDTH),
                               lambda b, i: (b, jnp.maximum(i - F1_STEPS, 0), 0, 0)),
        out_shape=jax.ShapeDtypeStruct((batch, DFT_ROWS, DFT_COLS, F_WIDTH), BF16),
        scratch_shapes=[pltpu.VMEM((2, DFT_ROWS, DFT_COLS, F_WIDTH), BF16),
                        pltpu.VMEM((F3_K1_BLOCK * DFT_COLS, F_WIDTH), BF16),
                        pltpu.VMEM((F3_K1_BLOCK * DFT_COLS, F_WIDTH), BF16)],
        compiler_params=_params(52),
        name="fnet",
    )(f1, u.reshape(batch, DFT_ROWS, DFT_COLS, F_WIDTH), g, cc, ss)
    return jnp.transpose(y, (0, 2, 1, 3)).reshape(batch * seq, F_WIDTH)


def _mixout_kernel(x_ref, g_ref, att_ref, fou_ref, wg_ref, gb_ref, wna_ref, wf_ref, wo_ref,
                   *rest, n_cast):
    cast_src, o_ref, cast_dst = rest[:n_cast], rest[n_cast], rest[n_cast + 1:]
    _cast_blocks(cast_src, cast_dst)
    for t in range(TOKEN_TILE // SUB_TILE):
        rows = slice(t * SUB_TILE, (t + 1) * SUB_TILE)
        x = x_ref[rows, :]
        h = _rms(x, g_ref[...]).astype(BF16)
        gates = jax.nn.sigmoid(_dot(h, wg_ref[...]) + gb_ref[...])
        y_na = _dot(att_ref[rows, :], wna_ref[...])
        y_f = _dot(fou_ref[rows, :], wf_ref[...])
        m = gates[:, :D_MODEL] * y_na + gates[:, D_MODEL:] * y_f
        o_ref[rows, :] = x + _dot(m.astype(BF16), wo_ref[...])


def _mixout(x, layer, g, att, fou, w_in, gate_bias, w_na, w_f, w_o, cast=()):
    n = x.shape[0]
    steps = n // TOKEN_TILE
    tile = pl.BlockSpec((TOKEN_TILE, D_MODEL), lambda i: (i, 0))
    half = pl.BlockSpec((TOKEN_TILE, NA_WIDTH), lambda i: (i, 0))
    assert w_in.shape[1] == 4 * D_MODEL
    cast_in, cast_out, cast_shapes = _cast_specs(cast, steps)
    y, *copies = pl.pallas_call(
        functools.partial(_mixout_kernel, n_cast=len(cast)),
        grid=(steps,),
        in_specs=[tile, _layer_spec((1, D_MODEL), layer), half, half,
                  _const_spec((D_MODEL, 2 * D_MODEL), 1),
                  _layer_spec((1, 2 * D_MODEL), layer),
                  _const_spec((NA_WIDTH, D_MODEL)), _const_spec((F_WIDTH, D_MODEL)),
                  _const_spec((D_MODEL, D_MODEL))] + cast_in,
        out_specs=[tile] + cast_out,
        out_shape=[jax.ShapeDtypeStruct((n, D_MODEL), F32)] + cast_shapes,
        compiler_params=_params(48),
        name="mix_out",
    )(x, g, att, fou, w_in, gate_bias, w_na, w_f, w_o, *[arr for arr, _ in cast])
    return y, copies


def kernel(x, ffn1_norm, ffn1_w_in, ffn1_w_out, mix_norm, mix_w_in, mix_gate_bias, na_rpb,
           na_w_out, f_w_out, mix_w_o, ffn2_norm, ffn2_w_in, ffn2_w_out, final_norm):
    batch, seq, d = x.shape
    depth = ffn1_norm.shape[0]
    assert d == D_MODEL and seq == DFT_ROWS * DFT_COLS and seq % (GRID_W * NA_ROW_BLOCK) == 0
    assert all((batch * seq) % t == 0
               for t in (PROJ_TOKEN_TILE, TOKEN_TILE, FFN_TOKEN_TILE, FFN_TOKEN_TILE_F32W))
    tables = _dft_tables()
    na_bias = _na_bias_tables(na_rpb)
    gain = lambda g: g.reshape(depth, 1, D_MODEL)
    g1, g2, mix_g = gain(ffn1_norm), gain(ffn2_norm), gain(mix_norm)
    gate_bias = mix_gate_bias.reshape(depth, 1, 2 * D_MODEL)
    xs = x.reshape(batch * seq, d)
    ffn1_w = [ffn1_w_in, ffn1_w_out]
    for l in range(depth):
        xs, (w_in, w_na, w_f, w_o) = _ffn(
            xs, l, g1, *ffn1_w,
            cast=[(mix_w_in, l), (na_w_out, l), (f_w_out, l), (mix_w_o, l)])
        q, k, v, u = _proj(xs, l, mix_g, w_in)
        att = _na(q, k, v, l, na_bias, batch, seq)
        fou = _fourier(u, tables, batch, seq)
        xs, ffn2_w = _mixout(xs, l, mix_g, att, fou, w_in, gate_bias, w_na, w_f, w_o,
                             cast=[(ffn2_w_in, l), (ffn2_w_out, l)])
        last = l == depth - 1
        xs, ffn1_w = _ffn(xs, l, g2, *ffn2_w, final_g=final_norm if last else None,
                          cast=[] if last else [(ffn1_w_in, l + 1), (ffn1_w_out, l + 1)])
    return xs.reshape(batch, seq, d)
```

```python
import functools
import math

import jax
import jax.numpy as jnp
import numpy as np
from jax import lax
from jax.experimental import pallas as pl
from jax.experimental.pallas import tpu as pltpu

D_MODEL = 1024
GRID_W = 64
NA_HEADS = 8
NA_HEAD_DIM = 64
NA_WIDTH = NA_HEADS * NA_HEAD_DIM
NA_ROWS = 8
NA_COLS = 16
F_GROUPS = 4
F_GROUP_DIM = 128
F_WIDTH = F_GROUPS * F_GROUP_DIM
D_FF = 2816
RMS_EPS = 1e-6

BF16 = jnp.bfloat16
F32 = jnp.float32

MXU_COLS_V7X = 256
LANES = 128
VMEM_BYTES_V7X = 64 * 1024 * 1024

PROJ_TOKEN_TILE = 2048
TOKEN_TILE = 1024
SUB_TILE = 512
FFN_TOKEN_TILE = 1024
FFN_TOKEN_TILE_F32W = 512
FF_CHUNK = MXU_COLS_V7X
NA_ROW_BLOCK = 16
DFT_ROWS = 64
DFT_COLS = 128
SUBLANES_F32 = 8
BF16_SUBLANES = 16
F1_T2_BLOCK = SUBLANES_F32
F1_T2_STEP = 32
F3_K1_BLOCK = 16
F1_STEPS = DFT_COLS // F1_T2_STEP
F3_STEPS = DFT_ROWS // F3_K1_BLOCK


def _params(vmem_mib):
    assert vmem_mib * 1024 * 1024 <= VMEM_BYTES_V7X
    return pltpu.CompilerParams(
        dimension_semantics=None,
        vmem_limit_bytes=vmem_mib * 1024 * 1024,
    )


def _const_spec(shape, col_block=0):
    index = (0,) * (len(shape) - 1) + (col_block,)
    return pl.BlockSpec(shape, lambda *_: index, pipeline_mode=pl.Buffered(1))


def _cast_specs(params, steps):
    in_specs, out_specs, out_shapes = [], [], []
    for arr, layer in params:
        _, rows, cols = arr.shape
        nblk = steps
        while rows % nblk or (rows // nblk) % BF16_SUBLANES:
            nblk //= 2
        blk = rows // nblk
        in_specs.append(pl.BlockSpec(
            (None, blk, cols), lambda i, layer=layer, nblk=nblk: (layer, jnp.minimum(i, nblk - 1), 0)))
        out_specs.append(pl.BlockSpec(
            (blk, cols), lambda i, nblk=nblk: (jnp.minimum(i, nblk - 1), 0)))
        out_shapes.append(jax.ShapeDtypeStruct((rows, cols), BF16))
    return in_specs, out_specs, out_shapes


def _cast_blocks(src_refs, dst_refs):
    for src, dst in zip(src_refs, dst_refs):
        dst[...] = src[...].astype(BF16)


def _layer_spec(shape, layer):
    index = (layer,) + (0,) * len(shape)
    return pl.BlockSpec((None,) + tuple(shape), lambda *_: index, pipeline_mode=pl.Buffered(1))


def _rms(x, g):
    ms = jnp.mean(x * x, axis=-1, keepdims=True)
    return x * lax.rsqrt(ms + RMS_EPS) * g


def _dot(a, b):
    return jnp.dot(a, b, preferred_element_type=F32)


def _ffn_kernel(x_ref, g_ref, win_ref, wout_ref, *rest, final, n_cast):
    rest = list(rest)
    gf_ref = rest.pop(0) if final else None
    cast_src, o_ref, cast_dst, act_ref = (rest[:n_cast], rest[n_cast],
                                          rest[n_cast + 1:2 * n_cast + 1], rest[-1])
    _cast_blocks(cast_src, cast_dst)
    x = x_ref[...]
    h = _rms(x, g_ref[...]).astype(BF16)
    for c in range(D_FF // FF_CHUNK):
        lo = c * FF_CHUNK
        g = _dot(h, win_ref[:, lo:lo + FF_CHUNK].astype(BF16))
        u = _dot(h, win_ref[:, D_FF + lo:D_FF + lo + FF_CHUNK].astype(BF16))
        act_ref[:, lo:lo + FF_CHUNK] = (g * jax.nn.sigmoid(g) * u).astype(BF16)
    y = x + 0.5 * _dot(act_ref[...], wout_ref[...].astype(BF16))
    if final:
        y = _rms(y, gf_ref[...])
    o_ref[...] = y


def _ffn(x, layer, g, w_in, w_out, final_g=None, cast=()):
    n = x.shape[0]
    f32_weights = w_in.ndim == 3
    rows = FFN_TOKEN_TILE_F32W if f32_weights else FFN_TOKEN_TILE
    steps = n // rows
    final = final_g is not None
    tile = pl.BlockSpec((rows, D_MODEL), lambda i: (i, 0))
    w_spec = functools.partial(_layer_spec, layer=layer) if f32_weights else _const_spec
    in_specs = [tile, _layer_spec((1, D_MODEL), layer), w_spec((D_MODEL, 2 * D_FF)),
                w_spec((D_FF, D_MODEL))]
    args = [x, g, w_in, w_out]
    if final:
        in_specs.append(_const_spec((1, D_MODEL)))
        args.append(final_g.reshape(1, D_MODEL))
    cast_in, cast_out, cast_shapes = _cast_specs(cast, steps)
    y, *copies = pl.pallas_call(
        functools.partial(_ffn_kernel, final=final, n_cast=len(cast)),
        grid=(steps,),
        in_specs=in_specs + cast_in,
        out_specs=[tile] + cast_out,
        out_shape=[jax.ShapeDtypeStruct((n, D_MODEL), F32)] + cast_shapes,
        scratch_shapes=[pltpu.VMEM((rows, D_FF), BF16)],
        compiler_params=_params(58),
        name="ffn",
    )(*args, *[arr for arr, _ in cast])
    return y, copies


def _proj_kernel(x_ref, g_ref, w_ref, q_ref, k_ref, v_ref, u_ref):
    s = NA_WIDTH
    for t in range(PROJ_TOKEN_TILE // SUB_TILE):
        rows = slice(t * SUB_TILE, (t + 1) * SUB_TILE)
        h = _rms(x_ref[rows, :], g_ref[...]).astype(BF16)
        z = _dot(h, w_ref[...])
        q_ref[rows, :] = (z[:, :s] * (NA_HEAD_DIM ** -0.5)).astype(BF16)
        k_ref[rows, :] = z[:, s:2 * s].astype(BF16)
        v_ref[rows, :] = z[:, 2 * s:3 * s].astype(BF16)
        u_ref[rows, :] = z[:, 3 * s:3 * s + F_WIDTH]


def _proj(x, layer, g, w):
    n = x.shape[0]
    cols = 3 * NA_WIDTH + F_WIDTH
    assert w.shape[1] == 2 * cols
    out_tile = pl.BlockSpec((PROJ_TOKEN_TILE, NA_WIDTH), lambda i: (i, 0))
    out_sds = jax.ShapeDtypeStruct((n, NA_WIDTH), BF16)
    return pl.pallas_call(
        _proj_kernel,
        grid=(n // PROJ_TOKEN_TILE,),
        in_specs=[pl.BlockSpec((PROJ_TOKEN_TILE, D_MODEL), lambda i: (i, 0)),
                  _layer_spec((1, D_MODEL), layer), _const_spec((D_MODEL, cols), 0)],
        out_specs=[out_tile] * 4,
        out_shape=[out_sds] * 3 + [jax.ShapeDtypeStruct((n, F_WIDTH), F32)],
        compiler_params=_params(48),
        name="mix_proj",
    )(x, g, w)


def _na_bias_tables(rpb):
    ncol = 2 * NA_COLS - 1
    qc = jnp.arange(GRID_W)
    lane = jnp.arange(2 * GRID_W)
    kc, side = lane % GRID_W, lane // GRID_W
    ws = jnp.clip(qc - NA_COLS // 2, 0, GRID_W - NA_COLS)
    ok = (kc[None, :] >= ws[:, None]) & (kc[None, :] < ws[:, None] + NA_COLS)
    dc = kc[None, :] - qc[:, None] + (NA_COLS - 1)
    src = jnp.where(ok, side[None, :] * ncol + dc, -1)
    pick = (src[:, :, None] == jnp.arange(2 * ncol)[None, None, :]).astype(F32)
    both = jnp.concatenate([rpb[:, :, :-1], rpb[:, :, 1:]], axis=-1).astype(F32)
    table = jnp.einsum('lhdm,qkm->lhdqk', both, pick, precision=lax.Precision.HIGHEST)
    return jnp.where(ok[None, None, None], table, -jnp.inf)


def _na_kernel(q_ref, k_ref, v_ref, bias_ref, hm_ref, o_ref, *, rows):
    blk = pl.program_id(1)
    kh = NA_ROWS
    half = kh // 2
    lane_lo = lax.broadcasted_iota(jnp.int32, (GRID_W, LANES), 1) < NA_HEAD_DIM

    def window(i):
        r = blk * NA_ROW_BLOCK + i
        rs = jnp.clip(r - half, 0, rows - kh)
        dr0 = rs - r + (NA_ROWS - 1)
        return pl.multiple_of(rs * GRID_W, GRID_W), dr0

    def bias(h, dr0):
        return jnp.concatenate([bias_ref[h, dr0 + 2 * j] for j in range(kh // 2)], axis=-1)

    def score_stage(i):
        koff, _ = window(i)
        ss = []
        for p in range(NA_HEADS // 2):
            lanes = slice(p * LANES, (p + 1) * LANES)
            qp = q_ref[i * GRID_W:(i + 1) * GRID_W, lanes]
            ql = jnp.concatenate([qp * hm_ref[0], qp * hm_ref[1]], axis=0)
            kw = k_ref[0, pl.ds(koff, kh * GRID_W), lanes]
            ss.append(lax.dot_general(ql, kw, (((1,), (1,)), ((), ())),
                                      preferred_element_type=F32))
        return ss

    def output_stage(i, ss):
        koff, dr0 = window(i)
        es, ls = [], []
        for p in range(NA_HEADS // 2):
            s = ss[p] + jnp.concatenate([bias(2 * p, dr0), bias(2 * p + 1, dr0)], axis=0)
            m = jnp.max(s, axis=-1, keepdims=True)
            e = jnp.exp((s - m).astype(BF16))
            ls.append(jnp.sum(e.astype(F32), axis=-1, keepdims=True))
            es.append(e)
        for p in range(NA_HEADS // 2):
            lanes = slice(p * LANES, (p + 1) * LANES)
            vw = v_ref[0, pl.ds(koff, kh * GRID_W), lanes]
            pv = _dot(es[p], vw) / ls[p]
            o = jnp.where(lane_lo, pv[:GRID_W], pv[GRID_W:])
            o_ref[i * GRID_W:(i + 1) * GRID_W, lanes] = o.astype(BF16)

    ss = score_stage(0)
    for i in range(NA_ROW_BLOCK):
        nxt = score_stage(i + 1) if i + 1 < NA_ROW_BLOCK else None
        output_stage(i, ss)
        ss = nxt


def _na(q, k, v, layer, bias, batch, seq):
    rows = seq // GRID_W
    assert rows >= NA_ROWS and NA_ROWS % 2 == 0
    blk_tokens = NA_ROW_BLOCK * GRID_W
    nblk = rows // NA_ROW_BLOCK
    assert 2 * NA_HEAD_DIM == LANES
    head_mask = (jnp.arange(LANES)[None, None, :] // NA_HEAD_DIM
                 == jnp.arange(2)[:, None, None]).astype(BF16)
    head_mask = jnp.broadcast_to(head_mask, (2, GRID_W, LANES))
    q_tile = pl.BlockSpec((blk_tokens, NA_WIDTH), lambda b, i: (b * nblk + i, 0))
    kv_spec = pl.BlockSpec((1, seq, NA_WIDTH), lambda b, i: (b, 0, 0))
    return pl.pallas_call(
        functools.partial(_na_kernel, rows=rows),
        grid=(batch, nblk),
        in_specs=[q_tile, kv_spec, kv_spec,
                  _layer_spec((NA_HEADS, 2 * NA_ROWS - 2, GRID_W, 2 * GRID_W), layer),
                  _const_spec((2, GRID_W, LANES))],
        out_specs=q_tile,
        out_shape=jax.ShapeDtypeStruct((batch * seq, NA_WIDTH), BF16),
        compiler_params=_params(56),
        name="na",
    )(q, k.reshape(batch, seq, NA_WIDTH), v.reshape(batch, seq, NA_WIDTH), bias, head_mask)


def _dft_tables():
    two_pi = 2.0 * math.pi
    n = DFT_ROWS * DFT_COLS
    k1 = np.arange(DFT_ROWS)
    ang = ((k1[:, None] * k1[None, :]) % DFT_ROWS) * (two_pi / DFT_ROWS)
    eye_t2 = np.eye(F1_T2_BLOCK)
    f1 = np.concatenate([np.kron(np.cos(ang), eye_t2), np.kron(-np.sin(ang), eye_t2)], axis=0)
    k2 = np.arange(DFT_COLS)
    t2 = np.arange(DFT_COLS)
    ang = ((t2[None, None, :] * (DFT_ROWS * k2[None, :, None] + k1[:, None, None])) % n) * (two_pi / n)
    gr, gi = np.cos(ang), -np.sin(ang)
    g = np.concatenate([np.concatenate([gr, -gi], axis=2),
                        np.concatenate([gi, gr], axis=2)], axis=1)
    c = np.arange(F_GROUP_DIM)
    ang = ((c[:, None] * c[None, :]) % F_GROUP_DIM) * (two_pi / F_GROUP_DIM)
    eye = np.eye(MXU_COLS_V7X // F_GROUP_DIM)
    cc = np.kron(eye, np.cos(ang))
    ss = np.kron(eye, np.sin(ang))
    return tuple(jnp.asarray(t.astype(np.float32)).astype(BF16) for t in (f1, g, cc, ss))


def _fnet_kernel(f_ref, x_ref, g_ref, cc_ref, ss_ref, o_ref, a_ref, zr_ref, zi_ref):
    step = pl.program_id(1)

    @pl.when(step < F1_STEPS)
    def _stage1():
        per_store = BF16_SUBLANES // F1_T2_BLOCK
        base = step * F1_T2_STEP
        for s in range(F1_T2_STEP // BF16_SUBLANES):
            zs = []
            for j in range(per_store):
                lo = (s * per_store + j) * F1_T2_BLOCK
                x = x_ref[0, :, lo:lo + F1_T2_BLOCK, :]
                x = x.reshape(DFT_ROWS * F1_T2_BLOCK, F_WIDTH).astype(BF16)
                z = _dot(f_ref[...], x)
                zs.append(z.reshape(2, DFT_ROWS, F1_T2_BLOCK, F_WIDTH))
            t2 = pl.multiple_of(base + s * BF16_SUBLANES, BF16_SUBLANES)
            a_ref[:, :, pl.ds(t2, BF16_SUBLANES), :] = jnp.concatenate(zs, axis=2).astype(BF16)

    @pl.when(step >= F1_STEPS)
    def _stage2():
        k1 = (step - F1_STEPS) * F3_K1_BLOCK
        for j in range(F3_K1_BLOCK):
            a = jnp.concatenate([a_ref[0, k1 + j], a_ref[1, k1 + j]], axis=0)
            z = _dot(g_ref[j], a)
            zr_ref[j * DFT_COLS:(j + 1) * DFT_COLS] = z[:DFT_COLS].astype(BF16)
            zi_ref[j * DFT_COLS:(j + 1) * DFT_COLS] = z[DFT_COLS:].astype(BF16)
        scale = 1.0 / math.sqrt(DFT_ROWS * DFT_COLS * F_GROUP_DIM)
        for p in range(F_WIDTH // MXU_COLS_V7X):
            cols = slice(p * MXU_COLS_V7X, (p + 1) * MXU_COLS_V7X)
            y = _dot(zr_ref[:, cols], cc_ref[...]) + _dot(zi_ref[:, cols], ss_ref[...])
            y = (y * scale).astype(BF16)
            for j in range(F3_K1_BLOCK):
                o_ref[0, j, :, cols] = y[j * DFT_COLS:(j + 1) * DFT_COLS]


def _fourier(u, tables, batch, seq):
    f1, g, cc, ss = tables
    f1_rows = 2 * DFT_ROWS * F1_T2_BLOCK
    y = pl.pallas_call(
        _fnet_kernel,
        grid=(batch, F1_STEPS + F3_STEPS),
        in_specs=[_const_spec((f1_rows, f1_rows // 2)),
                  pl.BlockSpec((1, DFT_ROWS, F1_T2_STEP, F_WIDTH),
                               lambda b, i: (b, 0, jnp.minimum(i, F1_STEPS - 1), 0)),
                  pl.BlockSpec((F3_K1_BLOCK, 2 * DFT_COLS, 2 * DFT_COLS),
                               lambda b, i: (jnp.maximum(i - F1_STEPS, 0), 0, 0)),
                  _const_spec((MXU_COLS_V7X, MXU_COLS_V7X)), _const_spec((MXU_COLS_V7X, MXU_COLS_V7X))],
        out_specs=pl.BlockSpec((1, F3_K1_BLOCK, DFT_COLS, F_WIDTH),
                               lambda b, i: (b, jnp.maximum(i - F1_STEPS, 0), 0, 0)),
        out_shape=jax.ShapeDtypeStruct((batch, DFT_ROWS, DFT_COLS, F_WIDTH), BF16),
        scratch_shapes=[pltpu.VMEM((2, DFT_ROWS, DFT_COLS, F_WIDTH), BF16),
                        pltpu.VMEM((F3_K1_BLOCK * DFT_COLS, F_WIDTH), BF16),
                        pltpu.VMEM((F3_K1_BLOCK * DFT_COLS, F_WIDTH), BF16)],
        compiler_params=_params(52),
        name="fnet",
    )(f1, u.reshape(batch, DFT_ROWS, DFT_COLS, F_WIDTH), g, cc, ss)
    return jnp.transpose(y, (0, 2, 1, 3)).reshape(batch * seq, F_WIDTH)


def _mixout_kernel(x_ref, g_ref, att_ref, fou_ref, wg_ref, gb_ref, wna_ref, wf_ref, wo_ref,
                   *rest, n_cast):
    cast_src, o_ref, cast_dst = rest[:n_cast], rest[n_cast], rest[n_cast + 1:]
    _cast_blocks(cast_src, cast_dst)
    for t in range(TOKEN_TILE // SUB_TILE):
        rows = slice(t * SUB_TILE, (t + 1) * SUB_TILE)
        x = x_ref[rows, :]
        h = _rms(x, g_ref[...]).astype(BF16)
        gates = jax.nn.sigmoid(_dot(h, wg_ref[...]) + gb_ref[...])
        y_na = _dot(att_ref[rows, :], wna_ref[...])
        y_f = _dot(fou_ref[rows, :], wf_ref[...])
        m = gates[:, :D_MODEL] * y_na + gates[:, D_MODEL:] * y_f
        o_ref[rows, :] = x + _dot(m.astype(BF16), wo_ref[...])


def _mixout(x, layer, g, att, fou, w_in, gate_bias, w_na, w_f, w_o, cast=()):
    n = x.shape[0]
    steps = n // TOKEN_TILE
    tile = pl.BlockSpec((TOKEN_TILE, D_MODEL), lambda i: (i, 0))
    half = pl.BlockSpec((TOKEN_TILE, NA_WIDTH), lambda i: (i, 0))
    assert w_in.shape[1] == 4 * D_MODEL
    cast_in, cast_out, cast_shapes = _cast_specs(cast, steps)
    y, *copies = pl.pallas_call(
        functools.partial(_mixout_kernel, n_cast=len(cast)),
        grid=(steps,),
        in_specs=[tile, _layer_spec((1, D_MODEL), layer), half, half,
                  _const_spec((D_MODEL, 2 * D_MODEL), 1),
                  _layer_spec((1, 2 * D_MODEL), layer),
                  _const_spec((NA_WIDTH, D_MODEL)), _const_spec((F_WIDTH, D_MODEL)),
                  _const_spec((D_MODEL, D_MODEL))] + cast_in,
        out_specs=[tile] + cast_out,
        out_shape=[jax.ShapeDtypeStruct((n, D_MODEL), F32)] + cast_shapes,
        compiler_params=_params(48),
        name="mix_out",
    )(x, g, att, fou, w_in, gate_bias, w_na, w_f, w_o, *[arr for arr, _ in cast])
    return y, copies


def kernel(x, ffn1_norm, ffn1_w_in, ffn1_w_out, mix_norm, mix_w_in, mix_gate_bias, na_rpb,
           na_w_out, f_w_out, mix_w_o, ffn2_norm, ffn2_w_in, ffn2_w_out, final_norm):
    batch, seq, d = x.shape
    depth = ffn1_norm.shape[0]
    assert d == D_MODEL and seq == DFT_ROWS * DFT_COLS and seq % (GRID_W * NA_ROW_BLOCK) == 0
    assert all((batch * seq) % t == 0
               for t in (PROJ_TOKEN_TILE, TOKEN_TILE, FFN_TOKEN_TILE, FFN_TOKEN_TILE_F32W))
    tables = _dft_tables()
    na_bias = _na_bias_tables(na_rpb)
    gain = lambda g: g.reshape(depth, 1, D_MODEL)
    g1, g2, mix_g = gain(ffn1_norm), gain(ffn2_norm), gain(mix_norm)
    gate_bias = mix_gate_bias.reshape(depth, 1, 2 * D_MODEL)
    xs = x.reshape(batch * seq, d)
    ffn1_w = [ffn1_w_in, ffn1_w_out]
    for l in range(depth):
        xs, (w_in, w_na, w_f, w_o) = _ffn(
            xs, l, g1, *ffn1_w,
            cast=[(mix_w_in, l), (na_w_out, l), (f_w_out, l), (mix_w_o, l)])
        q, k, v, u = _proj(xs, l, mix_g, w_in)
        att = _na(q, k, v, l, na_bias, batch, seq)
        fou = _fourier(u, tables, batch, seq)
        xs, ffn2_w = _mixout(xs, l, mix_g, att, fou, w_in, gate_bias, w_na, w_f, w_o,
                             cast=[(ffn2_w_in, l), (ffn2_w_out, l)])
        last = l == depth - 1
        xs, ffn1_w = _ffn(xs, l, g2, *ffn2_w, final_g=final_norm if last else None,
                          cast=[] if last else [(ffn1_w_in, l + 1), (ffn1_w_out, l + 1)])
    return xs.reshape(batch, seq, d)
```

```python
import functools
import math

import jax
import jax.numpy as jnp
import numpy as np
from jax import lax
from jax.experimental import pallas as pl
from jax.experimental.pallas import tpu as pltpu

D_MODEL = 1024
GRID_W = 64
NA_HEADS = 8
NA_HEAD_DIM = 64
NA_WIDTH = NA_HEADS * NA_HEAD_DIM
NA_ROWS = 8
NA_COLS = 16
F_GROUPS = 4
F_GROUP_DIM = 128
F_WIDTH = F_GROUPS * F_GROUP_DIM
D_FF = 2816
RMS_EPS = 1e-6

BF16 = jnp.bfloat16
F32 = jnp.float32

MXU_COLS_V7X = 256
LANES = 128
VMEM_BYTES_V7X = 64 * 1024 * 1024

PROJ_TOKEN_TILE = 2048
TOKEN_TILE = 1024
SUB_TILE = 512
FFN_TOKEN_TILE = 1024
FFN_TOKEN_TILE_F32W = 512
FF_CHUNK = MXU_COLS_V7X
NA_ROW_BLOCK = 32
DFT_ROWS = 64
DFT_COLS = 128
SUBLANES_F32 = 8
BF16_SUBLANES = 16
F1_T2_BLOCK = SUBLANES_F32
F1_T2_STEP = 32
F3_K1_BLOCK = 16
F1_STEPS = DFT_COLS // F1_T2_STEP
F3_STEPS = DFT_ROWS // F3_K1_BLOCK


def _params(vmem_mib):
    assert vmem_mib * 1024 * 1024 <= VMEM_BYTES_V7X
    return pltpu.CompilerParams(
        dimension_semantics=None,
        vmem_limit_bytes=vmem_mib * 1024 * 1024,
    )


def _const_spec(shape, col_block=0):
    index = (0,) * (len(shape) - 1) + (col_block,)
    return pl.BlockSpec(shape, lambda *_: index, pipeline_mode=pl.Buffered(1))


def _cast_specs(params, steps):
    in_specs, out_specs, out_shapes = [], [], []
    for arr, layer in params:
        _, rows, cols = arr.shape
        nblk = steps
        while rows % nblk or (rows // nblk) % BF16_SUBLANES:
            nblk //= 2
        blk = rows // nblk
        in_specs.append(pl.BlockSpec(
            (None, blk, cols), lambda i, layer=layer, nblk=nblk: (layer, jnp.minimum(i, nblk - 1), 0)))
        out_specs.append(pl.BlockSpec(
            (blk, cols), lambda i, nblk=nblk: (jnp.minimum(i, nblk - 1), 0)))
        out_shapes.append(jax.ShapeDtypeStruct((rows, cols), BF16))
    return in_specs, out_specs, out_shapes


def _cast_blocks(src_refs, dst_refs):
    for src, dst in zip(src_refs, dst_refs):
        dst[...] = src[...].astype(BF16)


def _layer_spec(shape, layer):
    index = (layer,) + (0,) * len(shape)
    return pl.BlockSpec((None,) + tuple(shape), lambda *_: index, pipeline_mode=pl.Buffered(1))


def _rms(x, g):
    ms = jnp.mean(x * x, axis=-1, keepdims=True)
    return x * lax.rsqrt(ms + RMS_EPS) * g


def _dot(a, b):
    return jnp.dot(a, b, preferred_element_type=F32)


def _ffn_kernel(x_ref, g_ref, win_ref, wout_ref, *rest, final, n_cast):
    rest = list(rest)
    gf_ref = rest.pop(0) if final else None
    cast_src, o_ref, cast_dst, act_ref = (rest[:n_cast], rest[n_cast],
                                          rest[n_cast + 1:2 * n_cast + 1], rest[-1])
    _cast_blocks(cast_src, cast_dst)
    x = x_ref[...]
    h = _rms(x, g_ref[...]).astype(BF16)
    for c in range(D_FF // FF_CHUNK):
        lo = c * FF_CHUNK
        g = _dot(h, win_ref[:, lo:lo + FF_CHUNK].astype(BF16))
        u = _dot(h, win_ref[:, D_FF + lo:D_FF + lo + FF_CHUNK].astype(BF16))
        act_ref[:, lo:lo + FF_CHUNK] = (g * jax.nn.sigmoid(g) * u).astype(BF16)
    y = x + 0.5 * _dot(act_ref[...], wout_ref[...].astype(BF16))
    if final:
        y = _rms(y, gf_ref[...])
    o_ref[...] = y


def _ffn(x, layer, g, w_in, w_out, final_g=None, cast=()):
    n = x.shape[0]
    f32_weights = w_in.ndim == 3
    rows = FFN_TOKEN_TILE_F32W if f32_weights else FFN_TOKEN_TILE
    steps = n // rows
    final = final_g is not None
    tile = pl.BlockSpec((rows, D_MODEL), lambda i: (i, 0))
    w_spec = functools.partial(_layer_spec, layer=layer) if f32_weights else _const_spec
    in_specs = [tile, _layer_spec((1, D_MODEL), layer), w_spec((D_MODEL, 2 * D_FF)),
                w_spec((D_FF, D_MODEL))]
    args = [x, g, w_in, w_out]
    if final:
        in_specs.append(_const_spec((1, D_MODEL)))
        args.append(final_g.reshape(1, D_MODEL))
    cast_in, cast_out, cast_shapes = _cast_specs(cast, steps)
    y, *copies = pl.pallas_call(
        functools.partial(_ffn_kernel, final=final, n_cast=len(cast)),
        grid=(steps,),
        in_specs=in_specs + cast_in,
        out_specs=[tile] + cast_out,
        out_shape=[jax.ShapeDtypeStruct((n, D_MODEL), F32)] + cast_shapes,
        scratch_shapes=[pltpu.VMEM((rows, D_FF), BF16)],
        compiler_params=_params(58),
        name="ffn",
    )(*args, *[arr for arr, _ in cast])
    return y, copies


def _proj_kernel(x_ref, g_ref, w_ref, q_ref, k_ref, v_ref, u_ref):
    s = NA_WIDTH
    for t in range(PROJ_TOKEN_TILE // SUB_TILE):
        rows = slice(t * SUB_TILE, (t + 1) * SUB_TILE)
        h = _rms(x_ref[rows, :], g_ref[...]).astype(BF16)
        z = _dot(h, w_ref[...])
        q_ref[rows, :] = (z[:, :s] * (NA_HEAD_DIM ** -0.5)).astype(BF16)
        k_ref[rows, :] = z[:, s:2 * s].astype(BF16)
        v_ref[rows, :] = z[:, 2 * s:3 * s].astype(BF16)
        u_ref[rows, :] = z[:, 3 * s:3 * s + F_WIDTH]


def _proj(x, layer, g, w):
    n = x.shape[0]
    cols = 3 * NA_WIDTH + F_WIDTH
    assert w.shape[1] == 2 * cols
    out_tile = pl.BlockSpec((PROJ_TOKEN_TILE, NA_WIDTH), lambda i: (i, 0))
    out_sds = jax.ShapeDtypeStruct((n, NA_WIDTH), BF16)
    return pl.pallas_call(
        _proj_kernel,
        grid=(n // PROJ_TOKEN_TILE,),
        in_specs=[pl.BlockSpec((PROJ_TOKEN_TILE, D_MODEL), lambda i: (i, 0)),
                  _layer_spec((1, D_MODEL), layer), _const_spec((D_MODEL, cols), 0)],
        out_specs=[out_tile] * 4,
        out_shape=[out_sds] * 3 + [jax.ShapeDtypeStruct((n, F_WIDTH), F32)],
        compiler_params=_params(48),
        name="mix_proj",
    )(x, g, w)


def _na_bias_tables(rpb):
    ncol = 2 * NA_COLS - 1
    qc = jnp.arange(GRID_W)
    lane = jnp.arange(2 * GRID_W)
    kc, side = lane % GRID_W, lane // GRID_W
    ws = jnp.clip(qc - NA_COLS // 2, 0, GRID_W - NA_COLS)
    ok = (kc[None, :] >= ws[:, None]) & (kc[None, :] < ws[:, None] + NA_COLS)
    dc = kc[None, :] - qc[:, None] + (NA_COLS - 1)
    src = jnp.where(ok, side[None, :] * ncol + dc, -1)
    pick = (src[:, :, None] == jnp.arange(2 * ncol)[None, None, :]).astype(F32)
    both = jnp.concatenate([rpb[:, :, :-1], rpb[:, :, 1:]], axis=-1).astype(F32)
    table = jnp.einsum('lhdm,qkm->lhdqk', both, pick, precision=lax.Precision.HIGHEST)
    return jnp.where(ok[None, None, None], table, -jnp.inf)


def _na_kernel(q_ref, k_ref, v_ref, bias_ref, hm_ref, o_ref, *, rows):
    blk = pl.program_id(1)
    kh = NA_ROWS
    half = kh // 2
    lane_lo = lax.broadcasted_iota(jnp.int32, (GRID_W, LANES), 1) < NA_HEAD_DIM

    def window(i):
        r = blk * NA_ROW_BLOCK + i
        rs = jnp.clip(r - half, 0, rows - kh)
        dr0 = rs - r + (NA_ROWS - 1)
        return pl.multiple_of(rs * GRID_W, GRID_W), dr0

    def bias(h, dr0):
        return jnp.concatenate([bias_ref[h, dr0 + 2 * j] for j in range(kh // 2)], axis=-1)

    def score_stage(i):
        koff, _ = window(i)
        ss = []
        for p in range(NA_HEADS // 2):
            lanes = slice(p * LANES, (p + 1) * LANES)
            qp = q_ref[i * GRID_W:(i + 1) * GRID_W, lanes]
            ql = jnp.concatenate([qp * hm_ref[0], qp * hm_ref[1]], axis=0)
            kw = k_ref[0, pl.ds(koff, kh * GRID_W), lanes]
            ss.append(lax.dot_general(ql, kw, (((1,), (1,)), ((), ())),
                                      preferred_element_type=F32))
        return ss

    def output_stage(i, ss):
        koff, dr0 = window(i)
        es, ls = [], []
        for p in range(NA_HEADS // 2):
            s = ss[p] + jnp.concatenate([bias(2 * p, dr0), bias(2 * p + 1, dr0)], axis=0)
            m = jnp.max(s, axis=-1, keepdims=True)
            e = jnp.exp((s - m).astype(BF16))
            ls.append(jnp.sum(e.astype(F32), axis=-1, keepdims=True))
            es.append(e)
        for p in range(NA_HEADS // 2):
            lanes = slice(p * LANES, (p + 1) * LANES)
            vw = v_ref[0, pl.ds(koff, kh * GRID_W), lanes]
            pv = _dot(es[p], vw) / ls[p]
            o = jnp.where(lane_lo, pv[:GRID_W], pv[GRID_W:])
            o_ref[i * GRID_W:(i + 1) * GRID_W, lanes] = o.astype(BF16)

    ss = score_stage(0)
    for i in range(NA_ROW_BLOCK):
        nxt = score_stage(i + 1) if i + 1 < NA_ROW_BLOCK else None
        output_stage(i, ss)
        ss = nxt


def _na(q, k, v, layer, bias, batch, seq):
    rows = seq // GRID_W
    assert rows >= NA_ROWS and NA_ROWS % 2 == 0
    blk_tokens = NA_ROW_BLOCK * GRID_W
    nblk = rows // NA_ROW_BLOCK
    assert 2 * NA_HEAD_DIM == LANES
    head_mask = (jnp.arange(LANES)[None, None, :] // NA_HEAD_DIM
                 == jnp.arange(2)[:, None, None]).astype(BF16)
    head_mask = jnp.broadcast_to(head_mask, (2, GRID_W, LANES))
    q_tile = pl.BlockSpec((blk_tokens, NA_WIDTH), lambda b, i: (b * nblk + i, 0))
    kv_spec = pl.BlockSpec((1, seq, NA_WIDTH), lambda b, i: (b, 0, 0))
    return pl.pallas_call(
        functools.partial(_na_kernel, rows=rows),
        grid=(batch, nblk),
        in_specs=[q_tile, kv_spec, kv_spec,
                  _layer_spec((NA_HEADS, 2 * NA_ROWS - 2, GRID_W, 2 * GRID_W), layer),
                  _const_spec((2, GRID_W, LANES))],
        out_specs=q_tile,
        out_shape=jax.ShapeDtypeStruct((batch * seq, NA_WIDTH), BF16),
        compiler_params=_params(60),
        name="na",
    )(q, k.reshape(batch, seq, NA_WIDTH), v.reshape(batch, seq, NA_WIDTH), bias, head_mask)


def _dft_tables():
    two_pi = 2.0 * math.pi
    n = DFT_ROWS * DFT_COLS
    k1 = np.arange(DFT_ROWS)
    ang = ((k1[:, None] * k1[None, :]) % DFT_ROWS) * (two_pi / DFT_ROWS)
    eye_t2 = np.eye(F1_T2_BLOCK)
    f1 = np.concatenate([np.kron(np.cos(ang), eye_t2), np.kron(-np.sin(ang), eye_t2)], axis=0)
    k2 = np.arange(DFT_COLS)
    t2 = np.arange(DFT_COLS)
    ang = ((t2[None, None, :] * (DFT_ROWS * k2[None, :, None] + k1[:, None, None])) % n) * (two_pi / n)
    gr, gi = np.cos(ang), -np.sin(ang)
    g = np.concatenate([np.concatenate([gr, -gi], axis=2),
                        np.concatenate([gi, gr], axis=2)], axis=1)
    c = np.arange(F_GROUP_DIM)
    ang = ((c[:, None] * c[None, :]) % F_GROUP_DIM) * (two_pi / F_GROUP_DIM)
    eye = np.eye(MXU_COLS_V7X // F_GROUP_DIM)
    cc = np.kron(eye, np.cos(ang))
    ss = np.kron(eye, np.sin(ang))
    return tuple(jnp.asarray(t.astype(np.float32)).astype(BF16) for t in (f1, g, cc, ss))


def _fnet_kernel(f_ref, x_ref, g_ref, cc_ref, ss_ref, o_ref, a_ref, zr_ref, zi_ref):
    step = pl.program_id(1)

    @pl.when(step < F1_STEPS)
    def _stage1():
        per_store = BF16_SUBLANES // F1_T2_BLOCK
        base = step * F1_T2_STEP
        for s in range(F1_T2_STEP // BF16_SUBLANES):
            zs = []
            for j in range(per_store):
                lo = (s * per_store + j) * F1_T2_BLOCK
                x = x_ref[0, :, lo:lo + F1_T2_BLOCK, :]
                x = x.reshape(DFT_ROWS * F1_T2_BLOCK, F_WIDTH).astype(BF16)
                z = _dot(f_ref[...], x)
                zs.append(z.reshape(2, DFT_ROWS, F1_T2_BLOCK, F_WIDTH))
            t2 = pl.multiple_of(base + s * BF16_SUBLANES, BF16_SUBLANES)
            a_ref[:, :, pl.ds(t2, BF16_SUBLANES), :] = jnp.concatenate(zs, axis=2).astype(BF16)

    @pl.when(step >= F1_STEPS)
    def _stage2():
        k1 = (step - F1_STEPS) * F3_K1_BLOCK
        for j in range(F3_K1_BLOCK):
            a = jnp.concatenate([a_ref[0, k1 + j], a_ref[1, k1 + j]], axis=0)
            z = _dot(g_ref[j], a)
            zr_ref[j * DFT_COLS:(j + 1) * DFT_COLS] = z[:DFT_COLS].astype(BF16)
            zi_ref[j * DFT_COLS:(j + 1) * DFT_COLS] = z[DFT_COLS:].astype(BF16)
        scale = 1.0 / math.sqrt(DFT_ROWS * DFT_COLS * F_GROUP_DIM)
        for p in range(F_WIDTH // MXU_COLS_V7X):
            cols = slice(p * MXU_COLS_V7X, (p + 1) * MXU_COLS_V7X)
            y = _dot(zr_ref[:, cols], cc_ref[...]) + _dot(zi_ref[:, cols], ss_ref[...])
            y = (y * scale).astype(BF16)
            for j in range(F3_K1_BLOCK):
                o_ref[0, j, :, cols] = y[j * DFT_COLS:(j + 1) * DFT_COLS]


def _fourier(u, tables, batch, seq):
    f1, g, cc, ss = tables
    f1_rows = 2 * DFT_ROWS * F1_T2_BLOCK
    y = pl.pallas_call(
        _fnet_kernel,
        grid=(batch, F1_STEPS + F3_STEPS),
        in_specs=[_const_spec((f1_rows, f1_rows // 2)),
                  pl.BlockSpec((1, DFT_ROWS, F1_T2_STEP, F_WIDTH),
                               lambda b, i: (b, 0, jnp.minimum(i, F1_STEPS - 1), 0)),
                  pl.BlockSpec((F3_K1_BLOCK, 2 * DFT_COLS, 2 * DFT_COLS),
                               lambda b, i: (jnp.maximum(i - F1_STEPS, 0), 0, 0)),
                  _const_spec((MXU_COLS_V7X, MXU_COLS_V7X)), _const_spec((MXU_COLS_V7X, MXU_COLS_V7X))],
        out_specs=pl.BlockSpec((1, F3_K1_BLOCK, DFT_COLS, F_WIDTH),
                               lambda b, i: (b, jnp.maximum(i - F1_STEPS, 0), 0, 0)),
        out_shape=jax.ShapeDtypeStruct((batch, DFT_ROWS, DFT_COLS, F_WIDTH), BF16),
        scratch_shapes=[pltpu.VMEM((2, DFT_ROWS, DFT_COLS, F_WIDTH), BF16),
                        pltpu.VMEM((F3_K1_BLOCK * DFT_COLS, F_WIDTH), BF16),
                        pltpu.VMEM((F3_K1_BLOCK * DFT_COLS, F_WIDTH), BF16)],
        compiler_params=_params(52),
        name="fnet",
    )(f1, u.reshape(batch, DFT_ROWS, DFT_COLS, F_WIDTH), g, cc, ss)
    return jnp.transpose(y, (0, 2, 1, 3)).reshape(batch * seq, F_WIDTH)


def _mixout_kernel(x_ref, g_ref, att_ref, fou_ref, wg_ref, gb_ref, wna_ref, wf_ref, wo_ref,
                   *rest, n_cast):
    cast_src, o_ref, cast_dst = rest[:n_cast], rest[n_cast], rest[n_cast + 1:]
    _cast_blocks(cast_src, cast_dst)
    for t in range(TOKEN_TILE // SUB_TILE):
        rows = slice(t * SUB_TILE, (t + 1) * SUB_TILE)
        x = x_ref[rows, :]
        h = _rms(x, g_ref[...]).astype(BF16)
        gates = jax.nn.sigmoid(_dot(h, wg_ref[...]) + gb_ref[...])
        y_na = _dot(att_ref[rows, :], wna_ref[...])
        y_f = _dot(fou_ref[rows, :], wf_ref[...])
        m = gates[:, :D_MODEL] * y_na + gates[:, D_MODEL:] * y_f
        o_ref[rows, :] = x + _dot(m.astype(BF16), wo_ref[...])


def _mixout(x, layer, g, att, fou, w_in, gate_bias, w_na, w_f, w_o, cast=()):
    n = x.shape[0]
    steps = n // TOKEN_TILE
    tile = pl.BlockSpec((TOKEN_TILE, D_MODEL), lambda i: (i, 0))
    half = pl.BlockSpec((TOKEN_TILE, NA_WIDTH), lambda i: (i, 0))
    assert w_in.shape[1] == 4 * D_MODEL
    cast_in, cast_out, cast_shapes = _cast_specs(cast, steps)
    y, *copies = pl.pallas_call(
        functools.partial(_mixout_kernel, n_cast=len(cast)),
        grid=(steps,),
        in_specs=[tile, _layer_spec((1, D_MODEL), layer), half, half,
                  _const_spec((D_MODEL, 2 * D_MODEL), 1),
                  _layer_spec((1, 2 * D_MODEL), layer),
                  _const_spec((NA_WIDTH, D_MODEL)), _const_spec((F_WIDTH, D_MODEL)),
                  _const_spec((D_MODEL, D_MODEL))] + cast_in,
        out_specs=[tile] + cast_out,
        out_shape=[jax.ShapeDtypeStruct((n, D_MODEL), F32)] + cast_shapes,
        compiler_params=_params(48),
        name="mix_out",
    )(x, g, att, fou, w_in, gate_bias, w_na, w_f, w_o, *[arr for arr, _ in cast])
    return y, copies


def kernel(x, ffn1_norm, ffn1_w_in, ffn1_w_out, mix_norm, mix_w_in, mix_gate_bias, na_rpb,
           na_w_out, f_w_out, mix_w_o, ffn2_norm, ffn2_w_in, ffn2_w_out, final_norm):
    batch, seq, d = x.shape
    depth = ffn1_norm.shape[0]
    assert d == D_MODEL and seq == DFT_ROWS * DFT_COLS and seq % (GRID_W * NA_ROW_BLOCK) == 0
    assert all((batch * seq) % t == 0
               for t in (PROJ_TOKEN_TILE, TOKEN_TILE, FFN_TOKEN_TILE, FFN_TOKEN_TILE_F32W))
    tables = _dft_tables()
    na_bias = _na_bias_tables(na_rpb)
    gain = lambda g: g.reshape(depth, 1, D_MODEL)
    g1, g2, mix_g = gain(ffn1_norm), gain(ffn2_norm), gain(mix_norm)
    gate_bias = mix_gate_bias.reshape(depth, 1, 2 * D_MODEL)
    xs = x.reshape(batch * seq, d)
    ffn1_w = [ffn1_w_in, ffn1_w_out]
    for l in range(depth):
        xs, (w_in, w_na, w_f, w_o) = _ffn(
            xs, l, g1, *ffn1_w,
            cast=[(mix_w_in, l), (na_w_out, l), (f_w_out, l), (mix_w_o, l)])
        q, k, v, u = _proj(xs, l, mix_g, w_in)
        att = _na(q, k, v, l, na_bias, batch, seq)
        fou = _fourier(u, tables, batch, seq)
        xs, ffn2_w = _mixout(xs, l, mix_g, att, fou, w_in, gate_bias, w_na, w_f, w_o,
                             cast=[(ffn2_w_in, l), (ffn2_w_out, l)])
        last = l == depth - 1
        xs, ffn1_w = _ffn(xs, l, g2, *ffn2_w, final_g=final_norm if last else None,
                          cast=[] if last else [(ffn1_w_in, l + 1), (ffn1_w_out, l + 1)])
    return xs.reshape(batch, seq, d)
```

```python
import functools
import math

import jax
import jax.numpy as jnp
import numpy as np
from jax import lax
from jax.experimental import pallas as pl
from jax.experimental.pallas import tpu as pltpu

D_MODEL = 1024
GRID_W = 64
NA_HEADS = 8
NA_HEAD_DIM = 64
NA_WIDTH = NA_HEADS * NA_HEAD_DIM
NA_ROWS = 8
NA_COLS = 16
F_GROUPS = 4
F_GROUP_DIM = 128
F_WIDTH = F_GROUPS * F_GROUP_DIM
D_FF = 2816
RMS_EPS = 1e-6

BF16 = jnp.bfloat16
F32 = jnp.float32

MXU_COLS_V7X = 256
LANES = 128
VMEM_BYTES_V7X = 64 * 1024 * 1024

PROJ_TOKEN_TILE = 2048
TOKEN_TILE = 1024
SUB_TILE = 512
FFN_TOKEN_TILE = 1024
FFN_TOKEN_TILE_F32W = 512
FF_CHUNK = MXU_COLS_V7X
MIX_CHUNK = 2 * MXU_COLS_V7X
NA_ROW_BLOCK = 16
DFT_ROWS = 64
DFT_COLS = 128
SUBLANES_F32 = 8
BF16_SUBLANES = 16
F1_T2_BLOCK = SUBLANES_F32
F1_T2_STEP = 32
F3_K1_BLOCK = 16
F1_STEPS = DFT_COLS // F1_T2_STEP
F3_STEPS = DFT_ROWS // F3_K1_BLOCK


def _params(vmem_mib):
    assert vmem_mib * 1024 * 1024 <= VMEM_BYTES_V7X
    return pltpu.CompilerParams(
        dimension_semantics=None,
        vmem_limit_bytes=vmem_mib * 1024 * 1024,
    )


def _const_spec(shape, col_block=0):
    index = (0,) * (len(shape) - 1) + (col_block,)
    return pl.BlockSpec(shape, lambda *_: index, pipeline_mode=pl.Buffered(1))


def _cast_specs(params, steps):
    in_specs, out_specs, out_shapes = [], [], []
    for arr, layer in params:
        _, rows, cols = arr.shape
        nblk = steps
        while rows % nblk or (rows // nblk) % BF16_SUBLANES:
            nblk //= 2
        blk = rows // nblk
        in_specs.append(pl.BlockSpec(
            (None, blk, cols), lambda i, layer=layer, nblk=nblk: (layer, jnp.minimum(i, nblk - 1), 0)))
        out_specs.append(pl.BlockSpec(
            (blk, cols), lambda i, nblk=nblk: (jnp.minimum(i, nblk - 1), 0)))
        out_shapes.append(jax.ShapeDtypeStruct((rows, cols), BF16))
    return in_specs, out_specs, out_shapes


def _cast_blocks(src_refs, dst_refs):
    for src, dst in zip(src_refs, dst_refs):
        dst[...] = src[...].astype(BF16)


def _layer_spec(shape, layer):
    index = (layer,) + (0,) * len(shape)
    return pl.BlockSpec((None,) + tuple(shape), lambda *_: index, pipeline_mode=pl.Buffered(1))


def _rms(x, g):
    ms = jnp.mean(x * x, axis=-1, keepdims=True)
    return x * lax.rsqrt(ms + RMS_EPS) * g


def _dot(a, b):
    return jnp.dot(a, b, preferred_element_type=F32)


def _ffn_kernel(x_ref, g_ref, win_ref, wout_ref, *rest, final, n_cast):
    rest = list(rest)
    gf_ref = rest.pop(0) if final else None
    cast_src, o_ref, cast_dst, act_ref = (rest[:n_cast], rest[n_cast],
                                          rest[n_cast + 1:2 * n_cast + 1], rest[-1])
    _cast_blocks(cast_src, cast_dst)
    x = x_ref[...]
    h = _rms(x, g_ref[...]).astype(BF16)
    for c in range(D_FF // FF_CHUNK):
        lo = c * FF_CHUNK
        g = _dot(h, win_ref[:, lo:lo + FF_CHUNK].astype(BF16))
        u = _dot(h, win_ref[:, D_FF + lo:D_FF + lo + FF_CHUNK].astype(BF16))
        act_ref[:, lo:lo + FF_CHUNK] = (g * jax.nn.sigmoid(g) * u).astype(BF16)
    y = x + 0.5 * _dot(act_ref[...], wout_ref[...].astype(BF16))
    if final:
        y = _rms(y, gf_ref[...])
    o_ref[...] = y


def _ffn(x, layer, g, w_in, w_out, final_g=None, cast=()):
    n = x.shape[0]
    f32_weights = w_in.ndim == 3
    rows = FFN_TOKEN_TILE_F32W if f32_weights else FFN_TOKEN_TILE
    steps = n // rows
    final = final_g is not None
    tile = pl.BlockSpec((rows, D_MODEL), lambda i: (i, 0))
    w_spec = functools.partial(_layer_spec, layer=layer) if f32_weights else _const_spec
    in_specs = [tile, _layer_spec((1, D_MODEL), layer), w_spec((D_MODEL, 2 * D_FF)),
                w_spec((D_FF, D_MODEL))]
    args = [x, g, w_in, w_out]
    if final:
        in_specs.append(_const_spec((1, D_MODEL)))
        args.append(final_g.reshape(1, D_MODEL))
    cast_in, cast_out, cast_shapes = _cast_specs(cast, steps)
    y, *copies = pl.pallas_call(
        functools.partial(_ffn_kernel, final=final, n_cast=len(cast)),
        grid=(steps,),
        in_specs=in_specs + cast_in,
        out_specs=[tile] + cast_out,
        out_shape=[jax.ShapeDtypeStruct((n, D_MODEL), F32)] + cast_shapes,
        scratch_shapes=[pltpu.VMEM((rows, D_FF), BF16)],
        compiler_params=_params(58),
        name="ffn",
    )(*args, *[arr for arr, _ in cast])
    return y, copies


def _proj_kernel(x_ref, g_ref, w_ref, q_ref, k_ref, v_ref, u_ref):
    s = NA_WIDTH
    for t in range(PROJ_TOKEN_TILE // SUB_TILE):
        rows = slice(t * SUB_TILE, (t + 1) * SUB_TILE)
        h = _rms(x_ref[rows, :], g_ref[...]).astype(BF16)
        z = _dot(h, w_ref[...])
        q_ref[rows, :] = (z[:, :s] * (NA_HEAD_DIM ** -0.5)).astype(BF16)
        k_ref[rows, :] = z[:, s:2 * s].astype(BF16)
        v_ref[rows, :] = z[:, 2 * s:3 * s].astype(BF16)
        u_ref[rows, :] = z[:, 3 * s:3 * s + F_WIDTH]


def _proj(x, layer, g, w):
    n = x.shape[0]
    cols = 3 * NA_WIDTH + F_WIDTH
    assert w.shape[1] == 2 * cols
    out_tile = pl.BlockSpec((PROJ_TOKEN_TILE, NA_WIDTH), lambda i: (i, 0))
    out_sds = jax.ShapeDtypeStruct((n, NA_WIDTH), BF16)
    return pl.pallas_call(
        _proj_kernel,
        grid=(n // PROJ_TOKEN_TILE,),
        in_specs=[pl.BlockSpec((PROJ_TOKEN_TILE, D_MODEL), lambda i: (i, 0)),
                  _layer_spec((1, D_MODEL), layer), _const_spec((D_MODEL, cols), 0)],
        out_specs=[out_tile] * 4,
        out_shape=[out_sds] * 3 + [jax.ShapeDtypeStruct((n, F_WIDTH), F32)],
        compiler_params=_params(48),
        name="mix_proj",
    )(x, g, w)


def _na_bias_tables(rpb):
    ncol = 2 * NA_COLS - 1
    qc = jnp.arange(GRID_W)
    lane = jnp.arange(2 * GRID_W)
    kc, side = lane % GRID_W, lane // GRID_W
    ws = jnp.clip(qc - NA_COLS // 2, 0, GRID_W - NA_COLS)
    ok = (kc[None, :] >= ws[:, None]) & (kc[None, :] < ws[:, None] + NA_COLS)
    dc = kc[None, :] - qc[:, None] + (NA_COLS - 1)
    src = jnp.where(ok, side[None, :] * ncol + dc, -1)
    pick = (src[:, :, None] == jnp.arange(2 * ncol)[None, None, :]).astype(F32)
    both = jnp.concatenate([rpb[:, :, :-1], rpb[:, :, 1:]], axis=-1).astype(F32)
    table = jnp.einsum('lhdm,qkm->lhdqk', both, pick, precision=lax.Precision.HIGHEST)
    return jnp.where(ok[None, None, None], table, -jnp.inf)


def _na_kernel(q_ref, k_ref, v_ref, bias_ref, hm_ref, o_ref, *, rows):
    blk = pl.program_id(1)
    kh = NA_ROWS
    half = kh // 2
    lane_lo = lax.broadcasted_iota(jnp.int32, (GRID_W, LANES), 1) < NA_HEAD_DIM

    def window(i):
        r = blk * NA_ROW_BLOCK + i
        rs = jnp.clip(r - half, 0, rows - kh)
        dr0 = rs - r + (NA_ROWS - 1)
        return pl.multiple_of(rs * GRID_W, GRID_W), dr0

    def bias(h, dr0):
        return jnp.concatenate([bias_ref[h, dr0 + 2 * j] for j in range(kh // 2)], axis=-1)

    def score_stage(i):
        koff, _ = window(i)
        ss = []
        for p in range(NA_HEADS // 2):
            lanes = slice(p * LANES, (p + 1) * LANES)
            qp = q_ref[i * GRID_W:(i + 1) * GRID_W, lanes]
            ql = jnp.concatenate([qp * hm_ref[0], qp * hm_ref[1]], axis=0)
            kw = k_ref[0, pl.ds(koff, kh * GRID_W), lanes]
            ss.append(lax.dot_general(ql, kw, (((1,), (1,)), ((), ())),
                                      preferred_element_type=F32))
        return ss

    def output_stage(i, ss):
        koff, dr0 = window(i)
        es, ls = [], []
        for p in range(NA_HEADS // 2):
            s = ss[p] + jnp.concatenate([bias(2 * p, dr0), bias(2 * p + 1, dr0)], axis=0)
            m = jnp.max(s, axis=-1, keepdims=True)
            e = jnp.exp((s - m).astype(BF16))
            ls.append(jnp.sum(e.astype(F32), axis=-1, keepdims=True))
            es.append(e)
        for p in range(NA_HEADS // 2):
            lanes = slice(p * LANES, (p + 1) * LANES)
            vw = v_ref[0, pl.ds(koff, kh * GRID_W), lanes]
            pv = _dot(es[p], vw) / ls[p]
            o = jnp.where(lane_lo, pv[:GRID_W], pv[GRID_W:])
            o_ref[i * GRID_W:(i + 1) * GRID_W, lanes] = o.astype(BF16)

    ss = score_stage(0)
    for i in range(NA_ROW_BLOCK):
        nxt = score_stage(i + 1) if i + 1 < NA_ROW_BLOCK else None
        output_stage(i, ss)
        ss = nxt


def _na(q, k, v, layer, bias, batch, seq):
    rows = seq // GRID_W
    assert rows >= NA_ROWS and NA_ROWS % 2 == 0
    blk_tokens = NA_ROW_BLOCK * GRID_W
    nblk = rows // NA_ROW_BLOCK
    assert 2 * NA_HEAD_DIM == LANES
    head_mask = (jnp.arange(LANES)[None, None, :] // NA_HEAD_DIM
                 == jnp.arange(2)[:, None, None]).astype(BF16)
    head_mask = jnp.broadcast_to(head_mask, (2, GRID_W, LANES))
    q_tile = pl.BlockSpec((blk_tokens, NA_WIDTH), lambda b, i: (b * nblk + i, 0))
    kv_spec = pl.BlockSpec((1, seq, NA_WIDTH), lambda b, i: (b, 0, 0))
    return pl.pallas_call(
        functools.partial(_na_kernel, rows=rows),
        grid=(batch, nblk),
        in_specs=[q_tile, kv_spec, kv_spec,
                  _layer_spec((NA_HEADS, 2 * NA_ROWS - 2, GRID_W, 2 * GRID_W), layer),
                  _const_spec((2, GRID_W, LANES))],
        out_specs=q_tile,
        out_shape=jax.ShapeDtypeStruct((batch * seq, NA_WIDTH), BF16),
        compiler_params=_params(56),
        name="na",
    )(q, k.reshape(batch, seq, NA_WIDTH), v.reshape(batch, seq, NA_WIDTH), bias, head_mask)


def _dft_tables():
    two_pi = 2.0 * math.pi
    n = DFT_ROWS * DFT_COLS
    k1 = np.arange(DFT_ROWS)
    ang = ((k1[:, None] * k1[None, :]) % DFT_ROWS) * (two_pi / DFT_ROWS)
    eye_t2 = np.eye(F1_T2_BLOCK)
    f1 = np.concatenate([np.kron(np.cos(ang), eye_t2), np.kron(-np.sin(ang), eye_t2)], axis=0)
    k2 = np.arange(DFT_COLS)
    t2 = np.arange(DFT_COLS)
    ang = ((t2[None, None, :] * (DFT_ROWS * k2[None, :, None] + k1[:, None, None])) % n) * (two_pi / n)
    gr, gi = np.cos(ang), -np.sin(ang)
    g = np.concatenate([np.concatenate([gr, -gi], axis=2),
                        np.concatenate([gi, gr], axis=2)], axis=1)
    c = np.arange(F_GROUP_DIM)
    ang = ((c[:, None] * c[None, :]) % F_GROUP_DIM) * (two_pi / F_GROUP_DIM)
    eye = np.eye(MXU_COLS_V7X // F_GROUP_DIM)
    cc = np.kron(eye, np.cos(ang))
    ss = np.kron(eye, np.sin(ang))
    return tuple(jnp.asarray(t.astype(np.float32)).astype(BF16) for t in (f1, g, cc, ss))


def _fnet_kernel(f_ref, x_ref, g_ref, cc_ref, ss_ref, o_ref, a_ref, zr_ref, zi_ref):
    step = pl.program_id(1)

    @pl.when(step < F1_STEPS)
    def _stage1():
        per_store = BF16_SUBLANES // F1_T2_BLOCK
        base = step * F1_T2_STEP
        for s in range(F1_T2_STEP // BF16_SUBLANES):
            zs = []
            for j in range(per_store):
                lo = (s * per_store + j) * F1_T2_BLOCK
                x = x_ref[0, :, lo:lo + F1_T2_BLOCK, :]
                x = x.reshape(DFT_ROWS * F1_T2_BLOCK, F_WIDTH).astype(BF16)
                z = _dot(f_ref[...], x)
                zs.append(z.reshape(2, DFT_ROWS, F1_T2_BLOCK, F_WIDTH))
            t2 = pl.multiple_of(base + s * BF16_SUBLANES, BF16_SUBLANES)
            a_ref[:, :, pl.ds(t2, BF16_SUBLANES), :] = jnp.concatenate(zs, axis=2).astype(BF16)

    @pl.when(step >= F1_STEPS)
    def _stage2():
        k1 = (step - F1_STEPS) * F3_K1_BLOCK
        for j in range(F3_K1_BLOCK):
            a = jnp.concatenate([a_ref[0, k1 + j], a_ref[1, k1 + j]], axis=0)
            z = _dot(g_ref[j], a)
            zr_ref[j * DFT_COLS:(j + 1) * DFT_COLS] = z[:DFT_COLS].astype(BF16)
            zi_ref[j * DFT_COLS:(j + 1) * DFT_COLS] = z[DFT_COLS:].astype(BF16)
        scale = 1.0 / math.sqrt(DFT_ROWS * DFT_COLS * F_GROUP_DIM)
        for p in range(F_WIDTH // MXU_COLS_V7X):
            cols = slice(p * MXU_COLS_V7X, (p + 1) * MXU_COLS_V7X)
            y = _dot(zr_ref[:, cols], cc_ref[...]) + _dot(zi_ref[:, cols], ss_ref[...])
            y = (y * scale).astype(BF16)
            for j in range(F3_K1_BLOCK):
                o_ref[0, j, :, cols] = y[j * DFT_COLS:(j + 1) * DFT_COLS]


def _fourier(u, tables, batch, seq):
    f1, g, cc, ss = tables
    f1_rows = 2 * DFT_ROWS * F1_T2_BLOCK
    y = pl.pallas_call(
        _fnet_kernel,
        grid=(batch, F1_STEPS + F3_STEPS),
        in_specs=[_const_spec((f1_rows, f1_rows // 2)),
                  pl.BlockSpec((1, DFT_ROWS, F1_T2_STEP, F_WIDTH),
                               lambda b, i: (b, 0, jnp.minimum(i, F1_STEPS - 1), 0)),
                  pl.BlockSpec((F3_K1_BLOCK, 2 * DFT_COLS, 2 * DFT_COLS),
                               lambda b, i: (jnp.maximum(i - F1_STEPS, 0), 0, 0)),
                  _const_spec((MXU_COLS_V7X, MXU_COLS_V7X)), _const_spec((MXU_COLS_V7X, MXU_COLS_V7X))],
        out_specs=pl.BlockSpec((1, F3_K1_BLOCK, DFT_COLS, F_WIDTH),
                               lambda b, i: (b, jnp.maximum(i - F1_STEPS, 0), 0, 0)),
        out_shape=jax.ShapeDtypeStruct((batch, DFT_ROWS, DFT_COLS, F_WIDTH), BF16),
        scratch_shapes=[pltpu.VMEM((2, DFT_ROWS, DFT_COLS, F_WIDTH), BF16),
                        pltpu.VMEM((F3_K1_BLOCK * DFT_COLS, F_WIDTH), BF16),
                        pltpu.VMEM((F3_K1_BLOCK * DFT_COLS, F_WIDTH), BF16)],
        compiler_params=_params(52),
        name="fnet",
    )(f1, u.reshape(batch, DFT_ROWS, DFT_COLS, F_WIDTH), g, cc, ss)
    return jnp.transpose(y, (0, 2, 1, 3)).reshape(batch * seq, F_WIDTH)


def _mixout_kernel(x_ref, g_ref, att_ref, fou_ref, wg_ref, gb_ref, wna_ref, wf_ref, wo_ref,
                   *rest, n_cast):
    cast_src, o_ref, cast_dst = rest[:n_cast], rest[n_cast], rest[n_cast + 1:]
    _cast_blocks(cast_src, cast_dst)
    for t in range(TOKEN_TILE // SUB_TILE):
        rows = slice(t * SUB_TILE, (t + 1) * SUB_TILE)
        x = x_ref[rows, :]
        h = _rms(x, g_ref[...]).astype(BF16)
        att = att_ref[rows, :]
        fou = fou_ref[rows, :]
        ms = []
        for c in range(D_MODEL // MIX_CHUNK):
            cn = slice(c * MIX_CHUNK, (c + 1) * MIX_CHUNK)
            cf = slice(D_MODEL + c * MIX_CHUNK, D_MODEL + (c + 1) * MIX_CHUNK)
            g_na = jax.nn.sigmoid(_dot(h, wg_ref[:, cn]) + gb_ref[:, cn])
            g_f = jax.nn.sigmoid(_dot(h, wg_ref[:, cf]) + gb_ref[:, cf])
            m = g_na * _dot(att, wna_ref[:, cn]) + g_f * _dot(fou, wf_ref[:, cn])
            ms.append(m.astype(BF16))
        o_ref[rows, :] = x + _dot(jnp.concatenate(ms, axis=1), wo_ref[...])


def _mixout(x, layer, g, att, fou, w_in, gate_bias, w_na, w_f, w_o, cast=()):
    n = x.shape[0]
    steps = n // TOKEN_TILE
    tile = pl.BlockSpec((TOKEN_TILE, D_MODEL), lambda i: (i, 0))
    half = pl.BlockSpec((TOKEN_TILE, NA_WIDTH), lambda i: (i, 0))
    assert w_in.shape[1] == 4 * D_MODEL
    cast_in, cast_out, cast_shapes = _cast_specs(cast, steps)
    y, *copies = pl.pallas_call(
        functools.partial(_mixout_kernel, n_cast=len(cast)),
        grid=(steps,),
        in_specs=[tile, _layer_spec((1, D_MODEL), layer), half, half,
                  _const_spec((D_MODEL, 2 * D_MODEL), 1),
                  _layer_spec((1, 2 * D_MODEL), layer),
                  _const_spec((NA_WIDTH, D_MODEL)), _const_spec((F_WIDTH, D_MODEL)),
                  _const_spec((D_MODEL, D_MODEL))] + cast_in,
        out_specs=[tile] + cast_out,
        out_shape=[jax.ShapeDtypeStruct((n, D_MODEL), F32)] + cast_shapes,
        compiler_params=_params(48),
        name="mix_out",
    )(x, g, att, fou, w_in, gate_bias, w_na, w_f, w_o, *[arr for arr, _ in cast])
    return y, copies


def kernel(x, ffn1_norm, ffn1_w_in, ffn1_w_out, mix_norm, mix_w_in, mix_gate_bias, na_rpb,
           na_w_out, f_w_out, mix_w_o, ffn2_norm, ffn2_w_in, ffn2_w_out, final_norm):
    batch, seq, d = x.shape
    depth = ffn1_norm.shape[0]
    assert d == D_MODEL and seq == DFT_ROWS * DFT_COLS and seq % (GRID_W * NA_ROW_BLOCK) == 0
    assert all((batch * seq) % t == 0
               for t in (PROJ_TOKEN_TILE, TOKEN_TILE, FFN_TOKEN_TILE, FFN_TOKEN_TILE_F32W))
    tables = _dft_tables()
    na_bias = _na_bias_tables(na_rpb)
    gain = lambda g: g.reshape(depth, 1, D_MODEL)
    g1, g2, mix_g = gain(ffn1_norm), gain(ffn2_norm), gain(mix_norm)
    gate_bias = mix_gate_bias.reshape(depth, 1, 2 * D_MODEL)
    xs = x.reshape(batch * seq, d)
    ffn1_w = [ffn1_w_in, ffn1_w_out]
    for l in range(depth):
        xs, (w_in, w_na, w_f, w_o) = _ffn(
            xs, l, g1, *ffn1_w,
            cast=[(mix_w_in, l), (na_w_out, l), (f_w_out, l), (mix_w_o, l)])
        q, k, v, u = _proj(xs, l, mix_g, w_in)
        att = _na(q, k, v, l, na_bias, batch, seq)
        fou = _fourier(u, tables, batch, seq)
        xs, ffn2_w = _mixout(xs, l, mix_g, att, fou, w_in, gate_bias, w_na, w_f, w_o,
                             cast=[(ffn2_w_in, l), (ffn2_w_out, l)])
        last = l == depth - 1
        xs, ffn1_w = _ffn(xs, l, g2, *ffn2_w, final_g=final_norm if last else None,
                          cast=[] if last else [(ffn1_w_in, l + 1), (ffn1_w_out, l + 1)])
    return xs.reshape(batch, seq, d)
```

```python
import functools
import math

import jax
import jax.numpy as jnp
import numpy as np
from jax import lax
from jax.experimental import pallas as pl
from jax.experimental.pallas import tpu as pltpu

D_MODEL = 1024
GRID_W = 64
NA_HEADS = 8
NA_HEAD_DIM = 64
NA_WIDTH = NA_HEADS * NA_HEAD_DIM
NA_ROWS = 8
NA_COLS = 16
F_GROUPS = 4
F_GROUP_DIM = 128
F_WIDTH = F_GROUPS * F_GROUP_DIM
D_FF = 2816
RMS_EPS = 1e-6

BF16 = jnp.bfloat16
F32 = jnp.float32

MXU_COLS_V7X = 256
LANES = 128
VMEM_BYTES_V7X = 64 * 1024 * 1024

PROJ_TOKEN_TILE = 2048
MIXFFN_TILE = 512
MIXFFN_SUB = 256
SUB_TILE = 512
FFN_TOKEN_TILE = 1024
FFN_TOKEN_TILE_F32W = 512
FF_CHUNK = MXU_COLS_V7X
NA_ROW_BLOCK = 16
DFT_ROWS = 64
DFT_COLS = 128
SUBLANES_F32 = 8
BF16_SUBLANES = 16
F1_T2_BLOCK = SUBLANES_F32
F1_T2_STEP = 32
F3_K1_BLOCK = 16
F1_STEPS = DFT_COLS // F1_T2_STEP
F3_STEPS = DFT_ROWS // F3_K1_BLOCK


def _params(vmem_mib):
    assert vmem_mib * 1024 * 1024 <= VMEM_BYTES_V7X
    return pltpu.CompilerParams(
        dimension_semantics=None,
        vmem_limit_bytes=vmem_mib * 1024 * 1024,
    )


def _const_spec(shape, col_block=0):
    index = (0,) * (len(shape) - 1) + (col_block,)
    return pl.BlockSpec(shape, lambda *_: index, pipeline_mode=pl.Buffered(1))


def _cast_specs(params, steps):
    in_specs, out_specs, out_shapes = [], [], []
    for arr, layer in params:
        _, rows, cols = arr.shape
        nblk = steps
        while rows % nblk or (rows // nblk) % BF16_SUBLANES:
            nblk //= 2
        blk = rows // nblk
        in_specs.append(pl.BlockSpec(
            (None, blk, cols), lambda i, layer=layer, nblk=nblk: (layer, jnp.minimum(i, nblk - 1), 0)))
        out_specs.append(pl.BlockSpec(
            (blk, cols), lambda i, nblk=nblk: (jnp.minimum(i, nblk - 1), 0)))
        out_shapes.append(jax.ShapeDtypeStruct((rows, cols), BF16))
    return in_specs, out_specs, out_shapes


def _cast_blocks(src_refs, dst_refs):
    for src, dst in zip(src_refs, dst_refs):
        dst[...] = src[...].astype(BF16)


def _layer_spec(shape, layer):
    index = (layer,) + (0,) * len(shape)
    return pl.BlockSpec((None,) + tuple(shape), lambda *_: index, pipeline_mode=pl.Buffered(1))


def _rms(x, g):
    ms = jnp.mean(x * x, axis=-1, keepdims=True)
    return x * lax.rsqrt(ms + RMS_EPS) * g


def _dot(a, b):
    return jnp.dot(a, b, preferred_element_type=F32)


def _ffn_kernel(x_ref, g_ref, win_ref, wout_ref, *rest, final, n_cast):
    rest = list(rest)
    gf_ref = rest.pop(0) if final else None
    cast_src, o_ref, cast_dst, act_ref = (rest[:n_cast], rest[n_cast],
                                          rest[n_cast + 1:2 * n_cast + 1], rest[-1])
    _cast_blocks(cast_src, cast_dst)
    x = x_ref[...]
    h = _rms(x, g_ref[...]).astype(BF16)
    for c in range(D_FF // FF_CHUNK):
        lo = c * FF_CHUNK
        g = _dot(h, win_ref[:, lo:lo + FF_CHUNK].astype(BF16))
        u = _dot(h, win_ref[:, D_FF + lo:D_FF + lo + FF_CHUNK].astype(BF16))
        act_ref[:, lo:lo + FF_CHUNK] = (g * jax.nn.sigmoid(g) * u).astype(BF16)
    y = x + 0.5 * _dot(act_ref[...], wout_ref[...].astype(BF16))
    if final:
        y = _rms(y, gf_ref[...])
    o_ref[...] = y


def _ffn(x, layer, g, w_in, w_out, final_g=None, cast=()):
    n = x.shape[0]
    f32_weights = w_in.ndim == 3
    rows = FFN_TOKEN_TILE_F32W if f32_weights else FFN_TOKEN_TILE
    steps = n // rows
    final = final_g is not None
    tile = pl.BlockSpec((rows, D_MODEL), lambda i: (i, 0))
    w_spec = functools.partial(_layer_spec, layer=layer) if f32_weights else _const_spec
    in_specs = [tile, _layer_spec((1, D_MODEL), layer), w_spec((D_MODEL, 2 * D_FF)),
                w_spec((D_FF, D_MODEL))]
    args = [x, g, w_in, w_out]
    if final:
        in_specs.append(_const_spec((1, D_MODEL)))
        args.append(final_g.reshape(1, D_MODEL))
    cast_in, cast_out, cast_shapes = _cast_specs(cast, steps)
    y, *copies = pl.pallas_call(
        functools.partial(_ffn_kernel, final=final, n_cast=len(cast)),
        grid=(steps,),
        in_specs=in_specs + cast_in,
        out_specs=[tile] + cast_out,
        out_shape=[jax.ShapeDtypeStruct((n, D_MODEL), F32)] + cast_shapes,
        scratch_shapes=[pltpu.VMEM((rows, D_FF), BF16)],
        compiler_params=_params(58),
        name="ffn",
    )(*args, *[arr for arr, _ in cast])
    return y, copies


def _proj_kernel(x_ref, g_ref, w_ref, q_ref, k_ref, v_ref, u_ref):
    s = NA_WIDTH
    for t in range(PROJ_TOKEN_TILE // SUB_TILE):
        rows = slice(t * SUB_TILE, (t + 1) * SUB_TILE)
        h = _rms(x_ref[rows, :], g_ref[...]).astype(BF16)
        z = _dot(h, w_ref[...])
        q_ref[rows, :] = (z[:, :s] * (NA_HEAD_DIM ** -0.5)).astype(BF16)
        k_ref[rows, :] = z[:, s:2 * s].astype(BF16)
        v_ref[rows, :] = z[:, 2 * s:3 * s].astype(BF16)
        u_ref[rows, :] = z[:, 3 * s:3 * s + F_WIDTH]


def _proj(x, layer, g, w):
    n = x.shape[0]
    cols = 3 * NA_WIDTH + F_WIDTH
    assert w.shape[1] == 2 * cols
    out_tile = pl.BlockSpec((PROJ_TOKEN_TILE, NA_WIDTH), lambda i: (i, 0))
    out_sds = jax.ShapeDtypeStruct((n, NA_WIDTH), BF16)
    return pl.pallas_call(
        _proj_kernel,
        grid=(n // PROJ_TOKEN_TILE,),
        in_specs=[pl.BlockSpec((PROJ_TOKEN_TILE, D_MODEL), lambda i: (i, 0)),
                  _layer_spec((1, D_MODEL), layer), _const_spec((D_MODEL, cols), 0)],
        out_specs=[out_tile] * 4,
        out_shape=[out_sds] * 3 + [jax.ShapeDtypeStruct((n, F_WIDTH), F32)],
        compiler_params=_params(48),
        name="mix_proj",
    )(x, g, w)


def _na_bias_tables(rpb):
    ncol = 2 * NA_COLS - 1
    qc = jnp.arange(GRID_W)
    lane = jnp.arange(2 * GRID_W)
    kc, side = lane % GRID_W, lane // GRID_W
    ws = jnp.clip(qc - NA_COLS // 2, 0, GRID_W - NA_COLS)
    ok = (kc[None, :] >= ws[:, None]) & (kc[None, :] < ws[:, None] + NA_COLS)
    dc = kc[None, :] - qc[:, None] + (NA_COLS - 1)
    src = jnp.where(ok, side[None, :] * ncol + dc, -1)
    pick = (src[:, :, None] == jnp.arange(2 * ncol)[None, None, :]).astype(F32)
    both = jnp.concatenate([rpb[:, :, :-1], rpb[:, :, 1:]], axis=-1).astype(F32)
    table = jnp.einsum('lhdm,qkm->lhdqk', both, pick, precision=lax.Precision.HIGHEST)
    return jnp.where(ok[None, None, None], table, -jnp.inf)


def _na_kernel(q_ref, k_ref, v_ref, bias_ref, hm_ref, o_ref, *, rows):
    blk = pl.program_id(1)
    kh = NA_ROWS
    half = kh // 2
    lane_lo = lax.broadcasted_iota(jnp.int32, (GRID_W, LANES), 1) < NA_HEAD_DIM

    def window(i):
        r = blk * NA_ROW_BLOCK + i
        rs = jnp.clip(r - half, 0, rows - kh)
        dr0 = rs - r + (NA_ROWS - 1)
        return pl.multiple_of(rs * GRID_W, GRID_W), dr0

    def bias(h, dr0):
        return jnp.concatenate([bias_ref[h, dr0 + 2 * j] for j in range(kh // 2)], axis=-1)

    def score_stage(i):
        koff, _ = window(i)
        ss = []
        for p in range(NA_HEADS // 2):
            lanes = slice(p * LANES, (p + 1) * LANES)
            qp = q_ref[i * GRID_W:(i + 1) * GRID_W, lanes]
            ql = jnp.concatenate([qp * hm_ref[0], qp * hm_ref[1]], axis=0)
            kw = k_ref[0, pl.ds(koff, kh * GRID_W), lanes]
            ss.append(lax.dot_general(ql, kw, (((1,), (1,)), ((), ())),
                                      preferred_element_type=F32))
        return ss

    def output_stage(i, ss):
        koff, dr0 = window(i)
        es, ls = [], []
        for p in range(NA_HEADS // 2):
            s = ss[p] + jnp.concatenate([bias(2 * p, dr0), bias(2 * p + 1, dr0)], axis=0)
            m = jnp.max(s, axis=-1, keepdims=True)
            e = jnp.exp((s - m).astype(BF16))
            ls.append(jnp.sum(e.astype(F32), axis=-1, keepdims=True))
            es.append(e)
        for p in range(NA_HEADS // 2):
            lanes = slice(p * LANES, (p + 1) * LANES)
            vw = v_ref[0, pl.ds(koff, kh * GRID_W), lanes]
            pv = _dot(es[p], vw) / ls[p]
            o = jnp.where(lane_lo, pv[:GRID_W], pv[GRID_W:])
            o_ref[i * GRID_W:(i + 1) * GRID_W, lanes] = o.astype(BF16)

    ss = score_stage(0)
    for i in range(NA_ROW_BLOCK):
        nxt = score_stage(i + 1) if i + 1 < NA_ROW_BLOCK else None
        output_stage(i, ss)
        ss = nxt


def _na(q, k, v, layer, bias, batch, seq):
    rows = seq // GRID_W
    assert rows >= NA_ROWS and NA_ROWS % 2 == 0
    blk_tokens = NA_ROW_BLOCK * GRID_W
    nblk = rows // NA_ROW_BLOCK
    assert 2 * NA_HEAD_DIM == LANES
    head_mask = (jnp.arange(LANES)[None, None, :] // NA_HEAD_DIM
                 == jnp.arange(2)[:, None, None]).astype(BF16)
    head_mask = jnp.broadcast_to(head_mask, (2, GRID_W, LANES))
    q_tile = pl.BlockSpec((blk_tokens, NA_WIDTH), lambda b, i: (b * nblk + i, 0))
    kv_spec = pl.BlockSpec((1, seq, NA_WIDTH), lambda b, i: (b, 0, 0))
    return pl.pallas_call(
        functools.partial(_na_kernel, rows=rows),
        grid=(batch, nblk),
        in_specs=[q_tile, kv_spec, kv_spec,
                  _layer_spec((NA_HEADS, 2 * NA_ROWS - 2, GRID_W, 2 * GRID_W), layer),
                  _const_spec((2, GRID_W, LANES))],
        out_specs=q_tile,
        out_shape=jax.ShapeDtypeStruct((batch * seq, NA_WIDTH), BF16),
        compiler_params=_params(56),
        name="na",
    )(q, k.reshape(batch, seq, NA_WIDTH), v.reshape(batch, seq, NA_WIDTH), bias, head_mask)


def _dft_tables():
    two_pi = 2.0 * math.pi
    n = DFT_ROWS * DFT_COLS
    k1 = np.arange(DFT_ROWS)
    ang = ((k1[:, None] * k1[None, :]) % DFT_ROWS) * (two_pi / DFT_ROWS)
    eye_t2 = np.eye(F1_T2_BLOCK)
    f1 = np.concatenate([np.kron(np.cos(ang), eye_t2), np.kron(-np.sin(ang), eye_t2)], axis=0)
    k2 = np.arange(DFT_COLS)
    t2 = np.arange(DFT_COLS)
    ang = ((t2[None, None, :] * (DFT_ROWS * k2[None, :, None] + k1[:, None, None])) % n) * (two_pi / n)
    gr, gi = np.cos(ang), -np.sin(ang)
    g = np.concatenate([np.concatenate([gr, -gi], axis=2),
                        np.concatenate([gi, gr], axis=2)], axis=1)
    c = np.arange(F_GROUP_DIM)
    ang = ((c[:, None] * c[None, :]) % F_GROUP_DIM) * (two_pi / F_GROUP_DIM)
    eye = np.eye(MXU_COLS_V7X // F_GROUP_DIM)
    cc = np.kron(eye, np.cos(ang))
    ss = np.kron(eye, np.sin(ang))
    return tuple(jnp.asarray(t.astype(np.float32)).astype(BF16) for t in (f1, g, cc, ss))


def _fnet_kernel(f_ref, x_ref, g_ref, cc_ref, ss_ref, o_ref, a_ref, zr_ref, zi_ref):
    step = pl.program_id(1)

    @pl.when(step < F1_STEPS)
    def _stage1():
        per_store = BF16_SUBLANES // F1_T2_BLOCK
        base = step * F1_T2_STEP
        for s in range(F1_T2_STEP // BF16_SUBLANES):
            zs = []
            for j in range(per_store):
                lo = (s * per_store + j) * F1_T2_BLOCK
                x = x_ref[0, :, lo:lo + F1_T2_BLOCK, :]
                x = x.reshape(DFT_ROWS * F1_T2_BLOCK, F_WIDTH).astype(BF16)
                z = _dot(f_ref[...], x)
                zs.append(z.reshape(2, DFT_ROWS, F1_T2_BLOCK, F_WIDTH))
            t2 = pl.multiple_of(base + s * BF16_SUBLANES, BF16_SUBLANES)
            a_ref[:, :, pl.ds(t2, BF16_SUBLANES), :] = jnp.concatenate(zs, axis=2).astype(BF16)

    @pl.when(step >= F1_STEPS)
    def _stage2():
        k1 = (step - F1_STEPS) * F3_K1_BLOCK
        for j in range(F3_K1_BLOCK):
            a = jnp.concatenate([a_ref[0, k1 + j], a_ref[1, k1 + j]], axis=0)
            z = _dot(g_ref[j], a)
            zr_ref[j * DFT_COLS:(j + 1) * DFT_COLS] = z[:DFT_COLS].astype(BF16)
            zi_ref[j * DFT_COLS:(j + 1) * DFT_COLS] = z[DFT_COLS:].astype(BF16)
        scale = 1.0 / math.sqrt(DFT_ROWS * DFT_COLS * F_GROUP_DIM)
        for p in range(F_WIDTH // MXU_COLS_V7X):
            cols = slice(p * MXU_COLS_V7X, (p + 1) * MXU_COLS_V7X)
            y = _dot(zr_ref[:, cols], cc_ref[...]) + _dot(zi_ref[:, cols], ss_ref[...])
            y = (y * scale).astype(BF16)
            for j in range(F3_K1_BLOCK):
                o_ref[0, j, :, cols] = y[j * DFT_COLS:(j + 1) * DFT_COLS]


def _fourier(u, tables, batch, seq):
    f1, g, cc, ss = tables
    f1_rows = 2 * DFT_ROWS * F1_T2_BLOCK
    y = pl.pallas_call(
        _fnet_kernel,
        grid=(batch, F1_STEPS + F3_STEPS),
        in_specs=[_const_spec((f1_rows, f1_rows // 2)),
                  pl.BlockSpec((1, DFT_ROWS, F1_T2_STEP, F_WIDTH),
                               lambda b, i: (b, 0, jnp.minimum(i, F1_STEPS - 1), 0)),
                  pl.BlockSpec((F3_K1_BLOCK, 2 * DFT_COLS, 2 * DFT_COLS),
                               lambda b, i: (jnp.maximum(i - F1_STEPS, 0), 0, 0)),
                  _const_spec((MXU_COLS_V7X, MXU_COLS_V7X)), _const_spec((MXU_COLS_V7X, MXU_COLS_V7X))],
        out_specs=pl.BlockSpec((1, F3_K1_BLOCK, DFT_COLS, F_WIDTH),
                               lambda b, i: (b, jnp.maximum(i - F1_STEPS, 0), 0, 0)),
        out_shape=jax.ShapeDtypeStruct((batch, DFT_ROWS, DFT_COLS, F_WIDTH), BF16),
        scratch_shapes=[pltpu.VMEM((2, DFT_ROWS, DFT_COLS, F_WIDTH), BF16),
                        pltpu.VMEM((F3_K1_BLOCK * DFT_COLS, F_WIDTH), BF16),
                        pltpu.VMEM((F3_K1_BLOCK * DFT_COLS, F_WIDTH), BF16)],
        compiler_params=_params(52),
        name="fnet",
    )(f1, u.reshape(batch, DFT_ROWS, DFT_COLS, F_WIDTH), g, cc, ss)
    return jnp.transpose(y, (0, 2, 1, 3)).reshape(batch * seq, F_WIDTH)


def _mixffn_kernel(x_ref, g_ref, att_ref, fou_ref, wg_ref, gb_ref, wna_ref, wf_ref, wo_ref,
                   g2_ref, win_ref, wout_ref, *rest, final, n_cast):
    rest = list(rest)
    gf_ref = rest.pop(0) if final else None
    cast_src, o_ref, cast_dst, act_ref = (rest[:n_cast], rest[n_cast],
                                          rest[n_cast + 1:2 * n_cast + 1], rest[-1])
    _cast_blocks(cast_src, cast_dst)
    for t in range(MIXFFN_TILE // MIXFFN_SUB):
        rows = slice(t * MIXFFN_SUB, (t + 1) * MIXFFN_SUB)
        x = x_ref[rows, :]
        h = _rms(x, g_ref[...]).astype(BF16)
        gates = jax.nn.sigmoid(_dot(h, wg_ref[...]) + gb_ref[...])
        y_na = _dot(att_ref[rows, :], wna_ref[...])
        y_f = _dot(fou_ref[rows, :], wf_ref[...])
        m = gates[:, :D_MODEL] * y_na + gates[:, D_MODEL:] * y_f
        x = x + _dot(m.astype(BF16), wo_ref[...])
        h = _rms(x, g2_ref[...]).astype(BF16)
        for c in range(D_FF // FF_CHUNK):
            lo = c * FF_CHUNK
            g = _dot(h, win_ref[:, lo:lo + FF_CHUNK])
            u = _dot(h, win_ref[:, D_FF + lo:D_FF + lo + FF_CHUNK])
            act_ref[rows, lo:lo + FF_CHUNK] = (g * jax.nn.sigmoid(g) * u).astype(BF16)
        y = x + 0.5 * _dot(act_ref[rows, :], wout_ref[...])
        if final:
            y = _rms(y, gf_ref[...])
        o_ref[rows, :] = y


def _mixffn(x, layer, g, att, fou, w_in, gate_bias, w_na, w_f, w_o, g2, ffn_w_in, ffn_w_out,
            final_g=None, cast=()):
    n = x.shape[0]
    steps = n // MIXFFN_TILE
    final = final_g is not None
    tile = pl.BlockSpec((MIXFFN_TILE, D_MODEL), lambda i: (i, 0))
    half = pl.BlockSpec((MIXFFN_TILE, NA_WIDTH), lambda i: (i, 0))
    assert w_in.shape[1] == 4 * D_MODEL
    in_specs = [tile, _layer_spec((1, D_MODEL), layer), half, half,
                _const_spec((D_MODEL, 2 * D_MODEL), 1),
                _layer_spec((1, 2 * D_MODEL), layer),
                _const_spec((NA_WIDTH, D_MODEL)), _const_spec((F_WIDTH, D_MODEL)),
                _const_spec((D_MODEL, D_MODEL)), _layer_spec((1, D_MODEL), layer),
                _const_spec((D_MODEL, 2 * D_FF)), _const_spec((D_FF, D_MODEL))]
    args = [x, g, att, fou, w_in, gate_bias, w_na, w_f, w_o, g2, ffn_w_in, ffn_w_out]
    if final:
        in_specs.append(_const_spec((1, D_MODEL)))
        args.append(final_g.reshape(1, D_MODEL))
    cast_in, cast_out, cast_shapes = _cast_specs(cast, steps)
    y, *copies = pl.pallas_call(
        functools.partial(_mixffn_kernel, final=final, n_cast=len(cast)),
        grid=(steps,),
        in_specs=in_specs + cast_in,
        out_specs=[tile] + cast_out,
        out_shape=[jax.ShapeDtypeStruct((n, D_MODEL), F32)] + cast_shapes,
        scratch_shapes=[pltpu.VMEM((MIXFFN_TILE, D_FF), BF16)],
        compiler_params=_params(58),
        name="mix_out_ffn",
    )(*args, *[arr for arr, _ in cast])
    return y, copies


def kernel(x, ffn1_norm, ffn1_w_in, ffn1_w_out, mix_norm, mix_w_in, mix_gate_bias, na_rpb,
           na_w_out, f_w_out, mix_w_o, ffn2_norm, ffn2_w_in, ffn2_w_out, final_norm):
    batch, seq, d = x.shape
    depth = ffn1_norm.shape[0]
    assert d == D_MODEL and seq == DFT_ROWS * DFT_COLS and seq % (GRID_W * NA_ROW_BLOCK) == 0
    assert all((batch * seq) % t == 0
               for t in (PROJ_TOKEN_TILE, MIXFFN_TILE, FFN_TOKEN_TILE, FFN_TOKEN_TILE_F32W))
    tables = _dft_tables()
    na_bias = _na_bias_tables(na_rpb)
    gain = lambda g: g.reshape(depth, 1, D_MODEL)
    g1, g2, mix_g = gain(ffn1_norm), gain(ffn2_norm), gain(mix_norm)
    gate_bias = mix_gate_bias.reshape(depth, 1, 2 * D_MODEL)
    xs = x.reshape(batch * seq, d)
    ffn1_w = [ffn1_w_in, ffn1_w_out]
    for l in range(depth):
        xs, (w_in, w_na, w_f, w_o, w2_in, w2_out) = _ffn(
            xs, l, g1, *ffn1_w,
            cast=[(mix_w_in, l), (na_w_out, l), (f_w_out, l), (mix_w_o, l),
                  (ffn2_w_in, l), (ffn2_w_out, l)])
        q, k, v, u = _proj(xs, l, mix_g, w_in)
        att = _na(q, k, v, l, na_bias, batch, seq)
        fou = _fourier(u, tables, batch, seq)
        last = l == depth - 1
        xs, ffn1_w = _mixffn(
            xs, l, mix_g, att, fou, w_in, gate_bias, w_na, w_f, w_o, g2, w2_in, w2_out,
            final_g=final_norm if last else None,
            cast=[] if last else [(ffn1_w_in, l + 1), (ffn1_w_out, l + 1)])
    return xs.reshape(batch, seq, d)
```

```python
import functools
import math

import jax
import jax.numpy as jnp
import numpy as np
from jax import lax
from jax.experimental import pallas as pl
from jax.experimental.pallas import tpu as pltpu

D_MODEL = 1024
GRID_W = 64
NA_HEADS = 8
NA_HEAD_DIM = 64
NA_WIDTH = NA_HEADS * NA_HEAD_DIM
NA_ROWS = 8
NA_COLS = 16
F_GROUPS = 4
F_GROUP_DIM = 128
F_WIDTH = F_GROUPS * F_GROUP_DIM
D_FF = 2816
RMS_EPS = 1e-6

BF16 = jnp.bfloat16
F32 = jnp.float32

MXU_COLS_V7X = 256
LANES = 128
VMEM_BYTES_V7X = 64 * 1024 * 1024

PROJ_TOKEN_TILE = 2048
TOKEN_TILE = 1024
SUB_TILE = 512
FFN_TOKEN_TILE = 1024
FFN_TOKEN_TILE_F32W = 512
FF_CHUNK = MXU_COLS_V7X
NA_ROW_BLOCK = 16
DFT_ROWS = 64
DFT_COLS = 128
SUBLANES_F32 = 8
BF16_SUBLANES = 16
F1_T2_BLOCK = SUBLANES_F32
F1_T2_STEP = 32
F3_K1_BLOCK = 16
F1_STEPS = DFT_COLS // F1_T2_STEP
F3_STEPS = DFT_ROWS // F3_K1_BLOCK


def _params(vmem_mib):
    assert vmem_mib * 1024 * 1024 <= VMEM_BYTES_V7X
    return pltpu.CompilerParams(
        dimension_semantics=None,
        vmem_limit_bytes=vmem_mib * 1024 * 1024,
    )


def _const_spec(shape, col_block=0):
    index = (0,) * (len(shape) - 1) + (col_block,)
    return pl.BlockSpec(shape, lambda *_: index, pipeline_mode=pl.Buffered(1))


def _cast_specs(params, steps):
    in_specs, out_specs, out_shapes = [], [], []
    for arr, layer in params:
        _, rows, cols = arr.shape
        nblk = steps
        while rows % nblk or (rows // nblk) % BF16_SUBLANES:
            nblk //= 2
        blk = rows // nblk
        in_specs.append(pl.BlockSpec(
            (None, blk, cols), lambda i, layer=layer, nblk=nblk: (layer, jnp.minimum(i, nblk - 1), 0)))
        out_specs.append(pl.BlockSpec(
            (blk, cols), lambda i, nblk=nblk: (jnp.minimum(i, nblk - 1), 0)))
        out_shapes.append(jax.ShapeDtypeStruct((rows, cols), BF16))
    return in_specs, out_specs, out_shapes


def _cast_blocks(src_refs, dst_refs):
    for src, dst in zip(src_refs, dst_refs):
        dst[...] = src[...].astype(BF16)


def _layer_spec(shape, layer):
    index = (layer,) + (0,) * len(shape)
    return pl.BlockSpec((None,) + tuple(shape), lambda *_: index, pipeline_mode=pl.Buffered(1))


def _rms(x, g):
    ms = jnp.mean(x * x, axis=-1, keepdims=True)
    return x * lax.rsqrt(ms + RMS_EPS) * g


def _dot(a, b):
    return jnp.dot(a, b, preferred_element_type=F32)


def _ffn_kernel(x_ref, g_ref, win_ref, wout_hbm, *rest, final, n_cast, layer):
    rest = list(rest)
    gf_ref = rest.pop(0) if final else None
    cast_src, o_ref, cast_dst = rest[:n_cast], rest[n_cast], rest[n_cast + 1:2 * n_cast + 1]
    act_ref, wout_ref, wout_sem = rest[2 * n_cast + 1:]
    first_step = pl.program_id(0) == 0
    wout_copy = pltpu.make_async_copy(
        wout_hbm if layer is None else wout_hbm.at[layer], wout_ref, wout_sem)

    @pl.when(first_step)
    def _start_wout():
        wout_copy.start()

    _cast_blocks(cast_src, cast_dst)
    x = x_ref[...]
    h = _rms(x, g_ref[...]).astype(BF16)
    for c in range(D_FF // FF_CHUNK):
        lo = c * FF_CHUNK
        g = _dot(h, win_ref[:, lo:lo + FF_CHUNK].astype(BF16))
        u = _dot(h, win_ref[:, D_FF + lo:D_FF + lo + FF_CHUNK].astype(BF16))
        act_ref[:, lo:lo + FF_CHUNK] = (g * jax.nn.sigmoid(g) * u).astype(BF16)

    @pl.when(first_step)
    def _wait_wout():
        wout_copy.wait()

    y = x + 0.5 * _dot(act_ref[...], wout_ref[...].astype(BF16))
    if final:
        y = _rms(y, gf_ref[...])
    o_ref[...] = y


def _ffn(x, layer, g, w_in, w_out, final_g=None, cast=()):
    n = x.shape[0]
    f32_weights = w_in.ndim == 3
    rows = FFN_TOKEN_TILE_F32W if f32_weights else FFN_TOKEN_TILE
    steps = n // rows
    final = final_g is not None
    tile = pl.BlockSpec((rows, D_MODEL), lambda i: (i, 0))
    w_spec = functools.partial(_layer_spec, layer=layer) if f32_weights else _const_spec
    in_specs = [tile, _layer_spec((1, D_MODEL), layer), w_spec((D_MODEL, 2 * D_FF)),
                pl.BlockSpec(memory_space=pl.ANY)]
    args = [x, g, w_in, w_out]
    if final:
        in_specs.append(_const_spec((1, D_MODEL)))
        args.append(final_g.reshape(1, D_MODEL))
    cast_in, cast_out, cast_shapes = _cast_specs(cast, steps)
    y, *copies = pl.pallas_call(
        functools.partial(_ffn_kernel, final=final, n_cast=len(cast),
                          layer=layer if f32_weights else None),
        grid=(steps,),
        in_specs=in_specs + cast_in,
        out_specs=[tile] + cast_out,
        out_shape=[jax.ShapeDtypeStruct((n, D_MODEL), F32)] + cast_shapes,
        scratch_shapes=[pltpu.VMEM((rows, D_FF), BF16),
                        pltpu.VMEM((D_FF, D_MODEL), w_out.dtype),
                        pltpu.SemaphoreType.DMA(())],
        compiler_params=_params(58),
        name="ffn",
    )(*args, *[arr for arr, _ in cast])
    return y, copies


def _proj_kernel(x_ref, g_ref, w_ref, q_ref, k_ref, v_ref, u_ref):
    s = NA_WIDTH
    for t in range(PROJ_TOKEN_TILE // SUB_TILE):
        rows = slice(t * SUB_TILE, (t + 1) * SUB_TILE)
        h = _rms(x_ref[rows, :], g_ref[...]).astype(BF16)
        z = _dot(h, w_ref[...])
        q_ref[rows, :] = (z[:, :s] * (NA_HEAD_DIM ** -0.5)).astype(BF16)
        k_ref[rows, :] = z[:, s:2 * s].astype(BF16)
        v_ref[rows, :] = z[:, 2 * s:3 * s].astype(BF16)
        u_ref[rows, :] = z[:, 3 * s:3 * s + F_WIDTH]


def _proj(x, layer, g, w):
    n = x.shape[0]
    cols = 3 * NA_WIDTH + F_WIDTH
    assert w.shape[1] == 2 * cols
    out_tile = pl.BlockSpec((PROJ_TOKEN_TILE, NA_WIDTH), lambda i: (i, 0))
    out_sds = jax.ShapeDtypeStruct((n, NA_WIDTH), BF16)
    return pl.pallas_call(
        _proj_kernel,
        grid=(n // PROJ_TOKEN_TILE,),
        in_specs=[pl.BlockSpec((PROJ_TOKEN_TILE, D_MODEL), lambda i: (i, 0)),
                  _layer_spec((1, D_MODEL), layer), _const_spec((D_MODEL, cols), 0)],
        out_specs=[out_tile] * 4,
        out_shape=[out_sds] * 3 + [jax.ShapeDtypeStruct((n, F_WIDTH), F32)],
        compiler_params=_params(48),
        name="mix_proj",
    )(x, g, w)


def _na_bias_tables(rpb):
    ncol = 2 * NA_COLS - 1
    qc = jnp.arange(GRID_W)
    lane = jnp.arange(2 * GRID_W)
    kc, side = lane % GRID_W, lane // GRID_W
    ws = jnp.clip(qc - NA_COLS // 2, 0, GRID_W - NA_COLS)
    ok = (kc[None, :] >= ws[:, None]) & (kc[None, :] < ws[:, None] + NA_COLS)
    dc = kc[None, :] - qc[:, None] + (NA_COLS - 1)
    src = jnp.where(ok, side[None, :] * ncol + dc, -1)
    pick = (src[:, :, None] == jnp.arange(2 * ncol)[None, None, :]).astype(F32)
    both = jnp.concatenate([rpb[:, :, :-1], rpb[:, :, 1:]], axis=-1).astype(F32)
    table = jnp.einsum('lhdm,qkm->lhdqk', both, pick, precision=lax.Precision.HIGHEST)
    return jnp.where(ok[None, None, None], table, -jnp.inf)


def _na_kernel(q_ref, k_ref, v_ref, bias_ref, hm_ref, o_ref, *, rows):
    blk = pl.program_id(1)
    kh = NA_ROWS
    half = kh // 2
    lane_lo = lax.broadcasted_iota(jnp.int32, (GRID_W, LANES), 1) < NA_HEAD_DIM

    def window(i):
        r = blk * NA_ROW_BLOCK + i
        rs = jnp.clip(r - half, 0, rows - kh)
        dr0 = rs - r + (NA_ROWS - 1)
        return pl.multiple_of(rs * GRID_W, GRID_W), dr0

    def bias(h, dr0):
        return jnp.concatenate([bias_ref[h, dr0 + 2 * j] for j in range(kh // 2)], axis=-1)

    def score_stage(i):
        koff, _ = window(i)
        ss = []
        for p in range(NA_HEADS // 2):
            lanes = slice(p * LANES, (p + 1) * LANES)
            qp = q_ref[i * GRID_W:(i + 1) * GRID_W, lanes]
            ql = jnp.concatenate([qp * hm_ref[0], qp * hm_ref[1]], axis=0)
            kw = k_ref[0, pl.ds(koff, kh * GRID_W), lanes]
            ss.append(lax.dot_general(ql, kw, (((1,), (1,)), ((), ())),
                                      preferred_element_type=F32))
        return ss

    def output_stage(i, ss):
        koff, dr0 = window(i)
        es, ls = [], []
        for p in range(NA_HEADS // 2):
            s = ss[p] + jnp.concatenate([bias(2 * p, dr0), bias(2 * p + 1, dr0)], axis=0)
            m = jnp.max(s, axis=-1, keepdims=True)
            e = jnp.exp((s - m).astype(BF16))
            ls.append(jnp.sum(e.astype(F32), axis=-1, keepdims=True))
            es.append(e)
        for p in range(NA_HEADS // 2):
            lanes = slice(p * LANES, (p + 1) * LANES)
            vw = v_ref[0, pl.ds(koff, kh * GRID_W), lanes]
            pv = _dot(es[p], vw) / ls[p]
            o = jnp.where(lane_lo, pv[:GRID_W], pv[GRID_W:])
            o_ref[i * GRID_W:(i + 1) * GRID_W, lanes] = o.astype(BF16)

    ss = score_stage(0)
    for i in range(NA_ROW_BLOCK):
        nxt = score_stage(i + 1) if i + 1 < NA_ROW_BLOCK else None
        output_stage(i, ss)
        ss = nxt


def _na(q, k, v, layer, bias, batch, seq):
    rows = seq // GRID_W
    assert rows >= NA_ROWS and NA_ROWS % 2 == 0
    blk_tokens = NA_ROW_BLOCK * GRID_W
    nblk = rows // NA_ROW_BLOCK
    assert 2 * NA_HEAD_DIM == LANES
    head_mask = (jnp.arange(LANES)[None, None, :] // NA_HEAD_DIM
                 == jnp.arange(2)[:, None, None]).astype(BF16)
    head_mask = jnp.broadcast_to(head_mask, (2, GRID_W, LANES))
    q_tile = pl.BlockSpec((blk_tokens, NA_WIDTH), lambda b, i: (b * nblk + i, 0))
    kv_spec = pl.BlockSpec((1, seq, NA_WIDTH), lambda b, i: (b, 0, 0))
    return pl.pallas_call(
        functools.partial(_na_kernel, rows=rows),
        grid=(batch, nblk),
        in_specs=[q_tile, kv_spec, kv_spec,
                  _layer_spec((NA_HEADS, 2 * NA_ROWS - 2, GRID_W, 2 * GRID_W), layer),
                  _const_spec((2, GRID_W, LANES))],
        out_specs=q_tile,
        out_shape=jax.ShapeDtypeStruct((batch * seq, NA_WIDTH), BF16),
        compiler_params=_params(56),
        name="na",
    )(q, k.reshape(batch, seq, NA_WIDTH), v.reshape(batch, seq, NA_WIDTH), bias, head_mask)


def _dft_tables():
    two_pi = 2.0 * math.pi
    n = DFT_ROWS * DFT_COLS
    k1 = np.arange(DFT_ROWS)
    ang = ((k1[:, None] * k1[None, :]) % DFT_ROWS) * (two_pi / DFT_ROWS)
    eye_t2 = np.eye(F1_T2_BLOCK)
    f1 = np.concatenate([np.kron(np.cos(ang), eye_t2), np.kron(-np.sin(ang), eye_t2)], axis=0)
    k2 = np.arange(DFT_COLS)
    t2 = np.arange(DFT_COLS)
    ang = ((t2[None, None, :] * (DFT_ROWS * k2[None, :, None] + k1[:, None, None])) % n) * (two_pi / n)
    gr, gi = np.cos(ang), -np.sin(ang)
    g = np.concatenate([np.concatenate([gr, -gi], axis=2),
                        np.concatenate([gi, gr], axis=2)], axis=1)
    c = np.arange(F_GROUP_DIM)
    ang = ((c[:, None] * c[None, :]) % F_GROUP_DIM) * (two_pi / F_GROUP_DIM)
    eye = np.eye(MXU_COLS_V7X // F_GROUP_DIM)
    cc = np.kron(eye, np.cos(ang))
    ss = np.kron(eye, np.sin(ang))
    return tuple(jnp.asarray(t.astype(np.float32)).astype(BF16) for t in (f1, g, cc, ss))


def _fnet_kernel(f_ref, x_ref, g_ref, cc_ref, ss_ref, o_ref, a_ref, zr_ref, zi_ref):
    step = pl.program_id(1)

    @pl.when(step < F1_STEPS)
    def _stage1():
        per_store = BF16_SUBLANES // F1_T2_BLOCK
        base = step * F1_T2_STEP
        for s in range(F1_T2_STEP // BF16_SUBLANES):
            zs = []
            for j in range(per_store):
                lo = (s * per_store + j) * F1_T2_BLOCK
                x = x_ref[0, :, lo:lo + F1_T2_BLOCK, :]
                x = x.reshape(DFT_ROWS * F1_T2_BLOCK, F_WIDTH).astype(BF16)
                z = _dot(f_ref[...], x)
                zs.append(z.reshape(2, DFT_ROWS, F1_T2_BLOCK, F_WIDTH))
            t2 = pl.multiple_of(base + s * BF16_SUBLANES, BF16_SUBLANES)
            a_ref[:, :, pl.ds(t2, BF16_SUBLANES), :] = jnp.concatenate(zs, axis=2).astype(BF16)

    @pl.when(step >= F1_STEPS)
    def _stage2():
        k1 = (step - F1_STEPS) * F3_K1_BLOCK
        for j in range(F3_K1_BLOCK):
            a = jnp.concatenate([a_ref[0, k1 + j], a_ref[1, k1 + j]], axis=0)
            z = _dot(g_ref[j], a)
            zr_ref[j * DFT_COLS:(j + 1) * DFT_COLS] = z[:DFT_COLS].astype(BF16)
            zi_ref[j * DFT_COLS:(j + 1) * DFT_COLS] = z[DFT_COLS:].astype(BF16)
        scale = 1.0 / math.sqrt(DFT_ROWS * DFT_COLS * F_GROUP_DIM)
        for p in range(F_WIDTH // MXU_COLS_V7X):
            cols = slice(p * MXU_COLS_V7X, (p + 1) * MXU_COLS_V7X)
            y = _dot(zr_ref[:, cols], cc_ref[...]) + _dot(zi_ref[:, cols], ss_ref[...])
            y = (y * scale).astype(BF16)
            for j in range(F3_K1_BLOCK):
                o_ref[0, j, :, cols] = y[j * DFT_COLS:(j + 1) * DFT_COLS]


def _fourier(u, tables, batch, seq):
    f1, g, cc, ss = tables
    f1_rows = 2 * DFT_ROWS * F1_T2_BLOCK
    y = pl.pallas_call(
        _fnet_kernel,
        grid=(batch, F1_STEPS + F3_STEPS),
        in_specs=[_const_spec((f1_rows, f1_rows // 2)),
                  pl.BlockSpec((1, DFT_ROWS, F1_T2_STEP, F_WIDTH),
                               lambda b, i: (b, 0, jnp.minimum(i, F1_STEPS - 1), 0)),
                  pl.BlockSpec((F3_K1_BLOCK, 2 * DFT_COLS, 2 * DFT_COLS),
                               lambda b, i: (jnp.maximum(i - F1_STEPS, 0), 0, 0)),
                  _const_spec((MXU_COLS_V7X, MXU_COLS_V7X)), _const_spec((MXU_COLS_V7X, MXU_COLS_V7X))],
        out_specs=pl.BlockSpec((1, F3_K1_BLOCK, DFT_COLS, F_WIDTH),
                               lambda b, i: (b, jnp.maximum(i - F1_STEPS, 0), 0, 0)),
        out_shape=jax.ShapeDtypeStruct((batch, DFT_ROWS, DFT_COLS, F_WIDTH), BF16),
        scratch_shapes=[pltpu.VMEM((2, DFT_ROWS, DFT_COLS, F_WIDTH), BF16),
                        pltpu.VMEM((F3_K1_BLOCK * DFT_COLS, F_WIDTH), BF16),
                        pltpu.VMEM((F3_K1_BLOCK * DFT_COLS, F_WIDTH), BF16)],
        compiler_params=_params(52),
        name="fnet",
    )(f1, u.reshape(batch, DFT_ROWS, DFT_COLS, F_WIDTH), g, cc, ss)
    return jnp.transpose(y, (0, 2, 1, 3)).reshape(batch * seq, F_WIDTH)


def _mixout_kernel(x_ref, g_ref, att_ref, fou_ref, wg_ref, gb_ref, wna_ref, wf_ref, wo_ref,
                   *rest, n_cast):
    cast_src, o_ref, cast_dst = rest[:n_cast], rest[n_cast], rest[n_cast + 1:]
    _cast_blocks(cast_src, cast_dst)
    for t in range(TOKEN_TILE // SUB_TILE):
        rows = slice(t * SUB_TILE, (t + 1) * SUB_TILE)
        x = x_ref[rows, :]
        h = _rms(x, g_ref[...]).astype(BF16)
        gates = jax.nn.sigmoid(_dot(h, wg_ref[...]) + gb_ref[...])
        y_na = _dot(att_ref[rows, :], wna_ref[...])
        y_f = _dot(fou_ref[rows, :], wf_ref[...])
        m = gates[:, :D_MODEL] * y_na + gates[:, D_MODEL:] * y_f
        o_ref[rows, :] = x + _dot(m.astype(BF16), wo_ref[...])


def _mixout(x, layer, g, att, fou, w_in, gate_bias, w_na, w_f, w_o, cast=()):
    n = x.shape[0]
    steps = n // TOKEN_TILE
    tile = pl.BlockSpec((TOKEN_TILE, D_MODEL), lambda i: (i, 0))
    half = pl.BlockSpec((TOKEN_TILE, NA_WIDTH), lambda i: (i, 0))
    assert w_in.shape[1] == 4 * D_MODEL
    cast_in, cast_out, cast_shapes = _cast_specs(cast, steps)
    y, *copies = pl.pallas_call(
        functools.partial(_mixout_kernel, n_cast=len(cast)),
        grid=(steps,),
        in_specs=[tile, _layer_spec((1, D_MODEL), layer), half, half,
                  _const_spec((D_MODEL, 2 * D_MODEL), 1),
                  _layer_spec((1, 2 * D_MODEL), layer),
                  _const_spec((NA_WIDTH, D_MODEL)), _const_spec((F_WIDTH, D_MODEL)),
                  _const_spec((D_MODEL, D_MODEL))] + cast_in,
        out_specs=[tile] + cast_out,
        out_shape=[jax.ShapeDtypeStruct((n, D_MODEL), F32)] + cast_shapes,
        compiler_params=_params(48),
        name="mix_out",
    )(x, g, att, fou, w_in, gate_bias, w_na, w_f, w_o, *[arr for arr, _ in cast])
    return y, copies


def kernel(x, ffn1_norm, ffn1_w_in, ffn1_w_out, mix_norm, mix_w_in, mix_gate_bias, na_rpb,
           na_w_out, f_w_out, mix_w_o, ffn2_norm, ffn2_w_in, ffn2_w_out, final_norm):
    batch, seq, d = x.shape
    depth = ffn1_norm.shape[0]
    assert d == D_MODEL and seq == DFT_ROWS * DFT_COLS and seq % (GRID_W * NA_ROW_BLOCK) == 0
    assert all((batch * seq) % t == 0
               for t in (PROJ_TOKEN_TILE, TOKEN_TILE, FFN_TOKEN_TILE, FFN_TOKEN_TILE_F32W))
    tables = _dft_tables()
    na_bias = _na_bias_tables(na_rpb)
    gain = lambda g: g.reshape(depth, 1, D_MODEL)
    g1, g2, mix_g = gain(ffn1_norm), gain(ffn2_norm), gain(mix_norm)
    gate_bias = mix_gate_bias.reshape(depth, 1, 2 * D_MODEL)
    xs = x.reshape(batch * seq, d)
    ffn1_w = [ffn1_w_in, ffn1_w_out]
    for l in range(depth):
        xs, (w_in, w_na, w_f, w_o) = _ffn(
            xs, l, g1, *ffn1_w,
            cast=[(mix_w_in, l), (na_w_out, l), (f_w_out, l), (mix_w_o, l)])
        q, k, v, u = _proj(xs, l, mix_g, w_in)
        att = _na(q, k, v, l, na_bias, batch, seq)
        fou = _fourier(u, tables, batch, seq)
        xs, ffn2_w = _mixout(xs, l, mix_g, att, fou, w_in, gate_bias, w_na, w_f, w_o,
                             cast=[(ffn2_w_in, l), (ffn2_w_out, l)])
        last = l == depth - 1
        xs, ffn1_w = _ffn(xs, l, g2, *ffn2_w, final_g=final_norm if last else None,
                          cast=[] if last else [(ffn1_w_in, l + 1), (ffn1_w_out, l + 1)])
    return xs.reshape(batch, seq, d)
```

```python
import functools
import math

import jax
import jax.numpy as jnp
import numpy as np
from jax import lax
from jax.experimental import pallas as pl
from jax.experimental.pallas import tpu as pltpu

D_MODEL = 1024
GRID_W = 64
NA_HEADS = 8
NA_HEAD_DIM = 64
NA_WIDTH = NA_HEADS * NA_HEAD_DIM
NA_ROWS = 8
NA_COLS = 16
F_GROUPS = 4
F_GROUP_DIM = 128
F_WIDTH = F_GROUPS * F_GROUP_DIM
D_FF = 2816
RMS_EPS = 1e-6

BF16 = jnp.bfloat16
F32 = jnp.float32

MXU_COLS_V7X = 256
LANES = 128
VMEM_BYTES_V7X = 64 * 1024 * 1024

PROJ_TOKEN_TILE = 2048
TOKEN_TILE = 1024
SUB_TILE = 512
FFN_TOKEN_TILE = 1024
FFN_TOKEN_TILE_F32W = 512
FF_CHUNK = MXU_COLS_V7X
NA_ROW_BLOCK = 16
DFT_ROWS = 64
DFT_COLS = 128
SUBLANES_F32 = 8
BF16_SUBLANES = 16
F1_T2_BLOCK = SUBLANES_F32
F1_T2_STEP = 32
F3_K1_BLOCK = 16
F1_STEPS = DFT_COLS // F1_T2_STEP
F3_STEPS = DFT_ROWS // F3_K1_BLOCK


def _params(vmem_mib):
    assert vmem_mib * 1024 * 1024 <= VMEM_BYTES_V7X
    return pltpu.CompilerParams(
        dimension_semantics=None,
        vmem_limit_bytes=vmem_mib * 1024 * 1024,
    )


def _const_spec(shape, col_block=0):
    index = (0,) * (len(shape) - 1) + (col_block,)
    return pl.BlockSpec(shape, lambda *_: index, pipeline_mode=pl.Buffered(1))


def _cast_specs(params, steps):
    in_specs, out_specs, out_shapes = [], [], []
    for arr, layer in params:
        _, rows, cols = arr.shape
        nblk = steps
        while rows % nblk or (rows // nblk) % BF16_SUBLANES:
            nblk //= 2
        blk = rows // nblk
        in_specs.append(pl.BlockSpec(
            (None, blk, cols), lambda i, layer=layer, nblk=nblk: (layer, jnp.minimum(i, nblk - 1), 0)))
        out_specs.append(pl.BlockSpec(
            (blk, cols), lambda i, nblk=nblk: (jnp.minimum(i, nblk - 1), 0)))
        out_shapes.append(jax.ShapeDtypeStruct((rows, cols), BF16))
    return in_specs, out_specs, out_shapes


def _cast_blocks(src_refs, dst_refs):
    for src, dst in zip(src_refs, dst_refs):
        dst[...] = src[...].astype(BF16)


def _layer_spec(shape, layer):
    index = (layer,) + (0,) * len(shape)
    return pl.BlockSpec((None,) + tuple(shape), lambda *_: index, pipeline_mode=pl.Buffered(1))


def _rms(x, g):
    ms = jnp.mean(x * x, axis=-1, keepdims=True)
    return x * lax.rsqrt(ms + RMS_EPS) * g


def _dot(a, b):
    return jnp.dot(a, b, preferred_element_type=F32)


def _ffn_kernel(x_ref, g_ref, win_ref, wout_ref, *rest, final, n_cast):
    rest = list(rest)
    gf_ref = rest.pop(0) if final else None
    cast_src, o_ref, cast_dst, act_ref = (rest[:n_cast], rest[n_cast],
                                          rest[n_cast + 1:2 * n_cast + 1], rest[-1])
    _cast_blocks(cast_src, cast_dst)
    x = x_ref[...]
    h = _rms(x, g_ref[...]).astype(BF16)
    for c in range(D_FF // FF_CHUNK):
        lo = c * FF_CHUNK
        g = _dot(h, win_ref[:, lo:lo + FF_CHUNK].astype(BF16))
        u = _dot(h, win_ref[:, D_FF + lo:D_FF + lo + FF_CHUNK].astype(BF16))
        act_ref[:, lo:lo + FF_CHUNK] = (g * jax.nn.sigmoid(g) * u).astype(BF16)
    y = x + 0.5 * _dot(act_ref[...], wout_ref[...].astype(BF16))
    if final:
        y = _rms(y, gf_ref[...])
    o_ref[...] = y


def _ffn(x, layer, g, w_in, w_out, final_g=None, cast=()):
    n = x.shape[0]
    f32_weights = w_in.ndim == 3
    rows = FFN_TOKEN_TILE_F32W if f32_weights else FFN_TOKEN_TILE
    steps = n // rows
    final = final_g is not None
    tile = pl.BlockSpec((rows, D_MODEL), lambda i: (i, 0))
    w_spec = functools.partial(_layer_spec, layer=layer) if f32_weights else _const_spec
    in_specs = [tile, _layer_spec((1, D_MODEL), layer), w_spec((D_MODEL, 2 * D_FF)),
                w_spec((D_FF, D_MODEL))]
    args = [x, g, w_in, w_out]
    if final:
        in_specs.append(_const_spec((1, D_MODEL)))
        args.append(final_g.reshape(1, D_MODEL))
    cast_in, cast_out, cast_shapes = _cast_specs(cast, steps)
    y, *copies = pl.pallas_call(
        functools.partial(_ffn_kernel, final=final, n_cast=len(cast)),
        grid=(steps,),
        in_specs=in_specs + cast_in,
        out_specs=[tile] + cast_out,
        out_shape=[jax.ShapeDtypeStruct((n, D_MODEL), F32)] + cast_shapes,
        scratch_shapes=[pltpu.VMEM((rows, D_FF), BF16)],
        compiler_params=_params(58),
        name="ffn",
    )(*args, *[arr for arr, _ in cast])
    return y, copies


def _proj_kernel(x_ref, g_ref, w_ref, q_ref, k_ref, v_ref, u_ref):
    s = NA_WIDTH
    for t in range(PROJ_TOKEN_TILE // SUB_TILE):
        rows = slice(t * SUB_TILE, (t + 1) * SUB_TILE)
        h = _rms(x_ref[rows, :], g_ref[...]).astype(BF16)
        z = _dot(h, w_ref[...])
        q_ref[rows, :] = (z[:, :s] * (NA_HEAD_DIM ** -0.5)).astype(BF16)
        k_ref[rows, :] = z[:, s:2 * s].astype(BF16)
        v_ref[rows, :] = z[:, 2 * s:3 * s].astype(BF16)
        u_ref[rows, :] = z[:, 3 * s:3 * s + F_WIDTH]


def _proj(x, layer, g, w):
    n = x.shape[0]
    cols = 3 * NA_WIDTH + F_WIDTH
    assert w.shape[1] == 2 * cols
    out_tile = pl.BlockSpec((PROJ_TOKEN_TILE, NA_WIDTH), lambda i: (i, 0))
    out_sds = jax.ShapeDtypeStruct((n, NA_WIDTH), BF16)
    return pl.pallas_call(
        _proj_kernel,
        grid=(n // PROJ_TOKEN_TILE,),
        in_specs=[pl.BlockSpec((PROJ_TOKEN_TILE, D_MODEL), lambda i: (i, 0)),
                  _layer_spec((1, D_MODEL), layer), _const_spec((D_MODEL, cols), 0)],
        out_specs=[out_tile] * 4,
        out_shape=[out_sds] * 3 + [jax.ShapeDtypeStruct((n, F_WIDTH), F32)],
        compiler_params=_params(48),
        name="mix_proj",
    )(x, g, w)


def _na_bias_tables(rpb):
    ncol = 2 * NA_COLS - 1
    qc = jnp.arange(GRID_W)
    lane = jnp.arange(2 * GRID_W)
    kc, side = lane % GRID_W, lane // GRID_W
    ws = jnp.clip(qc - NA_COLS // 2, 0, GRID_W - NA_COLS)
    ok = (kc[None, :] >= ws[:, None]) & (kc[None, :] < ws[:, None] + NA_COLS)
    dc = kc[None, :] - qc[:, None] + (NA_COLS - 1)
    src = jnp.where(ok, side[None, :] * ncol + dc, -1)
    pick = (src[:, :, None] == jnp.arange(2 * ncol)[None, None, :]).astype(F32)
    both = jnp.concatenate([rpb[:, :, :-1], rpb[:, :, 1:]], axis=-1).astype(F32)
    table = jnp.einsum('lhdm,qkm->lhdqk', both, pick, precision=lax.Precision.HIGHEST)
    return jnp.where(ok[None, None, None], table, -jnp.inf)


def _na_kernel(q_ref, k_ref, v_ref, bias_ref, hm_ref, o_ref, *, rows):
    blk = pl.program_id(1)
    kh = NA_ROWS
    half = kh // 2
    lane_lo = lax.broadcasted_iota(jnp.int32, (GRID_W, LANES), 1) < NA_HEAD_DIM

    def window(i):
        r = blk * NA_ROW_BLOCK + i
        rs = jnp.clip(r - half, 0, rows - kh)
        dr0 = rs - r + (NA_ROWS - 1)
        return pl.multiple_of(rs * GRID_W, GRID_W), dr0

    def bias(h, dr0):
        return jnp.concatenate([bias_ref[h, dr0 + 2 * j] for j in range(kh // 2)], axis=-1)

    def score_stage(i):
        koff, _ = window(i)
        ss = []
        for p in range(NA_HEADS // 2):
            lanes = slice(p * LANES, (p + 1) * LANES)
            qp = q_ref[i * GRID_W:(i + 1) * GRID_W, lanes]
            ql = jnp.concatenate([qp * hm_ref[0], qp * hm_ref[1]], axis=0)
            kw = k_ref[0, pl.ds(koff, kh * GRID_W), lanes]
            ss.append(lax.dot_general(ql, kw, (((1,), (1,)), ((), ())),
                                      preferred_element_type=F32))
        return ss

    def output_stage(i, ss):
        koff, dr0 = window(i)
        es, ls = [], []
        for p in range(NA_HEADS // 2):
            s = ss[p] + jnp.concatenate([bias(2 * p, dr0), bias(2 * p + 1, dr0)], axis=0)
            m = jnp.max(s, axis=-1, keepdims=True)
            e = jnp.exp((s - m).astype(BF16))
            ls.append(jnp.sum(e.astype(F32), axis=-1, keepdims=True))
            es.append(e)
        for p in range(NA_HEADS // 2):
            lanes = slice(p * LANES, (p + 1) * LANES)
            vw = v_ref[0, pl.ds(koff, kh * GRID_W), lanes]
            pv = _dot(es[p], vw) / ls[p]
            o = jnp.where(lane_lo, pv[:GRID_W], pv[GRID_W:])
            o_ref[i * GRID_W:(i + 1) * GRID_W, lanes] = o.astype(BF16)

    ss = score_stage(0)
    for i in range(NA_ROW_BLOCK):
        nxt = score_stage(i + 1) if i + 1 < NA_ROW_BLOCK else None
        output_stage(i, ss)
        ss = nxt


def _na(q, k, v, layer, bias, batch, seq):
    rows = seq // GRID_W
    assert rows >= NA_ROWS and NA_ROWS % 2 == 0
    blk_tokens = NA_ROW_BLOCK * GRID_W
    nblk = rows // NA_ROW_BLOCK
    assert 2 * NA_HEAD_DIM == LANES
    head_mask = (jnp.arange(LANES)[None, None, :] // NA_HEAD_DIM
                 == jnp.arange(2)[:, None, None]).astype(BF16)
    head_mask = jnp.broadcast_to(head_mask, (2, GRID_W, LANES))
    q_tile = pl.BlockSpec((blk_tokens, NA_WIDTH), lambda b, i: (b * nblk + i, 0))
    kv_spec = pl.BlockSpec((1, seq, NA_WIDTH), lambda b, i: (b, 0, 0),
                           pipeline_mode=pl.Buffered(1))
    return pl.pallas_call(
        functools.partial(_na_kernel, rows=rows),
        grid=(batch, nblk),
        in_specs=[q_tile, kv_spec, kv_spec,
                  _layer_spec((NA_HEADS, 2 * NA_ROWS - 2, GRID_W, 2 * GRID_W), layer),
                  _const_spec((2, GRID_W, LANES))],
        out_specs=q_tile,
        out_shape=jax.ShapeDtypeStruct((batch * seq, NA_WIDTH), BF16),
        compiler_params=_params(40),
        name="na",
    )(q, k.reshape(batch, seq, NA_WIDTH), v.reshape(batch, seq, NA_WIDTH), bias, head_mask)


def _dft_tables():
    two_pi = 2.0 * math.pi
    n = DFT_ROWS * DFT_COLS
    k1 = np.arange(DFT_ROWS)
    ang = ((k1[:, None] * k1[None, :]) % DFT_ROWS) * (two_pi / DFT_ROWS)
    eye_t2 = np.eye(F1_T2_BLOCK)
    f1 = np.concatenate([np.kron(np.cos(ang), eye_t2), np.kron(-np.sin(ang), eye_t2)], axis=0)
    k2 = np.arange(DFT_COLS)
    t2 = np.arange(DFT_COLS)
    ang = ((t2[None, None, :] * (DFT_ROWS * k2[None, :, None] + k1[:, None, None])) % n) * (two_pi / n)
    gr, gi = np.cos(ang), -np.sin(ang)
    g = np.concatenate([np.concatenate([gr, -gi], axis=2),
                        np.concatenate([gi, gr], axis=2)], axis=1)
    c = np.arange(F_GROUP_DIM)
    ang = ((c[:, None] * c[None, :]) % F_GROUP_DIM) * (two_pi / F_GROUP_DIM)
    eye = np.eye(MXU_COLS_V7X // F_GROUP_DIM)
    cc = np.kron(eye, np.cos(ang))
    ss = np.kron(eye, np.sin(ang))
    return tuple(jnp.asarray(t.astype(np.float32)).astype(BF16) for t in (f1, g, cc, ss))


def _fnet_kernel(f_ref, x_ref, g_ref, cc_ref, ss_ref, o_ref, a_ref, zr_ref, zi_ref):
    step = pl.program_id(1)

    @pl.when(step < F1_STEPS)
    def _stage1():
        per_store = BF16_SUBLANES // F1_T2_BLOCK
        base = step * F1_T2_STEP
        for s in range(F1_T2_STEP // BF16_SUBLANES):
            zs = []
            for j in range(per_store):
                lo = (s * per_store + j) * F1_T2_BLOCK
                x = x_ref[0, :, lo:lo + F1_T2_BLOCK, :]
                x = x.reshape(DFT_ROWS * F1_T2_BLOCK, F_WIDTH).astype(BF16)
                z = _dot(f_ref[...], x)
                zs.append(z.reshape(2, DFT_ROWS, F1_T2_BLOCK, F_WIDTH))
            t2 = pl.multiple_of(base + s * BF16_SUBLANES, BF16_SUBLANES)
            a_ref[:, :, pl.ds(t2, BF16_SUBLANES), :] = jnp.concatenate(zs, axis=2).astype(BF16)

    @pl.when(step >= F1_STEPS)
    def _stage2():
        k1 = (step - F1_STEPS) * F3_K1_BLOCK
        for j in range(F3_K1_BLOCK):
            a = jnp.concatenate([a_ref[0, k1 + j], a_ref[1, k1 + j]], axis=0)
            z = _dot(g_ref[j], a)
            zr_ref[j * DFT_COLS:(j + 1) * DFT_COLS] = z[:DFT_COLS].astype(BF16)
            zi_ref[j * DFT_COLS:(j + 1) * DFT_COLS] = z[DFT_COLS:].astype(BF16)
        scale = 1.0 / math.sqrt(DFT_ROWS * DFT_COLS * F_GROUP_DIM)
        for p in range(F_WIDTH // MXU_COLS_V7X):
            cols = slice(p * MXU_COLS_V7X, (p + 1) * MXU_COLS_V7X)
            y = _dot(zr_ref[:, cols], cc_ref[...]) + _dot(zi_ref[:, cols], ss_ref[...])
            y = (y * scale).astype(BF16)
            for j in range(F3_K1_BLOCK):
                o_ref[0, j, :, cols] = y[j * DFT_COLS:(j + 1) * DFT_COLS]


def _fourier(u, tables, batch, seq):
    f1, g, cc, ss = tables
    f1_rows = 2 * DFT_ROWS * F1_T2_BLOCK
    y = pl.pallas_call(
        _fnet_kernel,
        grid=(batch, F1_STEPS + F3_STEPS),
        in_specs=[_const_spec((f1_rows, f1_rows // 2)),
                  pl.BlockSpec((1, DFT_ROWS, F1_T2_STEP, F_WIDTH),
                               lambda b, i: (b, 0, jnp.minimum(i, F1_STEPS - 1), 0)),
                  pl.BlockSpec((F3_K1_BLOCK, 2 * DFT_COLS, 2 * DFT_COLS),
                               lambda b, i: (jnp.maximum(i - F1_STEPS, 0), 0, 0)),
                  _const_spec((MXU_COLS_V7X, MXU_COLS_V7X)), _const_spec((MXU_COLS_V7X, MXU_COLS_V7X))],
        out_specs=pl.BlockSpec((1, F3_K1_BLOCK, DFT_COLS, F_WIDTH),
                               lambda b, i: (b, jnp.maximum(i - F1_STEPS, 0), 0, 0)),
        out_shape=jax.ShapeDtypeStruct((batch, DFT_ROWS, DFT_COLS, F_WIDTH), BF16),
        scratch_shapes=[pltpu.VMEM((2, DFT_ROWS, DFT_COLS, F_WIDTH), BF16),
                        pltpu.VMEM((F3_K1_BLOCK * DFT_COLS, F_WIDTH), BF16),
                        pltpu.VMEM((F3_K1_BLOCK * DFT_COLS, F_WIDTH), BF16)],
        compiler_params=_params(52),
        name="fnet",
    )(f1, u.reshape(batch, DFT_ROWS, DFT_COLS, F_WIDTH), g, cc, ss)
    return jnp.transpose(y, (0, 2, 1, 3)).reshape(batch * seq, F_WIDTH)


def _mixout_kernel(x_ref, g_ref, att_ref, fou_ref, wg_ref, gb_ref, wna_ref, wf_ref, wo_ref,
                   *rest, n_cast):
    cast_src, o_ref, cast_dst = rest[:n_cast], rest[n_cast], rest[n_cast + 1:]
    _cast_blocks(cast_src, cast_dst)
    for t in range(TOKEN_TILE // SUB_TILE):
        rows = slice(t * SUB_TILE, (t + 1) * SUB_TILE)
        x = x_ref[rows, :]
        h = _rms(x, g_ref[...]).astype(BF16)
        gates = jax.nn.sigmoid(_dot(h, wg_ref[...]) + gb_ref[...])
        y_na = _dot(att_ref[rows, :], wna_ref[...])
        y_f = _dot(fou_ref[rows, :], wf_ref[...])
        m = gates[:, :D_MODEL] * y_na + gates[:, D_MODEL:] * y_f
        o_ref[rows, :] = x + _dot(m.astype(BF16), wo_ref[...])


def _mixout(x, layer, g, att, fou, w_in, gate_bias, w_na, w_f, w_o, cast=()):
    n = x.shape[0]
    steps = n // TOKEN_TILE
    tile = pl.BlockSpec((TOKEN_TILE, D_MODEL), lambda i: (i, 0))
    half = pl.BlockSpec((TOKEN_TILE, NA_WIDTH), lambda i: (i, 0))
    assert w_in.shape[1] == 4 * D_MODEL
    cast_in, cast_out, cast_shapes = _cast_specs(cast, steps)
    y, *copies = pl.pallas_call(
        functools.partial(_mixout_kernel, n_cast=len(cast)),
        grid=(steps,),
        in_specs=[tile, _layer_spec((1, D_MODEL), layer), half, half,
                  _const_spec((D_MODEL, 2 * D_MODEL), 1),
                  _layer_spec((1, 2 * D_MODEL), layer),
                  _const_spec((NA_WIDTH, D_MODEL)), _const_spec((F_WIDTH, D_MODEL)),
                  _const_spec((D_MODEL, D_MODEL))] + cast_in,
        out_specs=[tile] + cast_out,
        out_shape=[jax.ShapeDtypeStruct((n, D_MODEL), F32)] + cast_shapes,
        compiler_params=_params(48),
        name="mix_out",
    )(x, g, att, fou, w_in, gate_bias, w_na, w_f, w_o, *[arr for arr, _ in cast])
    return y, copies


def kernel(x, ffn1_norm, ffn1_w_in, ffn1_w_out, mix_norm, mix_w_in, mix_gate_bias, na_rpb,
           na_w_out, f_w_out, mix_w_o, ffn2_norm, ffn2_w_in, ffn2_w_out, final_norm):
    batch, seq, d = x.shape
    depth = ffn1_norm.shape[0]
    assert d == D_MODEL and seq == DFT_ROWS * DFT_COLS and seq % (GRID_W * NA_ROW_BLOCK) == 0
    assert all((batch * seq) % t == 0
               for t in (PROJ_TOKEN_TILE, TOKEN_TILE, FFN_TOKEN_TILE, FFN_TOKEN_TILE_F32W))
    tables = _dft_tables()
    na_bias = _na_bias_tables(na_rpb)
    gain = lambda g: g.reshape(depth, 1, D_MODEL)
    g1, g2, mix_g = gain(ffn1_norm), gain(ffn2_norm), gain(mix_norm)
    gate_bias = mix_gate_bias.reshape(depth, 1, 2 * D_MODEL)
    xs = x.reshape(batch * seq, d)
    ffn1_w = [ffn1_w_in, ffn1_w_out]
    for l in range(depth):
        xs, (w_in, w_na, w_f, w_o) = _ffn(
            xs, l, g1, *ffn1_w,
            cast=[(mix_w_in, l), (na_w_out, l), (f_w_out, l), (mix_w_o, l)])
        q, k, v, u = _proj(xs, l, mix_g, w_in)
        att = _na(q, k, v, l, na_bias, batch, seq)
        fou = _fourier(u, tables, batch, seq)
        xs, ffn2_w = _mixout(xs, l, mix_g, att, fou, w_in, gate_bias, w_na, w_f, w_o,
                             cast=[(ffn2_w_in, l), (ffn2_w_out, l)])
        last = l == depth - 1
        xs, ffn1_w = _ffn(xs, l, g2, *ffn2_w, final_g=final_norm if last else None,
                          cast=[] if last else [(ffn1_w_in, l + 1), (ffn1_w_out, l + 1)])
    return xs.reshape(batch, seq, d)
```

```python
import functools
import math

import jax
import jax.numpy as jnp
import numpy as np
from jax import lax
from jax.experimental import pallas as pl
from jax.experimental.pallas import tpu as pltpu

D_MODEL = 1024
GRID_W = 64
NA_HEADS = 8
NA_HEAD_DIM = 64
NA_WIDTH = NA_HEADS * NA_HEAD_DIM
NA_ROWS = 8
NA_COLS = 16
F_GROUPS = 4
F_GROUP_DIM = 128
F_WIDTH = F_GROUPS * F_GROUP_DIM
D_FF = 2816
RMS_EPS = 1e-6

BF16 = jnp.bfloat16
F32 = jnp.float32

MXU_COLS_V7X = 256
LANES = 128
VMEM_BYTES_V7X = 64 * 1024 * 1024

PROJ_TOKEN_TILE = 2048
TOKEN_TILE = 1024
SUB_TILE = 512
FFN_TOKEN_TILE = 1024
FFN_TOKEN_TILE_F32W = 512
FF_CHUNK = MXU_COLS_V7X
NA_ROW_BLOCK = 16
DFT_ROWS = 64
DFT_COLS = 128
SUBLANES_F32 = 8
BF16_SUBLANES = 16
F1_T2_BLOCK = SUBLANES_F32
F1_T2_STEP = 32
F3_K1_BLOCK = 16
F1_STEPS = DFT_COLS // F1_T2_STEP
F3_STEPS = DFT_ROWS // F3_K1_BLOCK


def _params(vmem_mib):
    assert vmem_mib * 1024 * 1024 <= VMEM_BYTES_V7X
    return pltpu.CompilerParams(
        dimension_semantics=None,
        vmem_limit_bytes=vmem_mib * 1024 * 1024,
    )


def _const_spec(shape, col_block=0):
    index = (0,) * (len(shape) - 1) + (col_block,)
    return pl.BlockSpec(shape, lambda *_: index, pipeline_mode=pl.Buffered(1))


def _cast_specs(params, steps):
    in_specs, out_specs, out_shapes = [], [], []
    for arr, layer in params:
        _, rows, cols = arr.shape
        nblk = steps
        while rows % nblk or (rows // nblk) % BF16_SUBLANES:
            nblk //= 2
        blk = rows // nblk
        in_specs.append(pl.BlockSpec(
            (None, blk, cols), lambda i, layer=layer, nblk=nblk: (layer, jnp.minimum(i, nblk - 1), 0)))
        out_specs.append(pl.BlockSpec(
            (blk, cols), lambda i, nblk=nblk: (jnp.minimum(i, nblk - 1), 0)))
        out_shapes.append(jax.ShapeDtypeStruct((rows, cols), BF16))
    return in_specs, out_specs, out_shapes


def _cast_blocks(src_refs, dst_refs):
    for src, dst in zip(src_refs, dst_refs):
        dst[...] = src[...].astype(BF16)


def _layer_spec(shape, layer):
    index = (layer,) + (0,) * len(shape)
    return pl.BlockSpec((None,) + tuple(shape), lambda *_: index, pipeline_mode=pl.Buffered(1))


def _rms(x, g):
    ms = jnp.mean(x * x, axis=-1, keepdims=True)
    return x * lax.rsqrt(ms + RMS_EPS) * g


def _dot(a, b):
    return jnp.dot(a, b, preferred_element_type=F32)


def _ffn_kernel(x_ref, g_ref, win_ref, wout_ref, *rest, final, n_cast):
    rest = list(rest)
    gf_ref = rest.pop(0) if final else None
    cast_src, o_ref, cast_dst, act_ref = (rest[:n_cast], rest[n_cast],
                                          rest[n_cast + 1:2 * n_cast + 1], rest[-1])
    _cast_blocks(cast_src, cast_dst)
    x = x_ref[...]
    h = _rms(x, g_ref[...]).astype(BF16)
    for c in range(D_FF // FF_CHUNK):
        lo = c * FF_CHUNK
        g = _dot(h, win_ref[:, lo:lo + FF_CHUNK].astype(BF16))
        u = _dot(h, win_ref[:, D_FF + lo:D_FF + lo + FF_CHUNK].astype(BF16))
        act_ref[:, lo:lo + FF_CHUNK] = (g * jax.nn.sigmoid(g) * u).astype(BF16)
    y = x + 0.5 * _dot(act_ref[...], wout_ref[...].astype(BF16))
    if final:
        y = _rms(y, gf_ref[...])
    o_ref[...] = y


def _ffn(x, layer, g, w_in, w_out, final_g=None, cast=()):
    n = x.shape[0]
    f32_weights = w_in.ndim == 3
    rows = FFN_TOKEN_TILE_F32W if f32_weights else FFN_TOKEN_TILE
    steps = n // rows
    final = final_g is not None
    tile = pl.BlockSpec((rows, D_MODEL), lambda i: (i, 0))
    w_spec = functools.partial(_layer_spec, layer=layer) if f32_weights else _const_spec
    in_specs = [tile, _layer_spec((1, D_MODEL), layer), w_spec((D_MODEL, 2 * D_FF)),
                w_spec((D_FF, D_MODEL))]
    args = [x, g, w_in, w_out]
    if final:
        in_specs.append(_const_spec((1, D_MODEL)))
        args.append(final_g.reshape(1, D_MODEL))
    cast_in, cast_out, cast_shapes = _cast_specs(cast, steps)
    y, *copies = pl.pallas_call(
        functools.partial(_ffn_kernel, final=final, n_cast=len(cast)),
        grid=(steps,),
        in_specs=in_specs + cast_in,
        out_specs=[tile] + cast_out,
        out_shape=[jax.ShapeDtypeStruct((n, D_MODEL), F32)] + cast_shapes,
        scratch_shapes=[pltpu.VMEM((rows, D_FF), BF16)],
        compiler_params=_params(58),
        name="ffn",
    )(*args, *[arr for arr, _ in cast])
    return y, copies


def _proj_kernel(x_ref, g_ref, w_ref, q_ref, k_ref, v_ref, u_ref):
    s = NA_WIDTH
    for t in range(PROJ_TOKEN_TILE // SUB_TILE):
        rows = slice(t * SUB_TILE, (t + 1) * SUB_TILE)
        h = _rms(x_ref[rows, :], g_ref[...]).astype(BF16)
        z = _dot(h, w_ref[...])
        q_ref[rows, :] = (z[:, :s] * (NA_HEAD_DIM ** -0.5)).astype(BF16)
        k_ref[rows, :] = z[:, s:2 * s].astype(BF16)
        v_ref[rows, :] = z[:, 2 * s:3 * s].astype(BF16)
        u_ref[rows, :] = z[:, 3 * s:3 * s + F_WIDTH]


def _proj(x, layer, g, w):
    n = x.shape[0]
    cols = 3 * NA_WIDTH + F_WIDTH
    assert w.shape[1] == 2 * cols
    out_tile = pl.BlockSpec((PROJ_TOKEN_TILE, NA_WIDTH), lambda i: (i, 0))
    out_sds = jax.ShapeDtypeStruct((n, NA_WIDTH), BF16)
    return pl.pallas_call(
        _proj_kernel,
        grid=(n // PROJ_TOKEN_TILE,),
        in_specs=[pl.BlockSpec((PROJ_TOKEN_TILE, D_MODEL), lambda i: (i, 0)),
                  _layer_spec((1, D_MODEL), layer), _const_spec((D_MODEL, cols), 0)],
        out_specs=[out_tile] * 4,
        out_shape=[out_sds] * 3 + [jax.ShapeDtypeStruct((n, F_WIDTH), F32)],
        compiler_params=_params(48),
        name="mix_proj",
    )(x, g, w)


def _na_bias_tables(rpb):
    ncol = 2 * NA_COLS - 1
    qc = jnp.arange(GRID_W)
    lane = jnp.arange(2 * GRID_W)
    kc, side = lane % GRID_W, lane // GRID_W
    ws = jnp.clip(qc - NA_COLS // 2, 0, GRID_W - NA_COLS)
    ok = (kc[None, :] >= ws[:, None]) & (kc[None, :] < ws[:, None] + NA_COLS)
    dc = kc[None, :] - qc[:, None] + (NA_COLS - 1)
    src = jnp.where(ok, side[None, :] * ncol + dc, -1)
    pick = (src[:, :, None] == jnp.arange(2 * ncol)[None, None, :]).astype(F32)
    both = jnp.concatenate([rpb[:, :, :-1], rpb[:, :, 1:]], axis=-1).astype(F32)
    table = jnp.einsum('lhdm,qkm->lhdqk', both, pick, precision=lax.Precision.HIGHEST)
    return jnp.where(ok[None, None, None], table, -jnp.inf)


def _na_kernel(q_ref, k_ref, v_ref, bias_ref, hm_ref, o_ref, *, rows):
    blk = pl.program_id(1)
    kh = NA_ROWS
    half = kh // 2
    lane_lo = lax.broadcasted_iota(jnp.int32, (GRID_W, LANES), 1) < NA_HEAD_DIM

    def window(i):
        r = blk * NA_ROW_BLOCK + i
        rs = jnp.clip(r - half, 0, rows - kh)
        dr0 = rs - r + (NA_ROWS - 1)
        return pl.multiple_of(rs * GRID_W, GRID_W), dr0

    def bias(h, dr0):
        return jnp.concatenate([bias_ref[h, dr0 + 2 * j] for j in range(kh // 2)], axis=-1)

    def score_stage(i):
        koff, _ = window(i)
        ss = []
        for p in range(NA_HEADS // 2):
            lanes = slice(p * LANES, (p + 1) * LANES)
            qp = q_ref[i * GRID_W:(i + 1) * GRID_W, lanes]
            ql = jnp.concatenate([qp * hm_ref[0], qp * hm_ref[1]], axis=0)
            kw = k_ref[0, pl.ds(koff, kh * GRID_W), lanes]
            ss.append(lax.dot_general(ql, kw, (((1,), (1,)), ((), ())),
                                      preferred_element_type=F32))
        return ss

    def output_stage(i, ss):
        koff, dr0 = window(i)
        es, ls = [], []
        for p in range(NA_HEADS // 2):
            s = ss[p] + jnp.concatenate([bias(2 * p, dr0), bias(2 * p + 1, dr0)], axis=0)
            m = jnp.max(s, axis=-1, keepdims=True)
            e = jnp.exp((s - m).astype(BF16))
            ls.append(jnp.sum(e.astype(F32), axis=-1, keepdims=True))
            es.append(e)
        for p in range(NA_HEADS // 2):
            lanes = slice(p * LANES, (p + 1) * LANES)
            vw = v_ref[0, pl.ds(koff, kh * GRID_W), lanes]
            pv = _dot(es[p], vw) / ls[p]
            o = jnp.where(lane_lo, pv[:GRID_W], pv[GRID_W:])
            o_ref[i * GRID_W:(i + 1) * GRID_W, lanes] = o.astype(BF16)

    ss = score_stage(0)
    for i in range(NA_ROW_BLOCK):
        nxt = score_stage(i + 1) if i + 1 < NA_ROW_BLOCK else None
        output_stage(i, ss)
        ss = nxt


def _na(q, k, v, layer, bias, batch, seq):
    rows = seq // GRID_W
    assert rows >= NA_ROWS and NA_ROWS % 2 == 0
    blk_tokens = NA_ROW_BLOCK * GRID_W
    nblk = rows // NA_ROW_BLOCK
    assert 2 * NA_HEAD_DIM == LANES
    head_mask = (jnp.arange(LANES)[None, None, :] // NA_HEAD_DIM
                 == jnp.arange(2)[:, None, None]).astype(BF16)
    head_mask = jnp.broadcast_to(head_mask, (2, GRID_W, LANES))
    q_tile = pl.BlockSpec((blk_tokens, NA_WIDTH), lambda b, i: (b * nblk + i, 0))
    kv_spec = pl.BlockSpec((1, seq, NA_WIDTH), lambda b, i: (b, 0, 0))
    return pl.pallas_call(
        functools.partial(_na_kernel, rows=rows),
        grid=(batch, nblk),
        in_specs=[q_tile, kv_spec, kv_spec,
                  _layer_spec((NA_HEADS, 2 * NA_ROWS - 2, GRID_W, 2 * GRID_W), layer),
                  _const_spec((2, GRID_W, LANES))],
        out_specs=q_tile,
        out_shape=jax.ShapeDtypeStruct((batch * seq, NA_WIDTH), BF16),
        compiler_params=_params(56),
        name="na",
    )(q, k.reshape(batch, seq, NA_WIDTH), v.reshape(batch, seq, NA_WIDTH), bias, head_mask)


def _dft_tables():
    two_pi = 2.0 * math.pi
    n = DFT_ROWS * DFT_COLS
    k1 = np.arange(DFT_ROWS)
    ang = ((k1[:, None] * k1[None, :]) % DFT_ROWS) * (two_pi / DFT_ROWS)
    eye_t2 = np.eye(F1_T2_BLOCK)
    f1 = np.concatenate([np.kron(np.cos(ang), eye_t2), np.kron(-np.sin(ang), eye_t2)], axis=0)
    k2 = np.arange(DFT_COLS)
    t2 = np.arange(DFT_COLS)
    ang = ((t2[None, None, :] * (DFT_ROWS * k2[None, :, None] + k1[:, None, None])) % n) * (two_pi / n)
    gr, gi = np.cos(ang), -np.sin(ang)
    g = np.concatenate([np.concatenate([gr, -gi], axis=2),
                        np.concatenate([gi, gr], axis=2)], axis=1)
    c = np.arange(F_GROUP_DIM)
    ang = ((c[:, None] * c[None, :]) % F_GROUP_DIM) * (two_pi / F_GROUP_DIM)
    eye = np.eye(MXU_COLS_V7X // F_GROUP_DIM)
    cc = np.kron(eye, np.cos(ang))
    ss = np.kron(eye, np.sin(ang))
    return tuple(jnp.asarray(t.astype(np.float32)).astype(BF16) for t in (f1, g, cc, ss))


def _fnet_kernel(f_ref, x_ref, g_ref, cc_ref, ss_ref, o_ref, a_ref, zr_ref, zi_ref):
    step = pl.program_id(1)

    @pl.when(step < F1_STEPS)
    def _stage1():
        per_store = BF16_SUBLANES // F1_T2_BLOCK
        base = step * F1_T2_STEP
        for s in range(F1_T2_STEP // BF16_SUBLANES):
            zs = []
            for j in range(per_store):
                lo = (s * per_store + j) * F1_T2_BLOCK
                x = x_ref[0, :, lo:lo + F1_T2_BLOCK, :]
                x = x.reshape(DFT_ROWS * F1_T2_BLOCK, F_WIDTH).astype(BF16)
                z = _dot(f_ref[...], x)
                zs.append(z.reshape(2, DFT_ROWS, F1_T2_BLOCK, F_WIDTH))
            t2 = pl.multiple_of(base + s * BF16_SUBLANES, BF16_SUBLANES)
            a_ref[:, :, pl.ds(t2, BF16_SUBLANES), :] = jnp.concatenate(zs, axis=2).astype(BF16)

    @pl.when(step >= F1_STEPS)
    def _stage2():
        k1 = (step - F1_STEPS) * F3_K1_BLOCK
        for j in range(F3_K1_BLOCK):
            a = jnp.concatenate([a_ref[0, k1 + j], a_ref[1, k1 + j]], axis=0)
            z = _dot(g_ref[k1 + j], a)
            zr_ref[j * DFT_COLS:(j + 1) * DFT_COLS] = z[:DFT_COLS].astype(BF16)
            zi_ref[j * DFT_COLS:(j + 1) * DFT_COLS] = z[DFT_COLS:].astype(BF16)
        scale = 1.0 / math.sqrt(DFT_ROWS * DFT_COLS * F_GROUP_DIM)
        for p in range(F_WIDTH // MXU_COLS_V7X):
            cols = slice(p * MXU_COLS_V7X, (p + 1) * MXU_COLS_V7X)
            y = _dot(zr_ref[:, cols], cc_ref[...]) + _dot(zi_ref[:, cols], ss_ref[...])
            y = (y * scale).astype(BF16)
            for j in range(F3_K1_BLOCK):
                o_ref[0, j, :, cols] = y[j * DFT_COLS:(j + 1) * DFT_COLS]


def _fourier(u, tables, batch, seq):
    f1, g, cc, ss = tables
    f1_rows = 2 * DFT_ROWS * F1_T2_BLOCK
    y = pl.pallas_call(
        _fnet_kernel,
        grid=(batch, F1_STEPS + F3_STEPS),
        in_specs=[_const_spec((f1_rows, f1_rows // 2)),
                  pl.BlockSpec((1, DFT_ROWS, F1_T2_STEP, F_WIDTH),
                               lambda b, i: (b, 0, jnp.minimum(i, F1_STEPS - 1), 0)),
                  _const_spec((DFT_ROWS, 2 * DFT_COLS, 2 * DFT_COLS)),
                  _const_spec((MXU_COLS_V7X, MXU_COLS_V7X)), _const_spec((MXU_COLS_V7X, MXU_COLS_V7X))],
        out_specs=pl.BlockSpec((1, F3_K1_BLOCK, DFT_COLS, F_WIDTH),
                               lambda b, i: (b, jnp.maximum(i - F1_STEPS, 0), 0, 0)),
        out_shape=jax.ShapeDtypeStruct((batch, DFT_ROWS, DFT_COLS, F_WIDTH), BF16),
        scratch_shapes=[pltpu.VMEM((2, DFT_ROWS, DFT_COLS, F_WIDTH), BF16),
                        pltpu.VMEM((F3_K1_BLOCK * DFT_COLS, F_WIDTH), BF16),
                        pltpu.VMEM((F3_K1_BLOCK * DFT_COLS, F_WIDTH), BF16)],
        compiler_params=_params(52),
        name="fnet",
    )(f1, u.reshape(batch, DFT_ROWS, DFT_COLS, F_WIDTH), g, cc, ss)
    return jnp.transpose(y, (0, 2, 1, 3)).reshape(batch * seq, F_WIDTH)


def _mixout_kernel(x_ref, g_ref, att_ref, fou_ref, wg_ref, gb_ref, wna_ref, wf_ref, wo_ref,
                   *rest, n_cast):
    cast_src, o_ref, cast_dst = rest[:n_cast], rest[n_cast], rest[n_cast + 1:]
    _cast_blocks(cast_src, cast_dst)
    for t in range(TOKEN_TILE // SUB_TILE):
        rows = slice(t * SUB_TILE, (t + 1) * SUB_TILE)
        x = x_ref[rows, :]
        h = _rms(x, g_ref[...]).astype(BF16)
        gates = jax.nn.sigmoid(_dot(h, wg_ref[...]) + gb_ref[...])
        y_na = _dot(att_ref[rows, :], wna_ref[...])
        y_f = _dot(fou_ref[rows, :], wf_ref[...])
        m = gates[:, :D_MODEL] * y_na + gates[:, D_MODEL:] * y_f
        o_ref[rows, :] = x + _dot(m.astype(BF16), wo_ref[...])


def _mixout(x, layer, g, att, fou, w_in, gate_bias, w_na, w_f, w_o, cast=()):
    n = x.shape[0]
    steps = n // TOKEN_TILE
    tile = pl.BlockSpec((TOKEN_TILE, D_MODEL), lambda i: (i, 0))
    half = pl.BlockSpec((TOKEN_TILE, NA_WIDTH), lambda i: (i, 0))
    assert w_in.shape[1] == 4 * D_MODEL
    cast_in, cast_out, cast_shapes = _cast_specs(cast, steps)
    y, *copies = pl.pallas_call(
        functools.partial(_mixout_kernel, n_cast=len(cast)),
        grid=(steps,),
        in_specs=[tile, _layer_spec((1, D_MODEL), layer), half, half,
                  _const_spec((D_MODEL, 2 * D_MODEL), 1),
                  _layer_spec((1, 2 * D_MODEL), layer),
                  _const_spec((NA_WIDTH, D_MODEL)), _const_spec((F_WIDTH, D_MODEL)),
                  _const_spec((D_MODEL, D_MODEL))] + cast_in,
        out_specs=[tile] + cast_out,
        out_shape=[jax.ShapeDtypeStruct((n, D_MODEL), F32)] + cast_shapes,
        compiler_params=_params(48),
        name="mix_out",
    )(x, g, att, fou, w_in, gate_bias, w_na, w_f, w_o, *[arr for arr, _ in cast])
    return y, copies


def kernel(x, ffn1_norm, ffn1_w_in, ffn1_w_out, mix_norm, mix_w_in, mix_gate_bias, na_rpb,
           na_w_out, f_w_out, mix_w_o, ffn2_norm, ffn2_w_in, ffn2_w_out, final_norm):
    batch, seq, d = x.shape
    depth = ffn1_norm.shape[0]
    assert d == D_MODEL and seq == DFT_ROWS * DFT_COLS and seq % (GRID_W * NA_ROW_BLOCK) == 0
    assert all((batch * seq) % t == 0
               for t in (PROJ_TOKEN_TILE, TOKEN_TILE, FFN_TOKEN_TILE, FFN_TOKEN_TILE_F32W))
    tables = _dft_tables()
    na_bias = _na_bias_tables(na_rpb)
    gain = lambda g: g.reshape(depth, 1, D_MODEL)
    g1, g2, mix_g = gain(ffn1_norm), gain(ffn2_norm), gain(mix_norm)
    gate_bias = mix_gate_bias.reshape(depth, 1, 2 * D_MODEL)
    xs = x.reshape(batch * seq, d)
    ffn1_w = [ffn1_w_in, ffn1_w_out]
    for l in range(depth):
        xs, (w_in, w_na, w_f, w_o) = _ffn(
            xs, l, g1, *ffn1_w,
            cast=[(mix_w_in, l), (na_w_out, l), (f_w_out, l), (mix_w_o, l)])
        q, k, v, u = _proj(xs, l, mix_g, w_in)
        att = _na(q, k, v, l, na_bias, batch, seq)
        fou = _fourier(u, tables, batch, seq)
        xs, ffn2_w = _mixout(xs, l, mix_g, att, fou, w_in, gate_bias, w_na, w_f, w_o,
                             cast=[(ffn2_w_in, l), (ffn2_w_out, l)])
        last = l == depth - 1
        xs, ffn1_w = _ffn(xs, l, g2, *ffn2_w, final_g=final_norm if last else None,
                          cast=[] if last else [(ffn1_w_in, l + 1), (ffn1_w_out, l + 1)])
    return xs.reshape(batch, seq, d)
```
